```python
import math
import jax, jax.numpy as jnp
from jax import lax
import numpy as np

D_MODEL = 2048
BATCH = 4
SEQ = 2048
DEPTH = 1
DEC_BATCH = 128
DEC_SEQ = 8
PAST_LEN = 2048
PAGE_SIZE = 128

MIX_WIDTH = D_MODEL
ATTN_WIDTH = MIX_WIDTH // 2
POOL_WIDTH = MIX_WIDTH - ATTN_WIDTH
N_HEADS = 8
HEAD_DIM = ATTN_WIDTH // (2 * N_HEADS)
V_DIM = 2 * HEAD_DIM
ROT_DIM = HEAD_DIM // 4
ROPE_THETA = 500000.0
POOL_WINDOWS = (2, 4, 8, 16)
N_POOL_GROUPS = len(POOL_WINDOWS)
POOL_GROUP_WIDTH = POOL_WIDTH // N_POOL_GROUPS
POOL_HIST = max(POOL_WINDOWS) - 1
IN_WIDTH = 3 * ATTN_WIDTH + POOL_WIDTH
D_FF = 4 * D_MODEL
N_MOD = 6
Q_BLOCK = 128
EPS = 1e-6

kernel_name = "hymba_diffattn_pool_decoder_step"


def rms_norm(x, g):
    xf = x.astype(jnp.float32)
    y = xf * lax.rsqrt(jnp.mean(xf * xf, axis=-1, keepdims=True) + EPS)
    return (y * g.astype(jnp.float32)).astype(x.dtype)


def rotary(x, pos):
    inv = 1.0 / (ROPE_THETA ** (jnp.arange(0, ROT_DIM, 2, dtype=jnp.float32) / ROT_DIM))
    ang = pos.astype(jnp.float32)[:, None] * inv[None, :]
    cos = jnp.cos(ang)[:, None, None, :].astype(x.dtype)
    sin = jnp.sin(ang)[:, None, None, :].astype(x.dtype)
    half = ROT_DIM // 2
    x1 = x[..., :half]
    x2 = x[..., half:ROT_DIM]
    return jnp.concatenate([x1 * cos - x2 * sin, x2 * cos + x1 * sin, x[..., ROT_DIM:]], axis=-1)


def ada_modulation(c, w_ada, b_ada):
    m = jax.nn.silu(c) @ w_ada + b_ada
    return jnp.split(m[:, None, :], N_MOD, axis=-1)


def mixer_inputs(x, shift, scale, g_pre, w_in, pos):
    b, t = x.shape[:2]
    h = rms_norm(x, g_pre) * (1.0 + scale) + shift
    z = h @ w_in
    q, k, v, u = jnp.split(z, [ATTN_WIDTH, 2 * ATTN_WIDTH, 3 * ATTN_WIDTH], axis=-1)
    q = rotary(q.reshape(b, t, N_HEADS, 2, HEAD_DIM), pos)
    k = rotary(k.reshape(b, t, N_HEADS, 2, HEAD_DIM), pos)
    v = v.reshape(b, t, N_HEADS, V_DIM)
    return q, k, v, u


def diff_attend(q, k, v, q_pos, k_pos, lam):
    s = jnp.einsum('bqhmd,bkhmd->bhmqk', q, k).astype(jnp.float32) * (HEAD_DIM ** -0.5)
    mask = k_pos[None, :] <= q_pos[:, None]
    p = jax.nn.softmax(jnp.where(mask, s, -jnp.inf), axis=-1)
    a = p[:, :, 0] - lam * p[:, :, 1]
    return jnp.einsum('bhqk,bkhe->bqhe', a.astype(v.dtype), v)


def prompt_attention(q, k, v, pos, lam):
    b, t = q.shape[:2]
    nb = t // Q_BLOCK
    qb = q.reshape(b, nb, Q_BLOCK, N_HEADS, 2, HEAD_DIM).swapaxes(0, 1)
    pb = pos.reshape(nb, Q_BLOCK)
    out = lax.map(lambda a: diff_attend(a[0], k, v, a[1], pos, lam), (qb, pb))
    return out.swapaxes(0, 1).reshape(b, t, N_HEADS, V_DIM)


def pool_mix(u_ext, pos_ext, n_out, w_pool, pool_scale):
    b, L = u_ext.shape[:2]
    uf = u_ext.astype(jnp.float32).reshape(b, L, N_POOL_GROUPS, POOL_GROUP_WIDTH)
    cs = jnp.cumsum(uf, axis=1)
    outs = []
    for g, w in enumerate(POOL_WINDOWS):
        c = cs[:, :, g]
        c_prev = jnp.pad(c, ((0, 0), (w, 0), (0, 0)))[:, :L]
        cnt = jnp.minimum(w, pos_ext + 1).astype(jnp.float32)[None, :, None]
        outs.append((c - c_prev) / cnt - uf[:, :, g])
    pooled = jnp.stack(outs, axis=2)[:, L - n_out:]
    y = jnp.einsum('btgc,gcd->btgd', pooled.astype(u_ext.dtype), w_pool)
    return y.reshape(b, n_out, POOL_WIDTH) * pool_scale


def mixer_output(x, attn, pooled, gate, g_subln, lam_init, w_out, g_post):
    b, t = x.shape[:2]
    a = rms_norm(attn, g_subln) * (1.0 - lam_init)
    m = jnp.concatenate([a.reshape(b, t, ATTN_WIDTH), pooled], axis=-1) @ w_out
    return x + gate * rms_norm(m, g_post)


def ffn_block(x, shift, scale, gate, g_pre, w1, w2, g_post):
    h = rms_norm(x, g_pre) * (1.0 + scale) + shift
    f = jnp.square(jax.nn.relu(h @ w1)) @ w2
    return x + gate * rms_norm(f, g_post)


def setup_inputs(seed: int = 0) -> dict:
    key = jax.random.key(seed)
    ks = jax.random.split(key, 28)
    n_pages = PAST_LEN // PAGE_SIZE
    n_used = DEC_BATCH * n_pages
    n_phys = (n_used * 5) // 4

    def nrm(k, shape, s):
        return jax.random.normal(k, shape, jnp.float32) * s

    def gain(k, shape):
        return 1.0 + 0.05 * jax.random.normal(k, shape, jnp.float32)

    page_table = jax.random.permutation(ks[7], n_phys)[:n_used].reshape(DEC_BATCH, n_pages).astype(jnp.int32)
    return {
        "x_prompt": nrm(ks[0], (BATCH, SEQ, D_MODEL), 1.0),
        "x_sample": nrm(ks[1], (DEC_BATCH, DEC_SEQ, D_MODEL), 1.0),
        "cache_k": nrm(ks[2], (DEPTH, n_phys, PAGE_SIZE, N_HEADS, 2 * HEAD_DIM), 1.0),
        "cache_v": nrm(ks[3], (DEPTH, n_phys, PAGE_SIZE, N_HEADS, V_DIM), 1.0),
        "state_pool": nrm(ks[4], (DEPTH, DEC_BATCH, POOL_HIST, POOL_WIDTH), 1.0),
        "page_table": page_table,
        "c_prompt": nrm(ks[5], (BATCH, D_MODEL), 1.0),
        "c_sample": nrm(ks[6], (DEC_BATCH, D_MODEL), 1.0),
        "w_ada": nrm(ks[8], (DEPTH, D_MODEL, N_MOD * D_MODEL), 0.5 * D_MODEL ** -0.5),
        "b_ada": nrm(ks[9], (DEPTH, N_MOD * D_MODEL), 0.01),
        "g_pre_mix": gain(ks[10], (DEPTH, D_MODEL)),
        "w_in": nrm(ks[11], (DEPTH, D_MODEL, IN_WIDTH), D_MODEL ** -0.5),
        "lambda_q1": nrm(ks[12], (DEPTH, HEAD_DIM), 0.1),
        "lambda_k1": nrm(ks[13], (DEPTH, HEAD_DIM), 0.1),
        "lambda_q2": nrm(ks[14], (DEPTH, HEAD_DIM), 0.1),
        "lambda_k2": nrm(ks[15], (DEPTH, HEAD_DIM), 0.1),
        "g_subln": gain(ks[16], (DEPTH, V_DIM)),
        "w_pool": nrm(ks[17], (DEPTH, N_POOL_GROUPS, POOL_GROUP_WIDTH, POOL_GROUP_WIDTH), POOL_GROUP_WIDTH ** -0.5),
        "pool_scale": gain(ks[18], (DEPTH, POOL_WIDTH)),
        "w_out": nrm(ks[19], (DEPTH, MIX_WIDTH, D_MODEL), MIX_WIDTH ** -0.5),
        "g_post_mix": gain(ks[20], (DEPTH, D_MODEL)),
        "g_pre_ffn": gain(ks[21], (DEPTH, D_MODEL)),
        "w_ff1": nrm(ks[22], (DEPTH, D_MODEL, D_FF), D_MODEL ** -0.5),
        "w_ff2": nrm(ks[23], (DEPTH, D_FF, D_MODEL), D_FF ** -0.5),
        "g_post_ffn": gain(ks[24], (DEPTH, D_MODEL)),
    }


def reference(x_prompt, x_sample, cache_k, cache_v, state_pool, page_table, c_prompt, c_sample,
              w_ada, b_ada, g_pre_mix, w_in, lambda_q1, lambda_k1, lambda_q2, lambda_k2, g_subln,
              w_pool, pool_scale, w_out, g_post_mix, g_pre_ffn, w_ff1, w_ff2, g_post_ffn):
    pos_p = jnp.arange(SEQ, dtype=jnp.int32)
    pos_s = PAST_LEN + jnp.arange(DEC_SEQ, dtype=jnp.int32)
    pos_past = jnp.arange(PAST_LEN, dtype=jnp.int32)
    pos_keys_s = jnp.concatenate([pos_past, pos_s])
    pos_pool_s = PAST_LEN - POOL_HIST + jnp.arange(POOL_HIST + DEC_SEQ, dtype=jnp.int32)

    xp, xs = x_prompt, x_sample
    kp_l, vp_l, sp_l, ks_l, vs_l, ss_l = [], [], [], [], [], []
    for l in range(DEPTH):
        lam_init = 0.8 - 0.6 * math.exp(-0.3 * l)
        lam = (jnp.exp(jnp.sum(lambda_q1[l] * lambda_k1[l]).astype(jnp.float32))
               - jnp.exp(jnp.sum(lambda_q2[l] * lambda_k2[l]).astype(jnp.float32)) + lam_init)
        mp = ada_modulation(c_prompt, w_ada[l], b_ada[l])
        ms = ada_modulation(c_sample, w_ada[l], b_ada[l])

        q, k, v, u = mixer_inputs(xp, mp[0], mp[1], g_pre_mix[l], w_in[l], pos_p)
        attn = prompt_attention(q, k, v, pos_p, lam)
        pooled = pool_mix(u, pos_p, SEQ, w_pool[l], pool_scale[l])
        xp = mixer_output(xp, attn, pooled, mp[2], g_subln[l], lam_init, w_out[l], g_post_mix[l])
        xp = ffn_block(xp, mp[3], mp[4], mp[5], g_pre_ffn[l], w_ff1[l], w_ff2[l], g_post_ffn[l])
        kp_l.append(k.reshape(BATCH, SEQ, N_HEADS, 2 * HEAD_DIM))
        vp_l.append(v)
        sp_l.append(u[:, SEQ - POOL_HIST:])

        q, k, v, u = mixer_inputs(xs, ms[0], ms[1], g_pre_mix[l], w_in[l], pos_s)
        k_past = cache_k[l, page_table].reshape(DEC_BATCH, PAST_LEN, N_HEADS, 2, HEAD_DIM)
        v_past = cache_v[l, page_table].reshape(DEC_BATCH, PAST_LEN, N_HEADS, V_DIM)
        k_all = jnp.concatenate([k_past.astype(k.dtype), k], axis=1)
        v_all = jnp.concatenate([v_past.astype(v.dtype), v], axis=1)
        attn = diff_attend(q, k_all, v_all, pos_s, pos_keys_s, lam)
        u_ext = jnp.concatenate([state_pool[l].astype(u.dtype), u], axis=1)
        pooled = pool_mix(u_ext, pos_pool_s, DEC_SEQ, w_pool[l], pool_scale[l])
        xs = mixer_output(xs, attn, pooled, ms[2], g_subln[l], lam_init, w_out[l], g_post_mix[l])
        xs = ffn_block(xs, ms[3], ms[4], ms[5], g_pre_ffn[l], w_ff1[l], w_ff2[l], g_post_ffn[l])
        ks_l.append(k.reshape(DEC_BATCH, DEC_SEQ, N_HEADS, 2 * HEAD_DIM))
        vs_l.append(v)
        ss_l.append(u_ext[:, DEC_SEQ:])

    k_prompt = jnp.stack(kp_l)
    v_prompt = jnp.stack(vp_l)
    pool_prompt = jnp.stack(sp_l)
    k_sample = jnp.stack(ks_l)
    v_sample = jnp.stack(vs_l)
    pool_sample = jnp.stack(ss_l)
    return (xp, xs, k_prompt, v_prompt, pool_prompt, k_sample, v_sample, pool_sample)
```

```python
import functools
import math

import jax
import jax.numpy as jnp
from jax import lax
from jax.experimental import pallas as pl
from jax.experimental.pallas import tpu as pltpu

F32 = jnp.float32
BF16 = jnp.bfloat16

D_MODEL = 2048
N_HEADS = 8
HEAD_DIM = 64
V_DIM = 2 * HEAD_DIM
ATTN_WIDTH = N_HEADS * V_DIM
POOL_WIDTH = 1024
IN_WIDTH = 3 * ATTN_WIDTH + POOL_WIDTH
ROT_DIM = HEAD_DIM // 4
ROPE_THETA = 500000.0
POOL_WINDOWS = (2, 4, 8, 16)
POOL_GROUP_WIDTH = POOL_WIDTH // len(POOL_WINDOWS)
POOL_HIST = max(POOL_WINDOWS) - 1
HIST_PAD = 16
D_FF = 4 * D_MODEL
N_MOD = 6
PAGE_SIZE = 128
EPS = 1e-6
NEG_BIG = -1e30

LANES = 128
SUBLANES = 8
VMEM_LIMIT = 56 * 1024 * 1024


def _cparams(sem):
    return pltpu.CompilerParams(dimension_semantics=sem, vmem_limit_bytes=VMEM_LIMIT)


def _rms(x, g):
    ms = jnp.mean(x * x, axis=-1, keepdims=True)
    return x * lax.rsqrt(ms + EPS) * g


def _ada_kernel(c_ref, w_ref, b_ref, o_ref):
    c = c_ref[...]
    s = c * (1.0 / (1.0 + jnp.exp(-c)))
    o_ref[...] = jnp.dot(s.astype(BF16), w_ref[...].astype(BF16),
                         preferred_element_type=F32) + b_ref[...]


def _ada(c_all, w_ada, b_ada):
    rows = c_all.shape[0]
    tn = 1024
    return pl.pallas_call(
        _ada_kernel,
        grid=(N_MOD * D_MODEL // tn,),
        in_specs=[
            pl.BlockSpec((rows, D_MODEL), lambda j: (0, 0)),
            pl.BlockSpec((D_MODEL, tn), lambda j: (0, j)),
            pl.BlockSpec((1, tn), lambda j: (0, j)),
        ],
        out_specs=pl.BlockSpec((rows, tn), lambda j: (0, j)),
        out_shape=jax.ShapeDtypeStruct((rows, N_MOD * D_MODEL), F32),
        compiler_params=_cparams(("arbitrary",)),
        name="ada",
    )(c_all, w_ada, b_ada.reshape(1, -1))


def _rope(z, cos_ref, sa_ref, sb_ref):
    return (z * cos_ref[...]
            + pltpu.roll(z, LANES - ROT_DIM // 2, 1) * sa_ref[...]
            + pltpu.roll(z, ROT_DIM // 2, 1) * sb_ref[...])


def _mixer_in_kernel(x_ref, shift_ref, scale_ref, g_ref, w_ref, cos_ref, sa_ref, sb_ref,
                     *refs, tm, decode):
    if decode:
        q4_ref, k4_ref, v4_ref, u_ref, h_scr = refs
    else:
        qb_ref, k4_ref, kb_ref, v4_ref, vb_ref, u_ref, h_scr = refs
    j = pl.program_id(1)

    @pl.when(j == 0)
    def _():
        h = _rms(x_ref[...], g_ref[...]) * (1.0 + scale_ref[...]) + shift_ref[...]
        h_scr[...] = h.astype(BF16)

    z = jnp.dot(h_scr[...], w_ref[...], preferred_element_type=F32)
    heads = [z[:, h * V_DIM:(h + 1) * V_DIM] for h in range(N_HEADS)]

    @pl.when(j == 0)
    def _():
        for h, zh in enumerate(heads):
            q = _rope(zh, cos_ref, sa_ref, sb_ref) * (HEAD_DIM ** -0.5)
            if decode:
                q4_ref[pl.ds(h, tm, stride=N_HEADS), :] = q
            else:
                qb_ref[:, h * V_DIM:(h + 1) * V_DIM] = q.astype(BF16)

    @pl.when(j == 1)
    def _():
        for h, zh in enumerate(heads):
            k = _rope(zh, cos_ref, sa_ref, sb_ref)
            k4_ref[pl.ds(h, tm, stride=N_HEADS), :] = k
            if not decode:
                kb_ref[:, h * V_DIM:(h + 1) * V_DIM] = k.astype(BF16)

    @pl.when(j == 2)
    def _():
        for h, zh in enumerate(heads):
            v4_ref[pl.ds(h, tm, stride=N_HEADS), :] = zh
            if not decode:
                vb_ref[:, h * V_DIM:(h + 1) * V_DIM] = zh.astype(BF16)

    @pl.when(j == 3)
    def _():
        u_ref[...] = z


def _mixer_in(x, mod, g_pre, w_in_bf, tables, *, tm, seq_tiles, decode):
    rows = x.shape[0]
    nt = rows // tm
    if decode:
        mod_spec = lambda c: pl.BlockSpec((tm, D_MODEL), lambda i, j: (i, c))
        tab_spec = pl.BlockSpec((tm, LANES), lambda i, j: (i, 0))
    else:
        mod_spec = lambda c: pl.BlockSpec((None, 1, D_MODEL), lambda i, j: (i // seq_tiles, 0, c))
        tab_spec = pl.BlockSpec((tm, LANES), lambda i, j: (i % seq_tiles, 0))
    row_bf = pl.BlockSpec((tm, ATTN_WIDTH), lambda i, j: (i, 0))
    row4 = pl.BlockSpec((tm * N_HEADS, V_DIM), lambda i, j: (i, 0))
    sd_bf = jax.ShapeDtypeStruct((rows, ATTN_WIDTH), BF16)
    sd4 = jax.ShapeDtypeStruct((rows * N_HEADS, V_DIM), F32)
    sd_u = jax.ShapeDtypeStruct((rows, POOL_WIDTH), F32)
    if decode:
        out_specs = [row4, row4, row4, row_bf]
        out_shape = [sd4, sd4, sd4, sd_u]
    else:
        out_specs = [row_bf, row4, row_bf, row4, row_bf, row_bf]
        out_shape = [sd_bf, sd4, sd_bf, sd4, sd_bf, sd_u]
    return pl.pallas_call(
        functools.partial(_mixer_in_kernel, tm=tm, decode=decode),
        grid=(nt, IN_WIDTH // ATTN_WIDTH),
        in_specs=[
            pl.BlockSpec((tm, D_MODEL), lambda i, j: (i, 0)),
            mod_spec(0), mod_spec(1),
            pl.BlockSpec((1, D_MODEL), lambda i, j: (0, 0)),
            pl.BlockSpec((D_MODEL, ATTN_WIDTH), lambda i, j: (0, j)),
            tab_spec, tab_spec, tab_spec,
        ],
        out_specs=out_specs,
        out_shape=out_shape,
        scratch_shapes=[pltpu.VMEM((tm, D_MODEL), BF16)],
        compiler_params=_cparams(("parallel", "arbitrary")),
        name="mixer_in_decode" if decode else "mixer_in_prompt",
    )(x, mod, mod, g_pre.reshape(1, -1), w_in_bf, *tables)


def _rope_tables(pos):
    half = ROT_DIM // 2
    inv = 1.0 / (ROPE_THETA ** (jnp.arange(0, ROT_DIM, 2, dtype=F32) / ROT_DIM))
    ang = pos.astype(F32)[:, None] * inv[None, :]
    cos, sin = jnp.cos(ang), jnp.sin(ang)
    lane = jnp.arange(LANES) % HEAD_DIM
    first = (lane < half)[None, :]
    second = ((lane >= half) & (lane < ROT_DIM))[None, :]
    cos_l = jnp.take(cos, lane % half, axis=1)
    sin_l = jnp.take(sin, lane % half, axis=1)
    c = jnp.where(first | second, cos_l, 1.0)
    sa = jnp.where(first, -sin_l, 0.0)
    sb = jnp.where(second, sin_l, 0.0)
    return c.astype(F32), sa.astype(F32), sb.astype(F32)


def _lam(lq1, lk1, lq2, lk2, lam_init):
    a = jnp.sum(lq1[...] * lk1[...], axis=-1, keepdims=True)
    b = jnp.sum(lq2[...] * lk2[...], axis=-1, keepdims=True)
    return jnp.exp(a) - jnp.exp(b) + lam_init


def _flash_kernel(q_ref, k_ref, v_ref, lq1, lk1, lq2, lk2, g_ref, o_ref,
                  m_scr, l_scr, acc_scr, *, tq, tk, lam_init):
    i = pl.program_id(2)
    q = q_ref[...].astype(F32)
    lane = lax.broadcasted_iota(jnp.int32, (tq, V_DIM), 1)
    qq = jnp.concatenate([jnp.where(lane < HEAD_DIM, q, 0.0),
                          jnp.where(lane >= HEAD_DIM, q, 0.0)], axis=0).astype(BF16)
    m_scr[...] = jnp.full(m_scr.shape, NEG_BIG, F32)
    l_scr[...] = jnp.zeros(l_scr.shape, F32)
    acc_scr[...] = jnp.zeros(acc_scr.shape, F32)

    def step(j, masked):
        start = pl.multiple_of(j * tk, tk)
        k = k_ref[pl.ds(start, tk), :]
        v = v_ref[pl.ds(start, tk), :]
        s = lax.dot_general(qq, k, (((1,), (1,)), ((), ())), preferred_element_type=F32)
        if masked:
            row = i * tq + (lax.broadcasted_iota(jnp.int32, (2 * tq, tk), 0) & (tq - 1))
            col = j * tk + lax.broadcasted_iota(jnp.int32, (2 * tq, tk), 1)
            s = jnp.where(col <= row, s, NEG_BIG)
        m_prev = m_scr[...]
        m_new = jnp.maximum(m_prev, jnp.max(s, axis=1, keepdims=True))
        alpha = jnp.exp(m_prev - m_new)
        p = jnp.exp(s - m_new)
        l_scr[...] = alpha * l_scr[...] + jnp.sum(p, axis=1, keepdims=True)
        acc_scr[...] = alpha * acc_scr[...] + jnp.dot(p.astype(BF16), v, preferred_element_type=F32)
        m_scr[...] = m_new

    n_full = (i * tq) // tk

    def body(j, carry):
        step(j, False)
        return carry

    lax.fori_loop(0, n_full, body, 0)
    step(n_full, True)

    n = acc_scr[...] / l_scr[...]
    out = n[:tq] - _lam(lq1, lk1, lq2, lk2, lam_init) * n[tq:]
    o_ref[...] = (_rms(out, g_ref[...]) * (1.0 - lam_init)).astype(BF16)


def _flash(q_bf, k_bf, v_bf, lams, g_subln, *, batch, seq, lam_init, tq=256, tk=512):
    nq = seq // tq
    lam_spec = pl.BlockSpec((1, HEAD_DIM), lambda b, h, i: (0, 0))
    return pl.pallas_call(
        functools.partial(_flash_kernel, tq=tq, tk=tk, lam_init=lam_init),
        grid=(batch, N_HEADS, nq),
        in_specs=[
            pl.BlockSpec((tq, V_DIM), lambda b, h, i: (b * nq + i, h)),
            pl.BlockSpec((seq, V_DIM), lambda b, h, i: (b, h)),
            pl.BlockSpec((seq, V_DIM), lambda b, h, i: (b, h)),
            lam_spec, lam_spec, lam_spec, lam_spec,
            pl.BlockSpec((1, V_DIM), lambda b, h, i: (0, 0)),
        ],
        out_specs=pl.BlockSpec((tq, V_DIM), lambda b, h, i: (b * nq + i, h)),
        out_shape=jax.ShapeDtypeStruct((batch * seq, ATTN_WIDTH), BF16),
        scratch_shapes=[pltpu.VMEM((2 * tq, 1), F32), pltpu.VMEM((2 * tq, 1), F32),
                        pltpu.VMEM((2 * tq, V_DIM), F32)],
        compiler_params=_cparams(("parallel", "parallel", "arbitrary")),
        name="flash_prompt",
    )(q_bf, k_bf, v_bf, *lams, g_subln.reshape(1, -1))


def _decode_kernel(pt_ref, q_ref, kn_ref, vn_ref, *refs, n_pages, n_groups, dec_seq, lam_init):
    k_refs = refs[:n_pages]
    v_refs = refs[n_pages:2 * n_pages]
    lq1, lk1, lq2, lk2, g_ref, o_ref, m_scr, l_scr, acc_scr = refs[2 * n_pages:]
    del pt_ref
    g = pl.program_id(1)
    qh = dec_seq * N_HEADS
    sub = lax.broadcasted_iota(jnp.int32, (SUBLANES, LANES), 0)
    lane = lax.broadcasted_iota(jnp.int32, (SUBLANES, LANES), 1)
    diag = (lane & (N_HEADS - 1)) == sub

    r = q_ref[...]
    lane_q = lax.broadcasted_iota(jnp.int32, (qh, V_DIM), 1)
    qt = jnp.concatenate([jnp.where(lane_q < HEAD_DIM, r, 0.0),
                          jnp.where(lane_q >= HEAD_DIM, r, 0.0)], axis=0).astype(BF16)

    @pl.when(g == 0)
    def _():
        m_scr[...] = jnp.full(m_scr.shape, NEG_BIG, F32)
        l_scr[...] = jnp.zeros(l_scr.shape, F32)
        acc_scr[...] = jnp.zeros(acc_scr.shape, F32)

    def scores(k_rows):
        s = lax.dot_general(k_rows.astype(BF16), qt, (((1,), (1,)), ((), ())),
                            preferred_element_type=F32)
        return s.reshape(k_rows.shape[0] // N_HEADS, N_HEADS, LANES)

    def update(s_list, v_list):
        m_prev = m_scr[...]
        m_new = m_prev
        for s in s_list:
            m_new = jnp.maximum(m_new, jnp.max(s, axis=0))
        alpha = jnp.exp(m_prev - m_new)
        l_new = alpha * l_scr[...]
        pv = jnp.zeros((V_DIM, LANES), F32)
        for s, v in zip(s_list, v_list):
            p = jnp.exp(s - m_new[None])
            l_new = l_new + jnp.sum(p, axis=0)
            pm = jnp.where(diag[None], p, 0.0).reshape(v.shape[0], LANES).astype(BF16)
            pv = pv + lax.dot_general(v.astype(BF16), pm, (((0,), (0,)), ((), ())),
                                      preferred_element_type=F32)
        alpha_row = jnp.sum(jnp.where(diag, alpha, 0.0), axis=0, keepdims=True)
        acc_scr[...] = acc_scr[...] * alpha_row + pv
        l_scr[...] = l_new
        m_scr[...] = m_new

    rows = PAGE_SIZE * N_HEADS
    update([scores(kr[...].reshape(rows, V_DIM)) for kr in k_refs],
           [vr[...].reshape(rows, V_DIM) for vr in v_refs])

    @pl.when(g == n_groups - 1)
    def _():
        s_new = scores(kn_ref[...])
        pos = lax.broadcasted_iota(jnp.int32, s_new.shape, 0)
        qidx = (lax.broadcasted_iota(jnp.int32, s_new.shape, 2) & (qh - 1)) >> 3
        update([jnp.where(pos <= qidx, s_new, NEG_BIG)], [vn_ref[...]])
        l_row = jnp.sum(jnp.where(diag, l_scr[...], 0.0), axis=0, keepdims=True)
        nt = (acc_scr[...] / l_row).T
        out = nt[:qh] - _lam(lq1, lk1, lq2, lk2, lam_init) * nt[qh:]
        o_ref[...] = _rms(out, g_ref[...]) * (1.0 - lam_init)


def _decode(q4, k4, v4, cache_k, cache_v, layer, page_table, lams, g_subln, *,
            dec_batch, dec_seq, lam_init, n_pages=8):
    pages_per_seq = page_table.shape[1]
    n_groups = pages_per_seq // n_pages
    qh = dec_seq * N_HEADS
    tok_spec = pl.BlockSpec((qh, V_DIM), lambda b, g, pt: (b, 0))
    lam_spec = pl.BlockSpec((1, HEAD_DIM), lambda b, g, pt: (0, 0))

    def page_spec(p):
        return pl.BlockSpec(
            (None, None, PAGE_SIZE, N_HEADS, V_DIM),
            lambda b, g, pt: (layer, pt[b * pages_per_seq + g * n_pages + p], 0, 0, 0))

    cache_specs = [page_spec(p) for p in range(n_pages)]
    grid_spec = pltpu.PrefetchScalarGridSpec(
        num_scalar_prefetch=1,
        grid=(dec_batch, n_groups),
        in_specs=[tok_spec, tok_spec, tok_spec] + cache_specs + cache_specs
        + [lam_spec] * 4 + [pl.BlockSpec((1, V_DIM), lambda b, g, pt: (0, 0))],
        out_specs=tok_spec,
        scratch_shapes=[pltpu.VMEM((SUBLANES, LANES), F32), pltpu.VMEM((SUBLANES, LANES), F32),
                        pltpu.VMEM((V_DIM, LANES), F32)],
    )
    return pl.pallas_call(
        functools.partial(_decode_kernel, n_pages=n_pages, n_groups=n_groups,
                          dec_seq=dec_seq, lam_init=lam_init),
        grid_spec=grid_spec,
        out_shape=jax.ShapeDtypeStruct((dec_batch * qh, V_DIM), F32),
        compiler_params=_cparams(("parallel", "arbitrary")),
        name="decode_attn",
    )(page_table.reshape(-1), q4, k4, v4, *([cache_k] * n_pages), *([cache_v] * n_pages),
      *lams, g_subln.reshape(1, -1))


def _mixer_out_kernel(x_ref, a_ref, u_ref, hist_ref, gate_ref, gpost_ref, wout_ref, wpool_ref,
                      ps_ref, o_ref, ext_scr, *, nb, t, seq_tiles, decode, pos_base):
    rows = nb * t
    cur = u_ref[...]
    ext_scr[:, HIST_PAD:HIST_PAD + t, :] = cur
    if decode:
        ext_scr[:, HIST_PAD - POOL_HIST:HIST_PAD, :] = hist_ref[...]
        base = pos_base
    else:
        first = (pl.program_id(0) % seq_tiles) == 0
        ext_scr[:, 0:HIST_PAD, :] = jnp.where(first, 0.0, hist_ref[...])
        base = (pl.program_id(0) % seq_tiles) * t
    pos = base + lax.broadcasted_iota(jnp.int32, (1, t, POOL_GROUP_WIDTH), 1)

    ys = []
    for gi, w in enumerate(POOL_WINDOWS):
        cs = slice(gi * POOL_GROUP_WIDTH, (gi + 1) * POOL_GROUP_WIDTH)
        cur_g = cur[:, :, cs]
        win = cur_g
        for k in range(1, w):
            win = win + ext_scr[:, HIST_PAD - k:HIST_PAD - k + t, cs]
        cnt = jnp.minimum(w, pos + 1).astype(F32)
        pooled = (win / cnt - cur_g).reshape(rows, POOL_GROUP_WIDTH)
        ys.append(jnp.dot(pooled.astype(BF16), wpool_ref[gi], preferred_element_type=F32))
    y = jnp.concatenate(ys, axis=1) * ps_ref[...]

    if decode:
        a = jnp.concatenate([a_ref[pl.ds(h, rows, stride=N_HEADS), :] for h in range(N_HEADS)],
                            axis=1).astype(BF16)
    else:
        a = a_ref[...]
    m = (jnp.dot(a, wout_ref[0:ATTN_WIDTH, :], preferred_element_type=F32)
         + jnp.dot(y.astype(BF16), wout_ref[ATTN_WIDTH:, :], preferred_element_type=F32))
    o_ref[...] = x_ref[...] + gate_ref[...] * _rms(m, gpost_ref[...])


def _mixer_out(x, a, u3, hist, mod, g_post, w_out_bf, w_pool_bf, pool_scale, *,
               nb, t, seq_tiles, decode, pos_base=0):
    rows_total = x.shape[0]
    tm = nb * t
    nt = rows_total // tm
    if decode:
        a_spec = pl.BlockSpec((tm * N_HEADS, V_DIM), lambda i: (i, 0))
        u_spec = pl.BlockSpec((nb, t, POOL_WIDTH), lambda i: (i, 0, 0))
        hist_spec = pl.BlockSpec((nb, POOL_HIST, POOL_WIDTH), lambda i: (i, 0, 0))
        gate_spec = pl.BlockSpec((tm, D_MODEL), lambda i: (i, 2))
    else:
        a_spec = pl.BlockSpec((tm, ATTN_WIDTH), lambda i: (i, 0))
        u_spec = pl.BlockSpec((1, t, POOL_WIDTH), lambda i: (i // seq_tiles, i % seq_tiles, 0))
        hpt = t // HIST_PAD
        hist_spec = pl.BlockSpec(
            (1, HIST_PAD, POOL_WIDTH),
            lambda i: (i // seq_tiles, jnp.maximum((i % seq_tiles) * hpt - 1, 0), 0))
        gate_spec = pl.BlockSpec((None, 1, D_MODEL), lambda i: (i // seq_tiles, 0, 2))
    return pl.pallas_call(
        functools.partial(_mixer_out_kernel, nb=nb, t=t, seq_tiles=seq_tiles, decode=decode,
                          pos_base=pos_base),
        grid=(nt,),
        in_specs=[
            pl.BlockSpec((tm, D_MODEL), lambda i: (i, 0)),
            a_spec, u_spec, hist_spec, gate_spec,
            pl.BlockSpec((1, D_MODEL), lambda i: (0, 0)),
            pl.BlockSpec((D_MODEL, D_MODEL), lambda i: (0, 0)),
            pl.BlockSpec((len(POOL_WINDOWS), POOL_GROUP_WIDTH, POOL_GROUP_WIDTH), lambda i: (0, 0, 0)),
            pl.BlockSpec((1, POOL_WIDTH), lambda i: (0, 0)),
        ],
        out_specs=pl.BlockSpec((tm, D_MODEL), lambda i: (i, 0)),
        out_shape=jax.ShapeDtypeStruct((rows_total, D_MODEL), F32),
        scratch_shapes=[pltpu.VMEM((nb, HIST_PAD + t, POOL_WIDTH), F32)],
        compiler_params=_cparams(("parallel",)),
        name="mixer_out_decode" if decode else "mixer_out_prompt",
    )(x, a, u3, hist, mod, g_post.reshape(1, -1), w_out_bf, w_pool_bf, pool_scale.reshape(1, -1))


def _ffn_kernel(x_ref, shift_ref, scale_ref, gate_ref, gpre_ref, gpost_ref, w1_ref, w2_ref,
                o_ref, h_scr, acc_scr, *, n_f):
    j = pl.program_id(1)

    @pl.when(j == 0)
    def _():
        h = _rms(x_ref[...], gpre_ref[...]) * (1.0 + scale_ref[...]) + shift_ref[...]
        h_scr[...] = h.astype(BF16)
        acc_scr[...] = jnp.zeros(acc_scr.shape, F32)

    t = jnp.dot(h_scr[...], w1_ref[...], preferred_element_type=F32)
    t = jnp.square(jnp.maximum(t, 0.0))
    acc_scr[...] += jnp.dot(t.astype(BF16), w2_ref[...], preferred_element_type=F32)

    @pl.when(j == n_f - 1)
    def _():
        o_ref[...] = x_ref[...] + gate_ref[...] * _rms(acc_scr[...], gpost_ref[...])


def _ffn(x, mod, g_pre, g_post, w1_bf, w2_bf, *, tm, tf, seq_tiles, decode):
    rows = x.shape[0]
    nt = rows // tm
    n_f = D_FF // tf
    if decode:
        mod_spec = lambda c: pl.BlockSpec((tm, D_MODEL), lambda i, j: (i, c))
    else:
        mod_spec = lambda c: pl.BlockSpec((None, 1, D_MODEL), lambda i, j: (i // seq_tiles, 0, c))
    vec_spec = pl.BlockSpec((1, D_MODEL), lambda i, j: (0, 0))
    return pl.pallas_call(
        functools.partial(_ffn_kernel, n_f=n_f),
        grid=(nt, n_f),
        in_specs=[
            pl.BlockSpec((tm, D_MODEL), lambda i, j: (i, 0)),
            mod_spec(3), mod_spec(4), mod_spec(5),
            vec_spec, vec_spec,
            pl.BlockSpec((D_MODEL, tf), lambda i, j: (0, j)),
            pl.BlockSpec((tf, D_MODEL), lambda i, j: (j, 0)),
        ],
        out_specs=pl.BlockSpec((tm, D_MODEL), lambda i, j: (i, 0)),
        out_shape=jax.ShapeDtypeStruct((rows, D_MODEL), F32),
        scratch_shapes=[pltpu.VMEM((tm, D_MODEL), BF16), pltpu.VMEM((tm, D_MODEL), F32)],
        compiler_params=_cparams(("parallel", "arbitrary")),
        name="ffn_decode" if decode else "ffn_prompt",
    )(x, mod, mod, mod, g_pre.reshape(1, -1), g_post.reshape(1, -1), w1_bf, w2_bf)


def kernel(x_prompt, x_sample, cache_k, cache_v, state_pool, page_table, c_prompt, c_sample,
           w_ada, b_ada, g_pre_mix, w_in, lambda_q1, lambda_k1, lambda_q2, lambda_k2, g_subln,
           w_pool, pool_scale, w_out, g_post_mix, g_pre_ffn, w_ff1, w_ff2, g_post_ffn):
    batch, seq, _ = x_prompt.shape
    dec_batch, dec_seq, _ = x_sample.shape
    depth = w_in.shape[0]
    past_len = page_table.shape[1] * PAGE_SIZE
    assert dec_seq == SUBLANES and state_pool.shape[2] == POOL_HIST

    tm_p = 512
    seq_tiles = seq // tm_p
    rows_s = dec_batch * dec_seq
    tables_p = _rope_tables(jnp.arange(seq, dtype=jnp.int32))
    tables_s = _rope_tables(past_len + (jnp.arange(rows_s, dtype=jnp.int32) % dec_seq))

    n_c = batch + dec_batch
    c_all = jnp.concatenate(
        [c_prompt, c_sample, jnp.zeros((-n_c % SUBLANES, D_MODEL), F32)], axis=0)

    xp = x_prompt.reshape(batch * seq, D_MODEL)
    xs = x_sample.reshape(rows_s, D_MODEL)
    outs = [[] for _ in range(6)]
    for l in range(depth):
        lam_init = 0.8 - 0.6 * math.exp(-0.3 * l)
        lams = [v[l].reshape(1, HEAD_DIM) for v in (lambda_q1, lambda_k1, lambda_q2, lambda_k2)]
        w_in_bf = w_in[l].astype(BF16)
        w_out_bf = w_out[l].astype(BF16)
        w_pool_bf = w_pool[l].astype(BF16)
        w1_bf = w_ff1[l].astype(BF16)
        w2_bf = w_ff2[l].astype(BF16)

        m_all = _ada(c_all, w_ada[l], b_ada[l])
        mod_p = m_all[:batch].reshape(batch, 1, N_MOD * D_MODEL)
        mod_s = jnp.repeat(m_all[batch:n_c], dec_seq, axis=0)

        q_bf, k4, k_bf, v4, v_bf, u = _mixer_in(
            xp, mod_p, g_pre_mix[l], w_in_bf, tables_p, tm=tm_p, seq_tiles=seq_tiles, decode=False)
        a_bf = _flash(q_bf, k_bf, v_bf, lams, g_subln[l], batch=batch, seq=seq, lam_init=lam_init)
        u3 = u.reshape(batch, seq, POOL_WIDTH)
        xp = _mixer_out(xp, a_bf, u3, u3, mod_p, g_post_mix[l], w_out_bf, w_pool_bf, pool_scale[l],
                        nb=1, t=tm_p, seq_tiles=seq_tiles, decode=False)
        xp = _ffn(xp, mod_p, g_pre_ffn[l], g_post_ffn[l], w1_bf, w2_bf,
                  tm=tm_p, tf=1024, seq_tiles=seq_tiles, decode=False)
        outs[0].append(k4.reshape(batch, seq, N_HEADS, V_DIM))
        outs[1].append(v4.reshape(batch, seq, N_HEADS, V_DIM))
        outs[2].append(u3[:, seq - POOL_HIST:])

        q4s, k4s, v4s, us = _mixer_in(
            xs, mod_s, g_pre_mix[l], w_in_bf, tables_s, tm=256, seq_tiles=1, decode=True)
        a4s = _decode(q4s, k4s, v4s, cache_k, cache_v, l, page_table, lams, g_subln[l],
                      dec_batch=dec_batch, dec_seq=dec_seq, lam_init=lam_init)
        us3 = us.reshape(dec_batch, dec_seq, POOL_WIDTH)
        xs = _mixer_out(xs, a4s, us3, state_pool[l], mod_s, g_post_mix[l], w_out_bf, w_pool_bf,
                        pool_scale[l], nb=16, t=dec_seq, seq_tiles=1, decode=True, pos_base=past_len)
        xs = _ffn(xs, mod_s, g_pre_ffn[l], g_post_ffn[l], w1_bf, w2_bf,
                  tm=256, tf=1024, seq_tiles=1, decode=True)
        outs[3].append(k4s.reshape(dec_batch, dec_seq, N_HEADS, V_DIM))
        outs[4].append(v4s.reshape(dec_batch, dec_seq, N_HEADS, V_DIM))
        outs[5].append(jnp.concatenate([state_pool[l][:, dec_seq:], us3], axis=1))

    kp, vp, sp, ks, vs, ss = (jnp.stack(o) for o in outs)
    return (xp.reshape(batch, seq, D_MODEL), xs.reshape(dec_batch, dec_seq, D_MODEL),
            kp, vp, sp, ks, vs, ss)
```

```python
import functools
import math

import jax
import jax.numpy as jnp
from jax import lax
from jax.experimental import pallas as pl
from jax.experimental.pallas import tpu as pltpu

F32 = jnp.float32
BF16 = jnp.bfloat16

D_MODEL = 2048
N_HEADS = 8
HEAD_DIM = 64
V_DIM = 2 * HEAD_DIM
ATTN_WIDTH = N_HEADS * V_DIM
POOL_WIDTH = 1024
IN_WIDTH = 3 * ATTN_WIDTH + POOL_WIDTH
ROT_DIM = HEAD_DIM // 4
ROPE_THETA = 500000.0
POOL_WINDOWS = (2, 4, 8, 16)
POOL_GROUP_WIDTH = POOL_WIDTH // len(POOL_WINDOWS)
POOL_HIST = max(POOL_WINDOWS) - 1
HIST_PAD = 16
D_FF = 4 * D_MODEL
N_MOD = 6
PAGE_SIZE = 128
EPS = 1e-6
NEG_BIG = -1e30

LANES = 128
SUBLANES = 8
VMEM_LIMIT = 56 * 1024 * 1024


def _cparams(sem):
    return pltpu.CompilerParams(dimension_semantics=sem, vmem_limit_bytes=VMEM_LIMIT)


def _rms(x, g):
    ms = jnp.mean(x * x, axis=-1, keepdims=True)
    return x * lax.rsqrt(ms + EPS) * g


def _ada_kernel(c_ref, w_ref, b_ref, o_ref):
    c = c_ref[...]
    s = c * (1.0 / (1.0 + jnp.exp(-c)))
    o_ref[...] = jnp.dot(s.astype(BF16), w_ref[...].astype(BF16),
                         preferred_element_type=F32) + b_ref[...]


def _ada(c_all, w_ada, b_ada):
    rows = c_all.shape[0]
    tn = 1024
    return pl.pallas_call(
        _ada_kernel,
        grid=(N_MOD * D_MODEL // tn,),
        in_specs=[
            pl.BlockSpec((rows, D_MODEL), lambda j: (0, 0)),
            pl.BlockSpec((D_MODEL, tn), lambda j: (0, j)),
            pl.BlockSpec((1, tn), lambda j: (0, j)),
        ],
        out_specs=pl.BlockSpec((rows, tn), lambda j: (0, j)),
        out_shape=jax.ShapeDtypeStruct((rows, N_MOD * D_MODEL), F32),
        compiler_params=_cparams(("arbitrary",)),
        name="ada",
    )(c_all, w_ada, b_ada.reshape(1, -1))


def _rope(z, cos_ref, sa_ref, sb_ref):
    return (z * cos_ref[...]
            + pltpu.roll(z, LANES - ROT_DIM // 2, 1) * sa_ref[...]
            + pltpu.roll(z, ROT_DIM // 2, 1) * sb_ref[...])


def _mixer_in_kernel(x_ref, shift_ref, scale_ref, g_ref, w_ref, cos_ref, sa_ref, sb_ref,
                     *refs, tm, decode):
    if decode:
        q4_ref, k4_ref, v4_ref, u_ref, h_scr = refs
    else:
        qb_ref, k4_ref, kb_ref, v4_ref, vb_ref, u_ref, h_scr = refs
    j = pl.program_id(1)

    @pl.when(j == 0)
    def _():
        h = _rms(x_ref[...], g_ref[...]) * (1.0 + scale_ref[...]) + shift_ref[...]
        h_scr[...] = h.astype(BF16)

    def heads():
        for hp in range(N_HEADS // 2):
            z2 = jnp.dot(h_scr[...], w_ref[:, 2 * hp * V_DIM:2 * (hp + 1) * V_DIM],
                         preferred_element_type=F32)
            for h in (2 * hp, 2 * hp + 1):
                yield h, slice(h * V_DIM, (h + 1) * V_DIM), z2[:, (h % 2) * V_DIM:(h % 2 + 1) * V_DIM]

    @pl.when(j == 0)
    def _():
        for h, cols, zh in heads():
            q = _rope(zh, cos_ref, sa_ref, sb_ref) * (HEAD_DIM ** -0.5)
            if decode:
                q4_ref[pl.ds(h, tm, stride=N_HEADS), :] = q
            else:
                qb_ref[:, cols] = q.astype(BF16)

    @pl.when(j == 1)
    def _():
        for h, cols, zh in heads():
            k = _rope(zh, cos_ref, sa_ref, sb_ref)
            k4_ref[pl.ds(h, tm, stride=N_HEADS), :] = k
            if not decode:
                kb_ref[:, cols] = k.astype(BF16)

    @pl.when(j == 2)
    def _():
        for h, cols, zh in heads():
            v4_ref[pl.ds(h, tm, stride=N_HEADS), :] = zh
            if not decode:
                vb_ref[:, cols] = zh.astype(BF16)

    @pl.when(j == 3)
    def _():
        u_ref[...] = jnp.dot(h_scr[...], w_ref[...], preferred_element_type=F32)


def _mixer_in(x, mod, g_pre, w_in_bf, tables, *, tm, seq_tiles, decode):
    rows = x.shape[0]
    nt = rows // tm
    if decode:
        mod_spec = lambda c: pl.BlockSpec((tm, D_MODEL), lambda i, j: (i, c))
        tab_spec = pl.BlockSpec((tm, LANES), lambda i, j: (i, 0))
    else:
        mod_spec = lambda c: pl.BlockSpec((None, 1, D_MODEL), lambda i, j: (i // seq_tiles, 0, c))
        tab_spec = pl.BlockSpec((tm, LANES), lambda i, j: (i % seq_tiles, 0))
    row_bf = pl.BlockSpec((tm, ATTN_WIDTH), lambda i, j: (i, 0))
    row4 = pl.BlockSpec((tm * N_HEADS, V_DIM), lambda i, j: (i, 0))
    sd_bf = jax.ShapeDtypeStruct((rows, ATTN_WIDTH), BF16)
    sd4 = jax.ShapeDtypeStruct((rows * N_HEADS, V_DIM), F32)
    sd_u = jax.ShapeDtypeStruct((rows, POOL_WIDTH), F32)
    if decode:
        out_specs = [row4, row4, row4, row_bf]
        out_shape = [sd4, sd4, sd4, sd_u]
    else:
        out_specs = [row_bf, row4, row_bf, row4, row_bf, row_bf]
        out_shape = [sd_bf, sd4, sd_bf, sd4, sd_bf, sd_u]
    return pl.pallas_call(
        functools.partial(_mixer_in_kernel, tm=tm, decode=decode),
        grid=(nt, IN_WIDTH // ATTN_WIDTH),
        in_specs=[
            pl.BlockSpec((tm, D_MODEL), lambda i, j: (i, 0)),
            mod_spec(0), mod_spec(1),
            pl.BlockSpec((1, D_MODEL), lambda i, j: (0, 0)),
            pl.BlockSpec((D_MODEL, ATTN_WIDTH), lambda i, j: (0, j)),
            tab_spec, tab_spec, tab_spec,
        ],
        out_specs=out_specs,
        out_shape=out_shape,
        scratch_shapes=[pltpu.VMEM((tm, D_MODEL), BF16)],
        compiler_params=_cparams(("parallel", "arbitrary")),
        name="mixer_in_decode" if decode else "mixer_in_prompt",
    )(x, mod, mod, g_pre.reshape(1, -1), w_in_bf, *tables)


def _rope_tables(pos):
    half = ROT_DIM // 2
    inv = 1.0 / (ROPE_THETA ** (jnp.arange(0, ROT_DIM, 2, dtype=F32) / ROT_DIM))
    ang = pos.astype(F32)[:, None] * inv[None, :]
    cos, sin = jnp.cos(ang), jnp.sin(ang)
    lane = jnp.arange(LANES) % HEAD_DIM
    first = (lane < half)[None, :]
    second = ((lane >= half) & (lane < ROT_DIM))[None, :]
    cos_l = jnp.take(cos, lane % half, axis=1)
    sin_l = jnp.take(sin, lane % half, axis=1)
    c = jnp.where(first | second, cos_l, 1.0)
    sa = jnp.where(first, -sin_l, 0.0)
    sb = jnp.where(second, sin_l, 0.0)
    return c.astype(F32), sa.astype(F32), sb.astype(F32)


def _lam(lq1, lk1, lq2, lk2, lam_init):
    a = jnp.sum(lq1[...] * lk1[...], axis=-1, keepdims=True)
    b = jnp.sum(lq2[...] * lk2[...], axis=-1, keepdims=True)
    return jnp.exp(a) - jnp.exp(b) + lam_init


def _flash_kernel(q_ref, k_ref, v_ref, lq1, lk1, lq2, lk2, g_ref, o_ref,
                  vt_scr, s_scr, *, tq, nq, lam_init):
    i = pl.program_id(2)
    vq = 2 * tq
    grp = tq // SUBLANES

    @pl.when(i == 0)
    def _():
        vt_scr[...] = v_ref[...].astype(F32).T.astype(BF16)

    q = q_ref[...].astype(F32)
    lane = lax.broadcasted_iota(jnp.int32, (tq, V_DIM), 1)
    qq = jnp.concatenate([jnp.where(lane < HEAD_DIM, q, 0.0),
                          jnp.where(lane >= HEAD_DIM, q, 0.0)], axis=0).astype(BF16)
    lam = _lam(lq1, lk1, lq2, lk2, lam_init)

    def block(ii):
        mx = jnp.full((SUBLANES, vq), NEG_BIG, F32)
        for c in range(ii + 1):
            rows = slice(c * tq, (c + 1) * tq)
            s = lax.dot_general(k_ref[rows, :], qq, (((1,), (1,)), ((), ())),
                                preferred_element_type=F32)
            if c == ii:
                kv = lax.broadcasted_iota(jnp.int32, (tq, vq), 0)
                qp = lax.broadcasted_iota(jnp.int32, (tq, vq), 1) & (tq - 1)
                s = jnp.where(kv <= qp, s, NEG_BIG)
            s_scr[rows, :] = s
            mx = jnp.maximum(mx, jnp.max(s.reshape(grp, SUBLANES, vq), axis=0))
        m = jnp.max(mx, axis=0, keepdims=True)
        ls = jnp.zeros((SUBLANES, vq), F32)
        o_t = jnp.zeros((V_DIM, vq), F32)
        for c in range(ii + 1):
            rows = slice(c * tq, (c + 1) * tq)
            p = jnp.exp(s_scr[rows, :] - m)
            ls = ls + jnp.sum(p.reshape(grp, SUBLANES, vq), axis=0)
            o_t = o_t + jnp.dot(vt_scr[:, rows], p.astype(BF16), preferred_element_type=F32)
        n_t = o_t / jnp.sum(ls, axis=0, keepdims=True)
        out = (n_t[:, :tq] - lam * n_t[:, tq:]).T
        o_ref[...] = (_rms(out, g_ref[...]) * (1.0 - lam_init)).astype(BF16)

    for ii in range(nq):
        pl.when(i == ii)(functools.partial(block, ii))


def _flash(q_bf, k_bf, v_bf, lams, g_subln, *, batch, seq, lam_init, tq=256):
    nq = seq // tq
    lam_spec = pl.BlockSpec((1, HEAD_DIM), lambda b, h, i: (0, 0))
    return pl.pallas_call(
        functools.partial(_flash_kernel, tq=tq, nq=nq, lam_init=lam_init),
        grid=(batch, N_HEADS, nq),
        in_specs=[
            pl.BlockSpec((tq, V_DIM), lambda b, h, i: (b * nq + i, h)),
            pl.BlockSpec((seq, V_DIM), lambda b, h, i: (b, h)),
            pl.BlockSpec((seq, V_DIM), lambda b, h, i: (b, h)),
            lam_spec, lam_spec, lam_spec, lam_spec,
            pl.BlockSpec((1, V_DIM), lambda b, h, i: (0, 0)),
        ],
        out_specs=pl.BlockSpec((tq, V_DIM), lambda b, h, i: (b * nq + i, h)),
        out_shape=jax.ShapeDtypeStruct((batch * seq, ATTN_WIDTH), BF16),
        scratch_shapes=[pltpu.VMEM((V_DIM, seq), BF16), pltpu.VMEM((seq, 2 * tq), F32)],
        compiler_params=_cparams(("parallel", "parallel", "arbitrary")),
        name="flash_prompt",
    )(q_bf, k_bf, v_bf, *lams, g_subln.reshape(1, -1))


def _decode_kernel(pt_ref, q_ref, kn_ref, vn_ref, *refs, n_seqs, n_pages, n_groups, dec_seq,
                   lam_init):
    n_blk = n_seqs * n_pages
    k_refs = refs[:n_blk]
    v_refs = refs[n_blk:2 * n_blk]
    lq1, lk1, lq2, lk2, g_ref, o_ref, m_scr, l_scr, acc_scr = refs[2 * n_blk:]
    del pt_ref
    g = pl.program_id(1)
    qh = dec_seq * N_HEADS
    page_rows = PAGE_SIZE * N_HEADS

    @pl.when(g == 0)
    def _():
        m_scr[...] = jnp.full(m_scr.shape, NEG_BIG, F32)
        l_scr[...] = jnp.zeros(l_scr.shape, F32)
        acc_scr[...] = jnp.zeros(acc_scr.shape, F32)

    sub = lax.broadcasted_iota(jnp.int32, (SUBLANES, LANES), 0)
    lane = lax.broadcasted_iota(jnp.int32, (SUBLANES, LANES), 1)
    diag = (lane & (N_HEADS - 1)) == sub
    lane_q = lax.broadcasted_iota(jnp.int32, (qh, V_DIM), 1)
    lam = _lam(lq1, lk1, lq2, lk2, lam_init)

    def query_cols(b):
        r = q_ref[b * qh:(b + 1) * qh, :]
        return jnp.concatenate([jnp.where(lane_q < HEAD_DIM, r, 0.0),
                                jnp.where(lane_q >= HEAD_DIM, r, 0.0)], axis=0).astype(BF16)

    def update(b, s_list, v_list):
        m_prev = m_scr[b]
        m_new = m_prev
        for s in s_list:
            m_new = jnp.maximum(m_new, jnp.max(s, axis=0))
        alpha = jnp.exp(m_prev - m_new)
        m_eff = jnp.where(diag, m_new, -NEG_BIG)
        l_new = alpha * l_scr[b]
        pv = jnp.zeros((V_DIM, LANES), F32)
        for s, v in zip(s_list, v_list):
            p = jnp.exp(s - m_eff[None])
            l_new = l_new + jnp.sum(p, axis=0)
            pm = p.reshape(v.shape[0], LANES).astype(BF16)
            pv = pv + lax.dot_general(v.astype(BF16), pm, (((0,), (0,)), ((), ())),
                                      preferred_element_type=F32)
        alpha_row = jnp.sum(jnp.where(diag, alpha, 0.0), axis=0, keepdims=True)
        acc_scr[b] = acc_scr[b] * alpha_row + pv
        l_scr[b] = l_new
        m_scr[b] = m_new

    for b in range(n_seqs):
        qt = query_cols(b)
        pages = range(b * n_pages, (b + 1) * n_pages)
        s_list = [
            lax.dot_general(k_refs[p][...].reshape(page_rows, V_DIM).astype(BF16), qt,
                            (((1,), (1,)), ((), ())), preferred_element_type=F32
                            ).reshape(PAGE_SIZE, N_HEADS, LANES) for p in pages]
        update(b, s_list, [v_refs[p][...].reshape(page_rows, V_DIM) for p in pages])

    @pl.when(g == n_groups - 1)
    def _():
        for b in range(n_seqs):
            tok = slice(b * qh, (b + 1) * qh)
            s_new = lax.dot_general(kn_ref[tok, :].astype(BF16), query_cols(b),
                                    (((1,), (1,)), ((), ())), preferred_element_type=F32)
            s_new = s_new.reshape(dec_seq, N_HEADS, LANES)
            pos = lax.broadcasted_iota(jnp.int32, s_new.shape, 0)
            qidx = (lax.broadcasted_iota(jnp.int32, s_new.shape, 2) & (qh - 1)) >> 3
            update(b, [jnp.where(pos <= qidx, s_new, NEG_BIG)], [vn_ref[tok, :]])
            l_row = jnp.sum(jnp.where(diag, l_scr[b], 0.0), axis=0, keepdims=True)
            nt = (acc_scr[b] / l_row).T
            out = nt[:qh] - lam * nt[qh:]
            o_ref[tok, :] = _rms(out, g_ref[...]) * (1.0 - lam_init)


def _decode(q4, k4, v4, cache_k, cache_v, layer, page_table, lams, g_subln, *,
            dec_batch, dec_seq, lam_init, n_seqs=1, n_pages=8):
    pages_per_seq = page_table.shape[1]
    n_groups = pages_per_seq // n_pages
    qh = dec_seq * N_HEADS
    tok_spec = pl.BlockSpec((n_seqs * qh, V_DIM), lambda b, g, pt: (b, 0))
    lam_spec = pl.BlockSpec((1, HEAD_DIM), lambda b, g, pt: (0, 0))

    def page_spec(s, p):
        return pl.BlockSpec(
            (None, None, PAGE_SIZE, N_HEADS, V_DIM),
            lambda b, g, pt: (layer, pt[(b * n_seqs + s) * pages_per_seq + g * n_pages + p], 0, 0, 0))

    cache_specs = [page_spec(s, p) for s in range(n_seqs) for p in range(n_pages)]
    n_blk = len(cache_specs)
    grid_spec = pltpu.PrefetchScalarGridSpec(
        num_scalar_prefetch=1,
        grid=(dec_batch // n_seqs, n_groups),
        in_specs=[tok_spec, tok_spec, tok_spec] + cache_specs + cache_specs
        + [lam_spec] * 4 + [pl.BlockSpec((1, V_DIM), lambda b, g, pt: (0, 0))],
        out_specs=tok_spec,
        scratch_shapes=[pltpu.VMEM((n_seqs, SUBLANES, LANES), F32),
                        pltpu.VMEM((n_seqs, SUBLANES, LANES), F32),
                        pltpu.VMEM((n_seqs, V_DIM, LANES), F32)],
    )
    return pl.pallas_call(
        functools.partial(_decode_kernel, n_seqs=n_seqs, n_pages=n_pages, n_groups=n_groups,
                          dec_seq=dec_seq, lam_init=lam_init),
        grid_spec=grid_spec,
        out_shape=jax.ShapeDtypeStruct((dec_batch * qh, V_DIM), F32),
        compiler_params=_cparams(("parallel", "arbitrary")),
        name="decode_attn",
    )(page_table.reshape(-1), q4, k4, v4, *([cache_k] * n_blk), *([cache_v] * n_blk),
      *lams, g_subln.reshape(1, -1))


def _mixer_out_kernel(x_ref, a_ref, u_ref, hist_ref, gate_ref, gpost_ref, wout_ref, wpool_ref,
                      ps_ref, o_ref, ext_scr, *, nb, t, seq_tiles, decode, pos_base):
    rows = nb * t
    cur = u_ref[...]
    ext_scr[:, HIST_PAD:HIST_PAD + t, :] = cur
    if decode:
        ext_scr[:, HIST_PAD - POOL_HIST:HIST_PAD, :] = hist_ref[...]
        base = pos_base
    else:
        first = (pl.program_id(0) % seq_tiles) == 0
        ext_scr[:, 0:HIST_PAD, :] = jnp.where(first, 0.0, hist_ref[...])
        base = (pl.program_id(0) % seq_tiles) * t
    pos = base + lax.broadcasted_iota(jnp.int32, (1, t, POOL_GROUP_WIDTH), 1)

    ys = []
    for gi, w in enumerate(POOL_WINDOWS):
        cs = slice(gi * POOL_GROUP_WIDTH, (gi + 1) * POOL_GROUP_WIDTH)
        cur_g = cur[:, :, cs]
        win = cur_g
        for k in range(1, w):
            win = win + ext_scr[:, HIST_PAD - k:HIST_PAD - k + t, cs]
        cnt = jnp.minimum(w, pos + 1).astype(F32)
        pooled = (win / cnt - cur_g).reshape(rows, POOL_GROUP_WIDTH)
        ys.append(jnp.dot(pooled.astype(BF16), wpool_ref[gi], preferred_element_type=F32))
    y = jnp.concatenate(ys, axis=1) * ps_ref[...]

    if decode:
        a = jnp.concatenate([a_ref[pl.ds(h, rows, stride=N_HEADS), :] for h in range(N_HEADS)],
                            axis=1).astype(BF16)
    else:
        a = a_ref[...]
    m = (jnp.dot(a, wout_ref[0:ATTN_WIDTH, :], preferred_element_type=F32)
         + jnp.dot(y.astype(BF16), wout_ref[ATTN_WIDTH:, :], preferred_element_type=F32))
    o_ref[...] = x_ref[...] + gate_ref[...] * _rms(m, gpost_ref[...])


def _mixer_out(x, a, u3, hist, mod, g_post, w_out_bf, w_pool_bf, pool_scale, *,
               nb, t, seq_tiles, decode, pos_base=0):
    rows_total = x.shape[0]
    tm = nb * t
    nt = rows_total // tm
    if decode:
        a_spec = pl.BlockSpec((tm * N_HEADS, V_DIM), lambda i: (i, 0))
        u_spec = pl.BlockSpec((nb, t, POOL_WIDTH), lambda i: (i, 0, 0))
        hist_spec = pl.BlockSpec((nb, POOL_HIST, POOL_WIDTH), lambda i: (i, 0, 0))
        gate_spec = pl.BlockSpec((tm, D_MODEL), lambda i: (i, 2))
    else:
        a_spec = pl.BlockSpec((tm, ATTN_WIDTH), lambda i: (i, 0))
        u_spec = pl.BlockSpec((1, t, POOL_WIDTH), lambda i: (i // seq_tiles, i % seq_tiles, 0))
        hpt = t // HIST_PAD
        hist_spec = pl.BlockSpec(
            (1, HIST_PAD, POOL_WIDTH),
            lambda i: (i // seq_tiles, jnp.maximum((i % seq_tiles) * hpt - 1, 0), 0))
        gate_spec = pl.BlockSpec((None, 1, D_MODEL), lambda i: (i // seq_tiles, 0, 2))
    return pl.pallas_call(
        functools.partial(_mixer_out_kernel, nb=nb, t=t, seq_tiles=seq_tiles, decode=decode,
                          pos_base=pos_base),
        grid=(nt,),
        in_specs=[
            pl.BlockSpec((tm, D_MODEL), lambda i: (i, 0)),
            a_spec, u_spec, hist_spec, gate_spec,
            pl.BlockSpec((1, D_MODEL), lambda i: (0, 0)),
            pl.BlockSpec((D_MODEL, D_MODEL), lambda i: (0, 0)),
            pl.BlockSpec((len(POOL_WINDOWS), POOL_GROUP_WIDTH, POOL_GROUP_WIDTH), lambda i: (0, 0, 0)),
            pl.BlockSpec((1, POOL_WIDTH), lambda i: (0, 0)),
        ],
        out_specs=pl.BlockSpec((tm, D_MODEL), lambda i: (i, 0)),
        out_shape=jax.ShapeDtypeStruct((rows_total, D_MODEL), F32),
        scratch_shapes=[pltpu.VMEM((nb, HIST_PAD + t, POOL_WIDTH), F32)],
        compiler_params=_cparams(("parallel",)),
        name="mixer_out_decode" if decode else "mixer_out_prompt",
    )(x, a, u3, hist, mod, g_post.reshape(1, -1), w_out_bf, w_pool_bf, pool_scale.reshape(1, -1))


def _ffn_kernel(x_ref, shift_ref, scale_ref, gate_ref, gpre_ref, gpost_ref, w1_ref, w2_ref,
                o_ref, h_scr, acc_scr, *, n_f):
    j = pl.program_id(1)

    @pl.when(j == 0)
    def _():
        h = _rms(x_ref[...], gpre_ref[...]) * (1.0 + scale_ref[...]) + shift_ref[...]
        h_scr[...] = h.astype(BF16)
        acc_scr[...] = jnp.zeros(acc_scr.shape, F32)

    t = jnp.dot(h_scr[...], w1_ref[...], preferred_element_type=F32)
    t = jnp.square(jnp.maximum(t, 0.0))
    acc_scr[...] += jnp.dot(t.astype(BF16), w2_ref[...], preferred_element_type=F32)

    @pl.when(j == n_f - 1)
    def _():
        o_ref[...] = x_ref[...] + gate_ref[...] * _rms(acc_scr[...], gpost_ref[...])


def _ffn(x, mod, g_pre, g_post, w1_bf, w2_bf, *, tm, tf, seq_tiles, decode):
    rows = x.shape[0]
    nt = rows // tm
    n_f = D_FF // tf
    if decode:
        mod_spec = lambda c: pl.BlockSpec((tm, D_MODEL), lambda i, j: (i, c))
    else:
        mod_spec = lambda c: pl.BlockSpec((None, 1, D_MODEL), lambda i, j: (i // seq_tiles, 0, c))
    vec_spec = pl.BlockSpec((1, D_MODEL), lambda i, j: (0, 0))
    return pl.pallas_call(
        functools.partial(_ffn_kernel, n_f=n_f),
        grid=(nt, n_f),
        in_specs=[
            pl.BlockSpec((tm, D_MODEL), lambda i, j: (i, 0)),
            mod_spec(3), mod_spec(4), mod_spec(5),
            vec_spec, vec_spec,
            pl.BlockSpec((D_MODEL, tf), lambda i, j: (0, j)),
            pl.BlockSpec((tf, D_MODEL), lambda i, j: (j, 0)),
        ],
        out_specs=pl.BlockSpec((tm, D_MODEL), lambda i, j: (i, 0)),
        out_shape=jax.ShapeDtypeStruct((rows, D_MODEL), F32),
        scratch_shapes=[pltpu.VMEM((tm, D_MODEL), BF16), pltpu.VMEM((tm, D_MODEL), F32)],
        compiler_params=_cparams(("parallel", "arbitrary")),
        name="ffn_decode" if decode else "ffn_prompt",
    )(x, mod, mod, mod, g_pre.reshape(1, -1), g_post.reshape(1, -1), w1_bf, w2_bf)


def kernel(x_prompt, x_sample, cache_k, cache_v, state_pool, page_table, c_prompt, c_sample,
           w_ada, b_ada, g_pre_mix, w_in, lambda_q1, lambda_k1, lambda_q2, lambda_k2, g_subln,
           w_pool, pool_scale, w_out, g_post_mix, g_pre_ffn, w_ff1, w_ff2, g_post_ffn):
    batch, seq, _ = x_prompt.shape
    dec_batch, dec_seq, _ = x_sample.shape
    depth = w_in.shape[0]
    past_len = page_table.shape[1] * PAGE_SIZE
    assert dec_seq == SUBLANES and state_pool.shape[2] == POOL_HIST

    tm_p = 512
    seq_tiles = seq // tm_p
    rows_s = dec_batch * dec_seq
    tables_p = _rope_tables(jnp.arange(seq, dtype=jnp.int32))
    tables_s = _rope_tables(past_len + (jnp.arange(rows_s, dtype=jnp.int32) % dec_seq))

    n_c = batch + dec_batch
    c_all = jnp.concatenate(
        [c_prompt, c_sample, jnp.zeros((-n_c % SUBLANES, D_MODEL), F32)], axis=0)

    xp = x_prompt.reshape(batch * seq, D_MODEL)
    xs = x_sample.reshape(rows_s, D_MODEL)
    outs = [[] for _ in range(6)]
    for l in range(depth):
        lam_init = 0.8 - 0.6 * math.exp(-0.3 * l)
        lams = [v[l].reshape(1, HEAD_DIM) for v in (lambda_q1, lambda_k1, lambda_q2, lambda_k2)]
        w_in_bf = w_in[l].astype(BF16)
        w_out_bf = w_out[l].astype(BF16)
        w_pool_bf = w_pool[l].astype(BF16)
        w1_bf = w_ff1[l].astype(BF16)
        w2_bf = w_ff2[l].astype(BF16)

        m_all = _ada(c_all, w_ada[l], b_ada[l])
        mod_p = m_all[:batch].reshape(batch, 1, N_MOD * D_MODEL)
        mod_s = jnp.repeat(m_all[batch:n_c], dec_seq, axis=0)

        q_bf, k4, k_bf, v4, v_bf, u = _mixer_in(
            xp, mod_p, g_pre_mix[l], w_in_bf, tables_p, tm=tm_p, seq_tiles=seq_tiles, decode=False)
        a_bf = _flash(q_bf, k_bf, v_bf, lams, g_subln[l], batch=batch, seq=seq, lam_init=lam_init)
        u3 = u.reshape(batch, seq, POOL_WIDTH)
        xp = _mixer_out(xp, a_bf, u3, u3, mod_p, g_post_mix[l], w_out_bf, w_pool_bf, pool_scale[l],
                        nb=1, t=tm_p, seq_tiles=seq_tiles, decode=False)
        xp = _ffn(xp, mod_p, g_pre_ffn[l], g_post_ffn[l], w1_bf, w2_bf,
                  tm=tm_p, tf=1024, seq_tiles=seq_tiles, decode=False)
        outs[0].append(k4.reshape(batch, seq, N_HEADS, V_DIM))
        outs[1].append(v4.reshape(batch, seq, N_HEADS, V_DIM))
        outs[2].append(u3[:, seq - POOL_HIST:])

        q4s, k4s, v4s, us = _mixer_in(
            xs, mod_s, g_pre_mix[l], w_in_bf, tables_s, tm=256, seq_tiles=1, decode=True)
        a4s = _decode(q4s, k4s, v4s, cache_k, cache_v, l, page_table, lams, g_subln[l],
                      dec_batch=dec_batch, dec_seq=dec_seq, lam_init=lam_init)
        us3 = us.reshape(dec_batch, dec_seq, POOL_WIDTH)
        xs = _mixer_out(xs, a4s, us3, state_pool[l], mod_s, g_post_mix[l], w_out_bf, w_pool_bf,
                        pool_scale[l], nb=16, t=dec_seq, seq_tiles=1, decode=True, pos_base=past_len)
        xs = _ffn(xs, mod_s, g_pre_ffn[l], g_post_ffn[l], w1_bf, w2_bf,
                  tm=256, tf=1024, seq_tiles=1, decode=True)
        outs[3].append(k4s.reshape(dec_batch, dec_seq, N_HEADS, V_DIM))
        outs[4].append(v4s.reshape(dec_batch, dec_seq, N_HEADS, V_DIM))
        outs[5].append(jnp.concatenate([state_pool[l][:, dec_seq:], us3], axis=1))

    kp, vp, sp, ks, vs, ss = (jnp.stack(o) for o in outs)
    return (xp.reshape(batch, seq, D_MODEL), xs.reshape(dec_batch, dec_seq, D_MODEL),
            kp, vp, sp, ks, vs, ss)
```

```python
import functools
import math

import jax
import jax.numpy as jnp
from jax import lax
from jax.experimental import pallas as pl
from jax.experimental.pallas import tpu as pltpu

F32 = jnp.float32
BF16 = jnp.bfloat16

D_MODEL = 2048
N_HEADS = 8
HEAD_DIM = 64
V_DIM = 2 * HEAD_DIM
ATTN_WIDTH = N_HEADS * V_DIM
POOL_WIDTH = 1024
IN_WIDTH = 3 * ATTN_WIDTH + POOL_WIDTH
ROT_DIM = HEAD_DIM // 4
ROPE_THETA = 500000.0
POOL_WINDOWS = (2, 4, 8, 16)
POOL_GROUP_WIDTH = POOL_WIDTH // len(POOL_WINDOWS)
POOL_HIST = max(POOL_WINDOWS) - 1
HIST_PAD = 16
D_FF = 4 * D_MODEL
N_MOD = 6
PAGE_SIZE = 128
EPS = 1e-6
NEG_BIG = -1e30

LANES = 128
SUBLANES = 8
VMEM_LIMIT = 56 * 1024 * 1024
RING_SLOTS = 3


def _cparams(sem):
    return pltpu.CompilerParams(dimension_semantics=sem, vmem_limit_bytes=VMEM_LIMIT)


def _rms(x, g):
    ms = jnp.mean(x * x, axis=-1, keepdims=True)
    return x * lax.rsqrt(ms + EPS) * g


def _ada_kernel(c_ref, w_ref, b_ref, o_ref):
    c = c_ref[...]
    s = c * (1.0 / (1.0 + jnp.exp(-c)))
    o_ref[...] = jnp.dot(s.astype(BF16), w_ref[...].astype(BF16),
                         preferred_element_type=F32) + b_ref[...]


def _ada(c_all, w_ada, b_ada):
    rows = c_all.shape[0]
    tn = 1024
    return pl.pallas_call(
        _ada_kernel,
        grid=(N_MOD * D_MODEL // tn,),
        in_specs=[
            pl.BlockSpec((rows, D_MODEL), lambda j: (0, 0)),
            pl.BlockSpec((D_MODEL, tn), lambda j: (0, j)),
            pl.BlockSpec((1, tn), lambda j: (0, j)),
        ],
        out_specs=pl.BlockSpec((rows, tn), lambda j: (0, j)),
        out_shape=jax.ShapeDtypeStruct((rows, N_MOD * D_MODEL), F32),
        compiler_params=_cparams(("arbitrary",)),
        name="ada",
    )(c_all, w_ada, b_ada.reshape(1, -1))


def _rope(z, cos_ref, sa_ref, sb_ref):
    return (z * cos_ref[...]
            + pltpu.roll(z, LANES - ROT_DIM // 2, 1) * sa_ref[...]
            + pltpu.roll(z, ROT_DIM // 2, 1) * sb_ref[...])


def _mixer_in_kernel(x_ref, shift_ref, scale_ref, g_ref, w_ref, cos_ref, sa_ref, sb_ref,
                     *refs, tm, decode):
    if decode:
        q4_ref, k4_ref, v4_ref, u_ref, h_scr = refs
    else:
        qb_ref, k4_ref, kb_ref, v4_ref, vb_ref, u_ref, h_scr = refs
    j = pl.program_id(1)

    @pl.when(j == 0)
    def _():
        h = _rms(x_ref[...], g_ref[...]) * (1.0 + scale_ref[...]) + shift_ref[...]
        h_scr[...] = h.astype(BF16)

    def heads():
        for hp in range(N_HEADS // 2):
            z2 = jnp.dot(h_scr[...], w_ref[:, 2 * hp * V_DIM:2 * (hp + 1) * V_DIM],
                         preferred_element_type=F32)
            for h in (2 * hp, 2 * hp + 1):
                yield h, slice(h * V_DIM, (h + 1) * V_DIM), z2[:, (h % 2) * V_DIM:(h % 2 + 1) * V_DIM]

    @pl.when(j == 0)
    def _():
        for h, cols, zh in heads():
            q = _rope(zh, cos_ref, sa_ref, sb_ref) * (HEAD_DIM ** -0.5)
            if decode:
                q4_ref[pl.ds(h, tm, stride=N_HEADS), :] = q
            else:
                qb_ref[:, cols] = q.astype(BF16)

    @pl.when(j == 1)
    def _():
        for h, cols, zh in heads():
            k = _rope(zh, cos_ref, sa_ref, sb_ref)
            k4_ref[pl.ds(h, tm, stride=N_HEADS), :] = k
            if not decode:
                kb_ref[:, cols] = k.astype(BF16)

    @pl.when(j == 2)
    def _():
        for h, cols, zh in heads():
            v4_ref[pl.ds(h, tm, stride=N_HEADS), :] = zh
            if not decode:
                vb_ref[:, cols] = zh.astype(BF16)

    @pl.when(j == 3)
    def _():
        u_ref[...] = jnp.dot(h_scr[...], w_ref[...], preferred_element_type=F32)


def _mixer_in(x, mod, g_pre, w_in_bf, tables, *, tm, seq_tiles, decode):
    rows = x.shape[0]
    nt = rows // tm
    if decode:
        mod_spec = lambda c: pl.BlockSpec((tm, D_MODEL), lambda i, j: (i, c))
        tab_spec = pl.BlockSpec((tm, LANES), lambda i, j: (i, 0))
    else:
        mod_spec = lambda c: pl.BlockSpec((None, 1, D_MODEL), lambda i, j: (i // seq_tiles, 0, c))
        tab_spec = pl.BlockSpec((tm, LANES), lambda i, j: (i % seq_tiles, 0))
    row_bf = pl.BlockSpec((tm, ATTN_WIDTH), lambda i, j: (i, 0))
    row4 = pl.BlockSpec((tm * N_HEADS, V_DIM), lambda i, j: (i, 0))
    sd_bf = jax.ShapeDtypeStruct((rows, ATTN_WIDTH), BF16)
    sd4 = jax.ShapeDtypeStruct((rows * N_HEADS, V_DIM), F32)
    sd_u = jax.ShapeDtypeStruct((rows, POOL_WIDTH), F32)
    if decode:
        out_specs = [row4, row4, row4, row_bf]
        out_shape = [sd4, sd4, sd4, sd_u]
    else:
        out_specs = [row_bf, row4, row_bf, row4, row_bf, row_bf]
        out_shape = [sd_bf, sd4, sd_bf, sd4, sd_bf, sd_u]
    return pl.pallas_call(
        functools.partial(_mixer_in_kernel, tm=tm, decode=decode),
        grid=(nt, IN_WIDTH // ATTN_WIDTH),
        in_specs=[
            pl.BlockSpec((tm, D_MODEL), lambda i, j: (i, 0)),
            mod_spec(0), mod_spec(1),
            pl.BlockSpec((1, D_MODEL), lambda i, j: (0, 0)),
            pl.BlockSpec((D_MODEL, ATTN_WIDTH), lambda i, j: (0, j)),
            tab_spec, tab_spec, tab_spec,
        ],
        out_specs=out_specs,
        out_shape=out_shape,
        scratch_shapes=[pltpu.VMEM((tm, D_MODEL), BF16)],
        compiler_params=_cparams(("parallel", "arbitrary")),
        name="mixer_in_decode" if decode else "mixer_in_prompt",
    )(x, mod, mod, g_pre.reshape(1, -1), w_in_bf, *tables)


def _rope_tables(pos):
    half = ROT_DIM // 2
    inv = 1.0 / (ROPE_THETA ** (jnp.arange(0, ROT_DIM, 2, dtype=F32) / ROT_DIM))
    ang = pos.astype(F32)[:, None] * inv[None, :]
    cos, sin = jnp.cos(ang), jnp.sin(ang)
    lane = jnp.arange(LANES) % HEAD_DIM
    first = (lane < half)[None, :]
    second = ((lane >= half) & (lane < ROT_DIM))[None, :]
    cos_l = jnp.take(cos, lane % half, axis=1)
    sin_l = jnp.take(sin, lane % half, axis=1)
    c = jnp.where(first | second, cos_l, 1.0)
    sa = jnp.where(first, -sin_l, 0.0)
    sb = jnp.where(second, sin_l, 0.0)
    return c.astype(F32), sa.astype(F32), sb.astype(F32)


def _lam(lq1, lk1, lq2, lk2, lam_init):
    a = jnp.sum(lq1[...] * lk1[...], axis=-1, keepdims=True)
    b = jnp.sum(lq2[...] * lk2[...], axis=-1, keepdims=True)
    return jnp.exp(a) - jnp.exp(b) + lam_init


def _flash_kernel(q_ref, k_ref, v_ref, lq1, lk1, lq2, lk2, g_ref, o_ref,
                  vt_scr, s_scr, *, tq, nq, lam_init):
    i = pl.program_id(2)
    vq = 2 * tq
    grp = tq // SUBLANES

    @pl.when(i == 0)
    def _():
        vt_scr[...] = v_ref[...].astype(F32).T.astype(BF16)

    q = q_ref[...].astype(F32)
    lane = lax.broadcasted_iota(jnp.int32, (tq, V_DIM), 1)
    qq = jnp.concatenate([jnp.where(lane < HEAD_DIM, q, 0.0),
                          jnp.where(lane >= HEAD_DIM, q, 0.0)], axis=0).astype(BF16)
    lam = _lam(lq1, lk1, lq2, lk2, lam_init)

    def block(ii):
        mx = jnp.full((SUBLANES, vq), NEG_BIG, F32)
        for c in range(ii + 1):
            rows = slice(c * tq, (c + 1) * tq)
            s = lax.dot_general(k_ref[rows, :], qq, (((1,), (1,)), ((), ())),
                                preferred_element_type=F32)
            if c == ii:
                kv = lax.broadcasted_iota(jnp.int32, (tq, vq), 0)
                qp = lax.broadcasted_iota(jnp.int32, (tq, vq), 1) & (tq - 1)
                s = jnp.where(kv <= qp, s, NEG_BIG)
            s_scr[rows, :] = s
            mx = jnp.maximum(mx, jnp.max(s.reshape(grp, SUBLANES, vq), axis=0))
        m = jnp.max(mx, axis=0, keepdims=True)
        ls = jnp.zeros((SUBLANES, vq), F32)
        o_t = jnp.zeros((V_DIM, vq), F32)
        for c in range(ii + 1):
            rows = slice(c * tq, (c + 1) * tq)
            p = jnp.exp(s_scr[rows, :] - m)
            ls = ls + jnp.sum(p.reshape(grp, SUBLANES, vq), axis=0)
            o_t = o_t + jnp.dot(vt_scr[:, rows], p.astype(BF16), preferred_element_type=F32)
        n_t = o_t / jnp.sum(ls, axis=0, keepdims=True)
        out = (n_t[:, :tq] - lam * n_t[:, tq:]).T
        o_ref[...] = (_rms(out, g_ref[...]) * (1.0 - lam_init)).astype(BF16)

    for ii in range(nq):
        pl.when(i == ii)(functools.partial(block, ii))


def _flash(q_bf, k_bf, v_bf, lams, g_subln, *, batch, seq, lam_init, tq=256):
    nq = seq // tq
    lam_spec = pl.BlockSpec((1, HEAD_DIM), lambda b, h, i: (0, 0))
    return pl.pallas_call(
        functools.partial(_flash_kernel, tq=tq, nq=nq, lam_init=lam_init),
        grid=(batch, N_HEADS, nq),
        in_specs=[
            pl.BlockSpec((tq, V_DIM), lambda b, h, i: (b * nq + i, h)),
            pl.BlockSpec((seq, V_DIM), lambda b, h, i: (b, h)),
            pl.BlockSpec((seq, V_DIM), lambda b, h, i: (b, h)),
            lam_spec, lam_spec, lam_spec, lam_spec,
            pl.BlockSpec((1, V_DIM), lambda b, h, i: (0, 0)),
        ],
        out_specs=pl.BlockSpec((tq, V_DIM), lambda b, h, i: (b * nq + i, h)),
        out_shape=jax.ShapeDtypeStruct((batch * seq, ATTN_WIDTH), BF16),
        scratch_shapes=[pltpu.VMEM((V_DIM, seq), BF16), pltpu.VMEM((seq, 2 * tq), F32)],
        compiler_params=_cparams(("parallel", "parallel", "arbitrary")),
        name="flash_prompt",
    )(q_bf, k_bf, v_bf, *lams, g_subln.reshape(1, -1))


def _decode_kernel(pt_ref, q_ref, kn_ref, vn_ref, ck_hbm, cv_hbm, lq1, lk1, lq2, lk2, g_ref, o_ref,
                   kbuf, vbuf, ksem, vsem, m_scr, l_scr, acc_scr, *, layer, n_pages, n_groups,
                   n_steps, dec_seq, lam_init):
    n_seqs = 1
    g = pl.program_id(1)
    t = pl.program_id(0) * n_groups + g
    qh = dec_seq * N_HEADS
    page_rows = PAGE_SIZE * N_HEADS

    def page_copies(step, slot):
        cps = []
        for p in range(n_pages):
            page = pt_ref[step * n_pages + p]
            cps.append(pltpu.make_async_copy(ck_hbm.at[layer, page], kbuf.at[slot, p], ksem.at[slot]))
            cps.append(pltpu.make_async_copy(cv_hbm.at[layer, page], vbuf.at[slot, p], vsem.at[slot]))
        return cps

    ahead = RING_SLOTS - 1

    @pl.when(t == 0)
    def _():
        for s in range(min(ahead, n_steps)):
            for cp in page_copies(s, s):
                cp.start()

    @pl.when(t + ahead < n_steps)
    def _():
        for cp in page_copies(t + ahead, lax.rem(t + ahead, RING_SLOTS)):
            cp.start()

    slot = lax.rem(t, RING_SLOTS)
    for cp in page_copies(t, slot):
        cp.wait()
    k_refs = [kbuf.at[slot, p] for p in range(n_pages)]
    v_refs = [vbuf.at[slot, p] for p in range(n_pages)]

    @pl.when(g == 0)
    def _():
        m_scr[...] = jnp.full(m_scr.shape, NEG_BIG, F32)
        l_scr[...] = jnp.zeros(l_scr.shape, F32)
        acc_scr[...] = jnp.zeros(acc_scr.shape, F32)

    sub = lax.broadcasted_iota(jnp.int32, (SUBLANES, LANES), 0)
    lane = lax.broadcasted_iota(jnp.int32, (SUBLANES, LANES), 1)
    diag = (lane & (N_HEADS - 1)) == sub
    lane_q = lax.broadcasted_iota(jnp.int32, (qh, V_DIM), 1)
    lam = _lam(lq1, lk1, lq2, lk2, lam_init)

    def query_cols(b):
        r = q_ref[b * qh:(b + 1) * qh, :]
        return jnp.concatenate([jnp.where(lane_q < HEAD_DIM, r, 0.0),
                                jnp.where(lane_q >= HEAD_DIM, r, 0.0)], axis=0).astype(BF16)

    def update(b, s_list, v_list):
        m_prev = m_scr[b]
        m_new = m_prev
        for s in s_list:
            m_new = jnp.maximum(m_new, jnp.max(s, axis=0))
        alpha = jnp.exp(m_prev - m_new)
        m_eff = jnp.where(diag, m_new, -NEG_BIG)
        l_new = alpha * l_scr[b]
        pv = jnp.zeros((V_DIM, LANES), F32)
        for s, v in zip(s_list, v_list):
            p = jnp.exp(s - m_eff[None])
            l_new = l_new + jnp.sum(p, axis=0)
            pm = p.reshape(v.shape[0], LANES).astype(BF16)
            pv = pv + lax.dot_general(v.astype(BF16), pm, (((0,), (0,)), ((), ())),
                                      preferred_element_type=F32)
        alpha_row = jnp.sum(jnp.where(diag, alpha, 0.0), axis=0, keepdims=True)
        acc_scr[b] = acc_scr[b] * alpha_row + pv
        l_scr[b] = l_new
        m_scr[b] = m_new

    for b in range(n_seqs):
        qt = query_cols(b)
        pages = range(b * n_pages, (b + 1) * n_pages)
        s_list = [
            lax.dot_general(k_refs[p][...].reshape(page_rows, V_DIM).astype(BF16), qt,
                            (((1,), (1,)), ((), ())), preferred_element_type=F32
                            ).reshape(PAGE_SIZE, N_HEADS, LANES) for p in pages]
        update(b, s_list, [v_refs[p][...].reshape(page_rows, V_DIM) for p in pages])

    @pl.when(g == n_groups - 1)
    def _():
        for b in range(n_seqs):
            tok = slice(b * qh, (b + 1) * qh)
            s_new = lax.dot_general(kn_ref[tok, :].astype(BF16), query_cols(b),
                                    (((1,), (1,)), ((), ())), preferred_element_type=F32)
            s_new = s_new.reshape(dec_seq, N_HEADS, LANES)
            pos = lax.broadcasted_iota(jnp.int32, s_new.shape, 0)
            qidx = (lax.broadcasted_iota(jnp.int32, s_new.shape, 2) & (qh - 1)) >> 3
            update(b, [jnp.where(pos <= qidx, s_new, NEG_BIG)], [vn_ref[tok, :]])
            l_row = jnp.sum(jnp.where(diag, l_scr[b], 0.0), axis=0, keepdims=True)
            nt = (acc_scr[b] / l_row).T
            out = nt[:qh] - lam * nt[qh:]
            o_ref[tok, :] = _rms(out, g_ref[...]) * (1.0 - lam_init)


def _decode(q4, k4, v4, cache_k, cache_v, layer, page_table, lams, g_subln, *,
            dec_batch, dec_seq, lam_init, n_pages=8):
    pages_per_seq = page_table.shape[1]
    assert pages_per_seq % n_pages == 0
    n_groups = pages_per_seq // n_pages
    qh = dec_seq * N_HEADS
    tok_spec = pl.BlockSpec((qh, V_DIM), lambda b, g, pt: (b, 0))
    lam_spec = pl.BlockSpec((1, HEAD_DIM), lambda b, g, pt: (0, 0))
    hbm_spec = pl.BlockSpec(memory_space=pl.ANY)
    page_buf = pltpu.VMEM((RING_SLOTS, n_pages, PAGE_SIZE, N_HEADS, V_DIM), F32)
    grid_spec = pltpu.PrefetchScalarGridSpec(
        num_scalar_prefetch=1,
        grid=(dec_batch, n_groups),
        in_specs=[tok_spec, tok_spec, tok_spec, hbm_spec, hbm_spec]
        + [lam_spec] * 4 + [pl.BlockSpec((1, V_DIM), lambda b, g, pt: (0, 0))],
        out_specs=tok_spec,
        scratch_shapes=[page_buf, page_buf,
                        pltpu.SemaphoreType.DMA((RING_SLOTS,)), pltpu.SemaphoreType.DMA((RING_SLOTS,)),
                        pltpu.VMEM((1, SUBLANES, LANES), F32), pltpu.VMEM((1, SUBLANES, LANES), F32),
                        pltpu.VMEM((1, V_DIM, LANES), F32)],
    )
    return pl.pallas_call(
        functools.partial(_decode_kernel, layer=layer, n_pages=n_pages, n_groups=n_groups,
                          n_steps=dec_batch * n_groups, dec_seq=dec_seq, lam_init=lam_init),
        grid_spec=grid_spec,
        out_shape=jax.ShapeDtypeStruct((dec_batch * qh, V_DIM), F32),
        compiler_params=_cparams(("arbitrary", "arbitrary")),
        name="decode_attn",
    )(page_table.reshape(-1), q4, k4, v4, cache_k, cache_v, *lams, g_subln.reshape(1, -1))


def _mixer_out_kernel(x_ref, a_ref, u_ref, hist_ref, gate_ref, gpost_ref, wout_ref, wpool_ref,
                      ps_ref, o_ref, ext_scr, *, nb, t, seq_tiles, decode, pos_base):
    rows = nb * t
    cur = u_ref[...]
    ext_scr[:, HIST_PAD:HIST_PAD + t, :] = cur
    if decode:
        ext_scr[:, HIST_PAD - POOL_HIST:HIST_PAD, :] = hist_ref[...]
        base = pos_base
    else:
        first = (pl.program_id(0) % seq_tiles) == 0
        ext_scr[:, 0:HIST_PAD, :] = jnp.where(first, 0.0, hist_ref[...])
        base = (pl.program_id(0) % seq_tiles) * t
    pos = base + lax.broadcasted_iota(jnp.int32, (1, t, POOL_GROUP_WIDTH), 1)

    ys = []
    for gi, w in enumerate(POOL_WINDOWS):
        cs = slice(gi * POOL_GROUP_WIDTH, (gi + 1) * POOL_GROUP_WIDTH)
        cur_g = cur[:, :, cs]
        win = cur_g
        for k in range(1, w):
            win = win + ext_scr[:, HIST_PAD - k:HIST_PAD - k + t, cs]
        cnt = jnp.minimum(w, pos + 1).astype(F32)
        pooled = (win / cnt - cur_g).reshape(rows, POOL_GROUP_WIDTH)
        ys.append(jnp.dot(pooled.astype(BF16), wpool_ref[gi], preferred_element_type=F32))
    y = jnp.concatenate(ys, axis=1) * ps_ref[...]

    if decode:
        a = jnp.concatenate([a_ref[pl.ds(h, rows, stride=N_HEADS), :] for h in range(N_HEADS)],
                            axis=1).astype(BF16)
    else:
        a = a_ref[...]
    m = (jnp.dot(a, wout_ref[0:ATTN_WIDTH, :], preferred_element_type=F32)
         + jnp.dot(y.astype(BF16), wout_ref[ATTN_WIDTH:, :], preferred_element_type=F32))
    o_ref[...] = x_ref[...] + gate_ref[...] * _rms(m, gpost_ref[...])


def _mixer_out(x, a, u3, hist, mod, g_post, w_out_bf, w_pool_bf, pool_scale, *,
               nb, t, seq_tiles, decode, pos_base=0):
    rows_total = x.shape[0]
    tm = nb * t
    nt = rows_total // tm
    if decode:
        a_spec = pl.BlockSpec((tm * N_HEADS, V_DIM), lambda i: (i, 0))
        u_spec = pl.BlockSpec((nb, t, POOL_WIDTH), lambda i: (i, 0, 0))
        hist_spec = pl.BlockSpec((nb, POOL_HIST, POOL_WIDTH), lambda i: (i, 0, 0))
        gate_spec = pl.BlockSpec((tm, D_MODEL), lambda i: (i, 2))
    else:
        a_spec = pl.BlockSpec((tm, ATTN_WIDTH), lambda i: (i, 0))
        u_spec = pl.BlockSpec((1, t, POOL_WIDTH), lambda i: (i // seq_tiles, i % seq_tiles, 0))
        hpt = t // HIST_PAD
        hist_spec = pl.BlockSpec(
            (1, HIST_PAD, POOL_WIDTH),
            lambda i: (i // seq_tiles, jnp.maximum((i % seq_tiles) * hpt - 1, 0), 0))
        gate_spec = pl.BlockSpec((None, 1, D_MODEL), lambda i: (i // seq_tiles, 0, 2))
    return pl.pallas_call(
        functools.partial(_mixer_out_kernel, nb=nb, t=t, seq_tiles=seq_tiles, decode=decode,
                          pos_base=pos_base),
        grid=(nt,),
        in_specs=[
            pl.BlockSpec((tm, D_MODEL), lambda i: (i, 0)),
            a_spec, u_spec, hist_spec, gate_spec,
            pl.BlockSpec((1, D_MODEL), lambda i: (0, 0)),
            pl.BlockSpec((D_MODEL, D_MODEL), lambda i: (0, 0)),
            pl.BlockSpec((len(POOL_WINDOWS), POOL_GROUP_WIDTH, POOL_GROUP_WIDTH), lambda i: (0, 0, 0)),
            pl.BlockSpec((1, POOL_WIDTH), lambda i: (0, 0)),
        ],
        out_specs=pl.BlockSpec((tm, D_MODEL), lambda i: (i, 0)),
        out_shape=jax.ShapeDtypeStruct((rows_total, D_MODEL), F32),
        scratch_shapes=[pltpu.VMEM((nb, HIST_PAD + t, POOL_WIDTH), F32)],
        compiler_params=_cparams(("parallel",)),
        name="mixer_out_decode" if decode else "mixer_out_prompt",
    )(x, a, u3, hist, mod, g_post.reshape(1, -1), w_out_bf, w_pool_bf, pool_scale.reshape(1, -1))


def _ffn_kernel(x_ref, shift_ref, scale_ref, gate_ref, gpre_ref, gpost_ref, w1_ref, w2_ref,
                o_ref, h_scr, acc_scr, *, n_f):
    j = pl.program_id(1)

    @pl.when(j == 0)
    def _():
        h = _rms(x_ref[...], gpre_ref[...]) * (1.0 + scale_ref[...]) + shift_ref[...]
        h_scr[...] = h.astype(BF16)
        acc_scr[...] = jnp.zeros(acc_scr.shape, F32)

    t = jnp.dot(h_scr[...], w1_ref[...], preferred_element_type=F32)
    t = jnp.square(jnp.maximum(t, 0.0))
    acc_scr[...] += jnp.dot(t.astype(BF16), w2_ref[...], preferred_element_type=F32)

    @pl.when(j == n_f - 1)
    def _():
        o_ref[...] = x_ref[...] + gate_ref[...] * _rms(acc_scr[...], gpost_ref[...])


def _ffn(x, mod, g_pre, g_post, w1_bf, w2_bf, *, tm, tf, seq_tiles, decode):
    rows = x.shape[0]
    nt = rows // tm
    n_f = D_FF // tf
    if decode:
        mod_spec = lambda c: pl.BlockSpec((tm, D_MODEL), lambda i, j: (i, c))
    else:
        mod_spec = lambda c: pl.BlockSpec((None, 1, D_MODEL), lambda i, j: (i // seq_tiles, 0, c))
    vec_spec = pl.BlockSpec((1, D_MODEL), lambda i, j: (0, 0))
    return pl.pallas_call(
        functools.partial(_ffn_kernel, n_f=n_f),
        grid=(nt, n_f),
        in_specs=[
            pl.BlockSpec((tm, D_MODEL), lambda i, j: (i, 0)),
            mod_spec(3), mod_spec(4), mod_spec(5),
            vec_spec, vec_spec,
            pl.BlockSpec((D_MODEL, tf), lambda i, j: (0, j)),
            pl.BlockSpec((tf, D_MODEL), lambda i, j: (j, 0)),
        ],
        out_specs=pl.BlockSpec((tm, D_MODEL), lambda i, j: (i, 0)),
        out_shape=jax.ShapeDtypeStruct((rows, D_MODEL), F32),
        scratch_shapes=[pltpu.VMEM((tm, D_MODEL), BF16), pltpu.VMEM((tm, D_MODEL), F32)],
        compiler_params=_cparams(("parallel", "arbitrary")),
        name="ffn_decode" if decode else "ffn_prompt",
    )(x, mod, mod, mod, g_pre.reshape(1, -1), g_post.reshape(1, -1), w1_bf, w2_bf)


def kernel(x_prompt, x_sample, cache_k, cache_v, state_pool, page_table, c_prompt, c_sample,
           w_ada, b_ada, g_pre_mix, w_in, lambda_q1, lambda_k1, lambda_q2, lambda_k2, g_subln,
           w_pool, pool_scale, w_out, g_post_mix, g_pre_ffn, w_ff1, w_ff2, g_post_ffn):
    batch, seq, _ = x_prompt.shape
    dec_batch, dec_seq, _ = x_sample.shape
    depth = w_in.shape[0]
    past_len = page_table.shape[1] * PAGE_SIZE
    assert dec_seq == SUBLANES and state_pool.shape[2] == POOL_HIST

    tm_p = 512
    seq_tiles = seq // tm_p
    rows_s = dec_batch * dec_seq
    tables_p = _rope_tables(jnp.arange(seq, dtype=jnp.int32))
    tables_s = _rope_tables(past_len + (jnp.arange(rows_s, dtype=jnp.int32) % dec_seq))

    n_c = batch + dec_batch
    c_all = jnp.concatenate(
        [c_prompt, c_sample, jnp.zeros((-n_c % SUBLANES, D_MODEL), F32)], axis=0)

    xp = x_prompt.reshape(batch * seq, D_MODEL)
    xs = x_sample.reshape(rows_s, D_MODEL)
    outs = [[] for _ in range(6)]
    for l in range(depth):
        lam_init = 0.8 - 0.6 * math.exp(-0.3 * l)
        lams = [v[l].reshape(1, HEAD_DIM) for v in (lambda_q1, lambda_k1, lambda_q2, lambda_k2)]
        w_in_bf = w_in[l].astype(BF16)
        w_out_bf = w_out[l].astype(BF16)
        w_pool_bf = w_pool[l].astype(BF16)
        w1_bf = w_ff1[l].astype(BF16)
        w2_bf = w_ff2[l].astype(BF16)

        m_all = _ada(c_all, w_ada[l], b_ada[l])
        mod_p = m_all[:batch].reshape(batch, 1, N_MOD * D_MODEL)
        mod_s = jnp.repeat(m_all[batch:n_c], dec_seq, axis=0)

        q_bf, k4, k_bf, v4, v_bf, u = _mixer_in(
            xp, mod_p, g_pre_mix[l], w_in_bf, tables_p, tm=tm_p, seq_tiles=seq_tiles, decode=False)
        a_bf = _flash(q_bf, k_bf, v_bf, lams, g_subln[l], batch=batch, seq=seq, lam_init=lam_init)
        u3 = u.reshape(batch, seq, POOL_WIDTH)
        xp = _mixer_out(xp, a_bf, u3, u3, mod_p, g_post_mix[l], w_out_bf, w_pool_bf, pool_scale[l],
                        nb=1, t=tm_p, seq_tiles=seq_tiles, decode=False)
        xp = _ffn(xp, mod_p, g_pre_ffn[l], g_post_ffn[l], w1_bf, w2_bf,
                  tm=tm_p, tf=1024, seq_tiles=seq_tiles, decode=False)
        outs[0].append(k4.reshape(batch, seq, N_HEADS, V_DIM))
        outs[1].append(v4.reshape(batch, seq, N_HEADS, V_DIM))
        outs[2].append(u3[:, seq - POOL_HIST:])

        q4s, k4s, v4s, us = _mixer_in(
            xs, mod_s, g_pre_mix[l], w_in_bf, tables_s, tm=256, seq_tiles=1, decode=True)
        a4s = _decode(q4s, k4s, v4s, cache_k, cache_v, l, page_table, lams, g_subln[l],
                      dec_batch=dec_batch, dec_seq=dec_seq, lam_init=lam_init)
        us3 = us.reshape(dec_batch, dec_seq, POOL_WIDTH)
        xs = _mixer_out(xs, a4s, us3, state_pool[l], mod_s, g_post_mix[l], w_out_bf, w_pool_bf,
                        pool_scale[l], nb=16, t=dec_seq, seq_tiles=1, decode=True, pos_base=past_len)
        xs = _ffn(xs, mod_s, g_pre_ffn[l], g_post_ffn[l], w1_bf, w2_bf,
                  tm=256, tf=1024, seq_tiles=1, decode=True)
        outs[3].append(k4s.reshape(dec_batch, dec_seq, N_HEADS, V_DIM))
        outs[4].append(v4s.reshape(dec_batch, dec_seq, N_HEADS, V_DIM))
        outs[5].append(jnp.concatenate([state_pool[l][:, dec_seq:], us3], axis=1))

    kp, vp, sp, ks, vs, ss = (jnp.stack(o) for o in outs)
    return (xp.reshape(batch, seq, D_MODEL), xs.reshape(dec_batch, dec_seq, D_MODEL),
            kp, vp, sp, ks, vs, ss)
```

```python
import functools
import math

import jax
import jax.numpy as jnp
import numpy as np
from jax import lax
from jax.experimental import pallas as pl
from jax.experimental.pallas import tpu as pltpu

F32 = jnp.float32
BF16 = jnp.bfloat16

D_MODEL = 2048
N_HEADS = 8
HEAD_DIM = 64
V_DIM = 2 * HEAD_DIM
ATTN_WIDTH = N_HEADS * V_DIM
POOL_WIDTH = 1024
IN_WIDTH = 3 * ATTN_WIDTH + POOL_WIDTH
ROT_DIM = HEAD_DIM // 4
ROPE_THETA = 500000.0
POOL_WINDOWS = (2, 4, 8, 16)
POOL_GROUP_WIDTH = POOL_WIDTH // len(POOL_WINDOWS)
POOL_HIST = max(POOL_WINDOWS) - 1
HIST_PAD = 16
D_FF = 4 * D_MODEL
N_MOD = 6
PAGE_SIZE = 128
EPS = 1e-6
NEG_BIG = -1e30
LOG2_E = math.log2(math.e)

LANES = 128
SUBLANES = 8
BF16_ROWS = 16
VMEM_LIMIT = 56 * 1024 * 1024
RING_SLOTS = 3


def _cparams(sem):
    return pltpu.CompilerParams(dimension_semantics=sem, vmem_limit_bytes=VMEM_LIMIT)


def _rms(x, g):
    ms = jnp.mean(x * x, axis=-1, keepdims=True)
    return x * lax.rsqrt(ms + EPS) * g


def _mod_rows(ref, rows):
    m = ref[...]
    n, d = m.shape
    if n == 1:
        return m
    return jnp.broadcast_to(m[:, None, :], (n, rows // n, d)).reshape(rows, d)


def _ada_kernel(c_ref, w_ref, b_ref, o_ref):
    c = c_ref[...]
    s = c * (1.0 / (1.0 + jnp.exp(-c)))
    o_ref[...] = jnp.dot(s.astype(BF16), w_ref[...].astype(BF16),
                         preferred_element_type=F32) + b_ref[...]


def _ada(c_all, w_ada, b_ada):
    rows = c_all.shape[0]
    tn = 1024
    return pl.pallas_call(
        _ada_kernel,
        grid=(N_MOD * D_MODEL // tn,),
        in_specs=[
            pl.BlockSpec((rows, D_MODEL), lambda j: (0, 0)),
            pl.BlockSpec((D_MODEL, tn), lambda j: (0, j)),
            pl.BlockSpec((1, tn), lambda j: (0, j)),
        ],
        out_specs=pl.BlockSpec((rows, tn), lambda j: (0, j)),
        out_shape=jax.ShapeDtypeStruct((rows, N_MOD * D_MODEL), F32),
        compiler_params=_cparams(("arbitrary",)),
        name="ada",
    )(c_all, w_ada, b_ada.reshape(1, -1))


def _rope(z, cos_ref, sa_ref, sb_ref):
    return (z * cos_ref[...]
            + pltpu.roll(z, LANES - ROT_DIM // 2, 1) * sa_ref[...]
            + pltpu.roll(z, ROT_DIM // 2, 1) * sb_ref[...])


def _mixer_in_kernel(x_ref, shift_ref, scale_ref, g_ref, w_ref, cos_ref, sa_ref, sb_ref,
                     *refs, tm, decode):
    if decode:
        q4_ref, k4_ref, v4_ref, u_ref, h_scr = refs
    else:
        qb_ref, k4_ref, kb_ref, v4_ref, vb_ref, u_ref, h_scr = refs
    j = pl.program_id(1)

    @pl.when(j == 0)
    def _():
        h = (_rms(x_ref[...], g_ref[...]) * (1.0 + _mod_rows(scale_ref, tm))
             + _mod_rows(shift_ref, tm))
        h_scr[...] = h.astype(BF16)

    def heads():
        for hp in range(N_HEADS // 2):
            z2 = jnp.dot(h_scr[...], w_ref[:, 2 * hp * V_DIM:2 * (hp + 1) * V_DIM],
                         preferred_element_type=F32)
            for h in (2 * hp, 2 * hp + 1):
                yield h, slice(h * V_DIM, (h + 1) * V_DIM), z2[:, (h % 2) * V_DIM:(h % 2 + 1) * V_DIM]

    @pl.when(j == 0)
    def _():
        for h, cols, zh in heads():
            q = _rope(zh, cos_ref, sa_ref, sb_ref) * (HEAD_DIM ** -0.5)
            if decode:
                q4_ref[pl.ds(h, tm, stride=N_HEADS), :] = q
            else:
                qb_ref[:, cols] = (q * LOG2_E).astype(BF16)

    @pl.when(j == 1)
    def _():
        for h, cols, zh in heads():
            k = _rope(zh, cos_ref, sa_ref, sb_ref)
            k4_ref[pl.ds(h, tm, stride=N_HEADS), :] = k
            if not decode:
                kb_ref[:, cols] = k.astype(BF16)

    @pl.when(j == 2)
    def _():
        for h, cols, zh in heads():
            v4_ref[pl.ds(h, tm, stride=N_HEADS), :] = zh
            if not decode:
                vb_ref[:, cols] = zh.astype(BF16)

    @pl.when(j == 3)
    def _():
        u_ref[...] = jnp.dot(h_scr[...], w_ref[...], preferred_element_type=F32)


def _mixer_in(x, mod, g_pre, w_in_bf, tables, *, tm, seq_tiles, decode):
    rows = x.shape[0]
    nt = rows // tm
    if decode:
        mod_spec = lambda c: pl.BlockSpec((tm // SUBLANES, D_MODEL), lambda i, j: (i, c))
        tab_spec = pl.BlockSpec((tm, LANES), lambda i, j: (i, 0))
    else:
        mod_spec = lambda c: pl.BlockSpec((None, 1, D_MODEL), lambda i, j: (i // seq_tiles, 0, c))
        tab_spec = pl.BlockSpec((tm, LANES), lambda i, j: (i % seq_tiles, 0))
    row_bf = pl.BlockSpec((tm, ATTN_WIDTH), lambda i, j: (i, 0))
    row4 = pl.BlockSpec((tm * N_HEADS, V_DIM), lambda i, j: (i, 0))
    sd_bf = jax.ShapeDtypeStruct((rows, ATTN_WIDTH), BF16)
    sd4 = jax.ShapeDtypeStruct((rows * N_HEADS, V_DIM), F32)
    sd_u = jax.ShapeDtypeStruct((rows, POOL_WIDTH), F32)
    if decode:
        out_specs = [row4, row4, row4, row_bf]
        out_shape = [sd4, sd4, sd4, sd_u]
    else:
        out_specs = [row_bf, row4, row_bf, row4, row_bf, row_bf]
        out_shape = [sd_bf, sd4, sd_bf, sd4, sd_bf, sd_u]
    return pl.pallas_call(
        functools.partial(_mixer_in_kernel, tm=tm, decode=decode),
        grid=(nt, IN_WIDTH // ATTN_WIDTH),
        in_specs=[
            pl.BlockSpec((tm, D_MODEL), lambda i, j: (i, 0)),
            mod_spec(0), mod_spec(1),
            pl.BlockSpec((1, D_MODEL), lambda i, j: (0, 0)),
            pl.BlockSpec((D_MODEL, ATTN_WIDTH), lambda i, j: (0, j)),
            tab_spec, tab_spec, tab_spec,
        ],
        out_specs=out_specs,
        out_shape=out_shape,
        scratch_shapes=[pltpu.VMEM((tm, D_MODEL), BF16)],
        compiler_params=_cparams(("parallel", "arbitrary")),
        name="mixer_in_decode" if decode else "mixer_in_prompt",
    )(x, mod, mod, g_pre.reshape(1, -1), w_in_bf, *tables)


def _rope_tables(pos):
    half = ROT_DIM // 2
    inv = 1.0 / (ROPE_THETA ** (np.arange(0, ROT_DIM, 2, dtype=np.float64) / ROT_DIM))
    ang = np.asarray(pos, np.float64)[:, None] * inv[None, :]
    cos, sin = np.cos(ang), np.sin(ang)
    lane = np.arange(LANES) % HEAD_DIM
    first = (lane < half)[None, :]
    second = ((lane >= half) & (lane < ROT_DIM))[None, :]
    cos_l = cos[:, lane % half]
    sin_l = sin[:, lane % half]
    c = np.where(first | second, cos_l, 1.0)
    sa = np.where(first, -sin_l, 0.0)
    sb = np.where(second, sin_l, 0.0)
    return tuple(jnp.asarray(t, F32) for t in (c, sa, sb))


def _lam(lq1, lk1, lq2, lk2, lam_init):
    a = jnp.sum(lq1[...] * lk1[...], axis=-1, keepdims=True)
    b = jnp.sum(lq2[...] * lk2[...], axis=-1, keepdims=True)
    return jnp.exp(a) - jnp.exp(b) + lam_init


def _flash_kernel(q_ref, k_ref, v_ref, lq1, lk1, lq2, lk2, g_ref, o_ref,
                  vt_scr, s_scr, *, tq, nq, lam_init):
    seq = nq * tq
    vq = 2 * tq
    grp = tq // SUBLANES
    vt_scr[0:V_DIM, :] = v_ref[...].astype(F32).T.astype(BF16)
    vt_scr[V_DIM:, :] = jnp.ones((BF16_ROWS, seq), BF16)
    lam = _lam(lq1, lk1, lq2, lk2, lam_init)
    lane = lax.broadcasted_iota(jnp.int32, (tq, V_DIM), 1)
    kv = lax.broadcasted_iota(jnp.int32, (tq, vq), 0)
    qp = lax.broadcasted_iota(jnp.int32, (tq, vq), 1) & (tq - 1)
    causal = kv <= qp

    for ii in range(nq):
        q = q_ref[ii * tq:(ii + 1) * tq, :].astype(F32)
        qq = jnp.concatenate([jnp.where(lane < HEAD_DIM, q, 0.0),
                              jnp.where(lane >= HEAD_DIM, q, 0.0)], axis=0).astype(BF16)
        sbuf = s_scr.at[ii % 2]
        mx = jnp.full((SUBLANES, vq), NEG_BIG, F32)
        for c in range(ii + 1):
            rows = slice(c * tq, (c + 1) * tq)
            s = lax.dot_general(k_ref[rows, :], qq, (((1,), (1,)), ((), ())),
                                preferred_element_type=F32)
            if c == ii:
                s = jnp.where(causal, s, NEG_BIG)
            sbuf[rows, :] = s
            mx = jnp.maximum(mx, jnp.max(s.reshape(grp, SUBLANES, vq), axis=0))
        m = jnp.max(mx, axis=0, keepdims=True)
        o_t = jnp.zeros((V_DIM + BF16_ROWS, vq), F32)
        for c in range(ii + 1):
            rows = slice(c * tq, (c + 1) * tq)
            p = jnp.exp2(sbuf[rows, :] - m)
            o_t = o_t + jnp.dot(vt_scr[:, rows], p.astype(BF16), preferred_element_type=F32)
        n_t = o_t[:V_DIM] / o_t[V_DIM:V_DIM + 1]
        out = (n_t[:, :tq] - lam * n_t[:, tq:]).T
        o_ref[ii * tq:(ii + 1) * tq, :] = (_rms(out, g_ref[...]) * (1.0 - lam_init)).astype(BF16)


def _flash(q_bf, k_bf, v_bf, lams, g_subln, *, batch, seq, lam_init, tq=256):
    nq = seq // tq
    lam_spec = pl.BlockSpec((1, HEAD_DIM), lambda b, h: (0, 0))
    head_spec = pl.BlockSpec((seq, V_DIM), lambda b, h: (b, h))
    return pl.pallas_call(
        functools.partial(_flash_kernel, tq=tq, nq=nq, lam_init=lam_init),
        grid=(batch, N_HEADS),
        in_specs=[head_spec, head_spec, head_spec, lam_spec, lam_spec, lam_spec, lam_spec,
                  pl.BlockSpec((1, V_DIM), lambda b, h: (0, 0))],
        out_specs=head_spec,
        out_shape=jax.ShapeDtypeStruct((batch * seq, ATTN_WIDTH), BF16),
        scratch_shapes=[pltpu.VMEM((V_DIM + BF16_ROWS, seq), BF16),
                        pltpu.VMEM((2, seq, 2 * tq), F32)],
        compiler_params=_cparams(("parallel", "parallel")),
        name="flash_prompt",
    )(q_bf, k_bf, v_bf, *lams, g_subln.reshape(1, -1))


def _decode_kernel(pt_ref, q_ref, kn_ref, vn_ref, ck_hbm, cv_hbm, lq1, lk1, lq2, lk2, g_ref, o_ref,
                   kbuf, vbuf, ksem, vsem, m_scr, l_scr, acc_scr, *, layer, n_pages, n_groups,
                   n_steps, dec_seq, lam_init):
    n_seqs = 1
    g = pl.program_id(1)
    t = pl.program_id(0) * n_groups + g
    qh = dec_seq * N_HEADS
    page_rows = PAGE_SIZE * N_HEADS

    def page_copies(step, slot):
        cps = []
        for p in range(n_pages):
            page = pt_ref[step * n_pages + p]
            cps.append(pltpu.make_async_copy(ck_hbm.at[layer, page], kbuf.at[slot, p], ksem.at[slot]))
            cps.append(pltpu.make_async_copy(cv_hbm.at[layer, page], vbuf.at[slot, p], vsem.at[slot]))
        return cps

    ahead = RING_SLOTS - 1

    @pl.when(t == 0)
    def _():
        for s in range(min(ahead, n_steps)):
            for cp in page_copies(s, s):
                cp.start()

    @pl.when(t + ahead < n_steps)
    def _():
        for cp in page_copies(t + ahead, lax.rem(t + ahead, RING_SLOTS)):
            cp.start()

    slot = lax.rem(t, RING_SLOTS)
    for cp in page_copies(t, slot):
        cp.wait()
    k_refs = [kbuf.at[slot, p] for p in range(n_pages)]
    v_refs = [vbuf.at[slot, p] for p in range(n_pages)]

    @pl.when(g == 0)
    def _():
        m_scr[...] = jnp.full(m_scr.shape, NEG_BIG, F32)
        l_scr[...] = jnp.zeros(l_scr.shape, F32)
        acc_scr[...] = jnp.zeros(acc_scr.shape, F32)

    sub = lax.broadcasted_iota(jnp.int32, (SUBLANES, LANES), 0)
    lane = lax.broadcasted_iota(jnp.int32, (SUBLANES, LANES), 1)
    diag = (lane & (N_HEADS - 1)) == sub
    lane_q = lax.broadcasted_iota(jnp.int32, (qh, V_DIM), 1)
    lam = _lam(lq1, lk1, lq2, lk2, lam_init)

    def query_cols(b):
        r = q_ref[b * qh:(b + 1) * qh, :]
        return jnp.concatenate([jnp.where(lane_q < HEAD_DIM, r, 0.0),
                                jnp.where(lane_q >= HEAD_DIM, r, 0.0)], axis=0).astype(BF16)

    def update(b, s_list, v_list):
        m_prev = m_scr[b]
        m_new = m_prev
        for s in s_list:
            m_new = jnp.maximum(m_new, jnp.max(s, axis=0))
        alpha = jnp.exp(m_prev - m_new)
        m_eff = jnp.where(diag, m_new, -NEG_BIG)
        l_new = alpha * l_scr[b]
        pv = jnp.zeros((V_DIM, LANES), F32)
        for s, v in zip(s_list, v_list):
            p = jnp.exp(s - m_eff[None])
            l_new = l_new + jnp.sum(p, axis=0)
            pm = p.reshape(v.shape[0], LANES).astype(BF16)
            pv = pv + lax.dot_general(v.astype(BF16), pm, (((0,), (0,)), ((), ())),
                                      preferred_element_type=F32)
        alpha_row = jnp.sum(jnp.where(diag, alpha, 0.0), axis=0, keepdims=True)
        acc_scr[b] = acc_scr[b] * alpha_row + pv
        l_scr[b] = l_new
        m_scr[b] = m_new

    for b in range(n_seqs):
        qt = query_cols(b)
        pages = range(b * n_pages, (b + 1) * n_pages)
        s_list = [
            lax.dot_general(k_refs[p][...].reshape(page_rows, V_DIM).astype(BF16), qt,
                            (((1,), (1,)), ((), ())), preferred_element_type=F32
                            ).reshape(PAGE_SIZE, N_HEADS, LANES) for p in pages]
        update(b, s_list, [v_refs[p][...].reshape(page_rows, V_DIM) for p in pages])

    @pl.when(g == n_groups - 1)
    def _():
        for b in range(n_seqs):
            tok = slice(b * qh, (b + 1) * qh)
            s_new = lax.dot_general(kn_ref[tok, :].astype(BF16), query_cols(b),
                                    (((1,), (1,)), ((), ())), preferred_element_type=F32)
            s_new = s_new.reshape(dec_seq, N_HEADS, LANES)
            pos = lax.broadcasted_iota(jnp.int32, s_new.shape, 0)
            qidx = (lax.broadcasted_iota(jnp.int32, s_new.shape, 2) & (qh - 1)) >> 3
            update(b, [jnp.where(pos <= qidx, s_new, NEG_BIG)], [vn_ref[tok, :]])
            l_row = jnp.sum(jnp.where(diag, l_scr[b], 0.0), axis=0, keepdims=True)
            nt = (acc_scr[b] / l_row).T
            out = nt[:qh] - lam * nt[qh:]
            o_ref[tok, :] = _rms(out, g_ref[...]) * (1.0 - lam_init)


def _decode(q4, k4, v4, cache_k, cache_v, layer, page_table, lams, g_subln, *,
            dec_batch, dec_seq, lam_init, n_pages=8):
    pages_per_seq = page_table.shape[1]
    assert pages_per_seq % n_pages == 0
    n_groups = pages_per_seq // n_pages
    qh = dec_seq * N_HEADS
    tok_spec = pl.BlockSpec((qh, V_DIM), lambda b, g, pt: (b, 0))
    lam_spec = pl.BlockSpec((1, HEAD_DIM), lambda b, g, pt: (0, 0))
    hbm_spec = pl.BlockSpec(memory_space=pl.ANY)
    page_buf = pltpu.VMEM((RING_SLOTS, n_pages, PAGE_SIZE, N_HEADS, V_DIM), F32)
    grid_spec = pltpu.PrefetchScalarGridSpec(
        num_scalar_prefetch=1,
        grid=(dec_batch, n_groups),
        in_specs=[tok_spec, tok_spec, tok_spec, hbm_spec, hbm_spec]
        + [lam_spec] * 4 + [pl.BlockSpec((1, V_DIM), lambda b, g, pt: (0, 0))],
        out_specs=tok_spec,
        scratch_shapes=[page_buf, page_buf,
                        pltpu.SemaphoreType.DMA((RING_SLOTS,)), pltpu.SemaphoreType.DMA((RING_SLOTS,)),
                        pltpu.VMEM((1, SUBLANES, LANES), F32), pltpu.VMEM((1, SUBLANES, LANES), F32),
                        pltpu.VMEM((1, V_DIM, LANES), F32)],
    )
    return pl.pallas_call(
        functools.partial(_decode_kernel, layer=layer, n_pages=n_pages, n_groups=n_groups,
                          n_steps=dec_batch * n_groups, dec_seq=dec_seq, lam_init=lam_init),
        grid_spec=grid_spec,
        out_shape=jax.ShapeDtypeStruct((dec_batch * qh, V_DIM), F32),
        compiler_params=_cparams(("arbitrary", "arbitrary")),
        name="decode_attn",
    )(page_table.reshape(-1), q4, k4, v4, cache_k, cache_v, *lams, g_subln.reshape(1, -1))


def _mixer_out_kernel(x_ref, a_ref, u_ref, hist_ref, gate_ref, gpost_ref, wout_ref, wpool_ref,
                      ps_ref, o_ref, ext_scr, *, nb, t, seq_tiles, decode, pos_base):
    rows = nb * t
    cur = u_ref[...]
    ext_scr[:, HIST_PAD:HIST_PAD + t, :] = cur
    if decode:
        ext_scr[:, HIST_PAD - POOL_HIST:HIST_PAD, :] = hist_ref[...]
        base = pos_base
    else:
        first = (pl.program_id(0) % seq_tiles) == 0
        ext_scr[:, 0:HIST_PAD, :] = jnp.where(first, 0.0, hist_ref[...])
        base = (pl.program_id(0) % seq_tiles) * t
    pos = base + lax.broadcasted_iota(jnp.int32, (1, t, POOL_GROUP_WIDTH), 1)

    ys = []
    for gi, w in enumerate(POOL_WINDOWS):
        cs = slice(gi * POOL_GROUP_WIDTH, (gi + 1) * POOL_GROUP_WIDTH)
        cur_g = cur[:, :, cs]
        win = cur_g
        for k in range(1, w):
            win = win + ext_scr[:, HIST_PAD - k:HIST_PAD - k + t, cs]
        cnt = jnp.minimum(w, pos + 1).astype(F32)
        pooled = (win / cnt - cur_g).reshape(rows, POOL_GROUP_WIDTH)
        ys.append(jnp.dot(pooled.astype(BF16), wpool_ref[gi], preferred_element_type=F32))
    y = jnp.concatenate(ys, axis=1) * ps_ref[...]

    if decode:
        a = jnp.concatenate([a_ref[pl.ds(h, rows, stride=N_HEADS), :] for h in range(N_HEADS)],
                            axis=1).astype(BF16)
    else:
        a = a_ref[...]
    m = (jnp.dot(a, wout_ref[0:ATTN_WIDTH, :], preferred_element_type=F32)
         + jnp.dot(y.astype(BF16), wout_ref[ATTN_WIDTH:, :], preferred_element_type=F32))
    o_ref[...] = x_ref[...] + _mod_rows(gate_ref, rows) * _rms(m, gpost_ref[...])


def _mixer_out(x, a, u3, hist, mod, g_post, w_out_bf, w_pool_bf, pool_scale, *,
               nb, t, seq_tiles, decode, pos_base=0):
    rows_total = x.shape[0]
    tm = nb * t
    nt = rows_total // tm
    if decode:
        a_spec = pl.BlockSpec((tm * N_HEADS, V_DIM), lambda i: (i, 0))
        u_spec = pl.BlockSpec((nb, t, POOL_WIDTH), lambda i: (i, 0, 0))
        hist_spec = pl.BlockSpec((nb, POOL_HIST, POOL_WIDTH), lambda i: (i, 0, 0))
        gate_spec = pl.BlockSpec((nb, D_MODEL), lambda i: (i, 2))
    else:
        a_spec = pl.BlockSpec((tm, ATTN_WIDTH), lambda i: (i, 0))
        u_spec = pl.BlockSpec((1, t, POOL_WIDTH), lambda i: (i // seq_tiles, i % seq_tiles, 0))
        hpt = t // HIST_PAD
        hist_spec = pl.BlockSpec(
            (1, HIST_PAD, POOL_WIDTH),
            lambda i: (i // seq_tiles, jnp.maximum((i % seq_tiles) * hpt - 1, 0), 0))
        gate_spec = pl.BlockSpec((None, 1, D_MODEL), lambda i: (i // seq_tiles, 0, 2))
    return pl.pallas_call(
        functools.partial(_mixer_out_kernel, nb=nb, t=t, seq_tiles=seq_tiles, decode=decode,
                          pos_base=pos_base),
        grid=(nt,),
        in_specs=[
            pl.BlockSpec((tm, D_MODEL), lambda i: (i, 0)),
            a_spec, u_spec, hist_spec, gate_spec,
            pl.BlockSpec((1, D_MODEL), lambda i: (0, 0)),
            pl.BlockSpec((D_MODEL, D_MODEL), lambda i: (0, 0)),
            pl.BlockSpec((len(POOL_WINDOWS), POOL_GROUP_WIDTH, POOL_GROUP_WIDTH), lambda i: (0, 0, 0)),
            pl.BlockSpec((1, POOL_WIDTH), lambda i: (0, 0)),
        ],
        out_specs=pl.BlockSpec((tm, D_MODEL), lambda i: (i, 0)),
        out_shape=jax.ShapeDtypeStruct((rows_total, D_MODEL), F32),
        scratch_shapes=[pltpu.VMEM((nb, HIST_PAD + t, POOL_WIDTH), F32)],
        compiler_params=_cparams(("parallel",)),
        name="mixer_out_decode" if decode else "mixer_out_prompt",
    )(x, a, u3, hist, mod, g_post.reshape(1, -1), w_out_bf, w_pool_bf, pool_scale.reshape(1, -1))


def _ffn_kernel(x_ref, shift_ref, scale_ref, gate_ref, gpre_ref, gpost_ref, w1_ref, w2_ref,
                o_ref, h_scr, acc_scr, *, n_f):
    j = pl.program_id(1)

    @pl.when(j == 0)
    def _():
        tm = x_ref.shape[0]
        h = (_rms(x_ref[...], gpre_ref[...]) * (1.0 + _mod_rows(scale_ref, tm))
             + _mod_rows(shift_ref, tm))
        h_scr[...] = h.astype(BF16)
        acc_scr[...] = jnp.zeros(acc_scr.shape, F32)

    t = jnp.dot(h_scr[...], w1_ref[...], preferred_element_type=F32)
    t = jnp.square(jnp.maximum(t, 0.0))
    acc_scr[...] += jnp.dot(t.astype(BF16), w2_ref[...], preferred_element_type=F32)

    @pl.when(j == n_f - 1)
    def _():
        o_ref[...] = x_ref[...] + _mod_rows(gate_ref, x_ref.shape[0]) * _rms(acc_scr[...], gpost_ref[...])


def _ffn(x, mod, g_pre, g_post, w1_bf, w2_bf, *, tm, tf, seq_tiles, decode):
    rows = x.shape[0]
    nt = rows // tm
    n_f = D_FF // tf
    if decode:
        mod_spec = lambda c: pl.BlockSpec((tm // SUBLANES, D_MODEL), lambda i, j: (i, c))
    else:
        mod_spec = lambda c: pl.BlockSpec((None, 1, D_MODEL), lambda i, j: (i // seq_tiles, 0, c))
    vec_spec = pl.BlockSpec((1, D_MODEL), lambda i, j: (0, 0))
    return pl.pallas_call(
        functools.partial(_ffn_kernel, n_f=n_f),
        grid=(nt, n_f),
        in_specs=[
            pl.BlockSpec((tm, D_MODEL), lambda i, j: (i, 0)),
            mod_spec(3), mod_spec(4), mod_spec(5),
            vec_spec, vec_spec,
            pl.BlockSpec((D_MODEL, tf), lambda i, j: (0, j)),
            pl.BlockSpec((tf, D_MODEL), lambda i, j: (j, 0)),
        ],
        out_specs=pl.BlockSpec((tm, D_MODEL), lambda i, j: (i, 0)),
        out_shape=jax.ShapeDtypeStruct((rows, D_MODEL), F32),
        scratch_shapes=[pltpu.VMEM((tm, D_MODEL), BF16), pltpu.VMEM((tm, D_MODEL), F32)],
        compiler_params=_cparams(("parallel", "arbitrary")),
        name="ffn_decode" if decode else "ffn_prompt",
    )(x, mod, mod, mod, g_pre.reshape(1, -1), g_post.reshape(1, -1), w1_bf, w2_bf)


def kernel(x_prompt, x_sample, cache_k, cache_v, state_pool, page_table, c_prompt, c_sample,
           w_ada, b_ada, g_pre_mix, w_in, lambda_q1, lambda_k1, lambda_q2, lambda_k2, g_subln,
           w_pool, pool_scale, w_out, g_post_mix, g_pre_ffn, w_ff1, w_ff2, g_post_ffn):
    batch, seq, _ = x_prompt.shape
    dec_batch, dec_seq, _ = x_sample.shape
    depth = w_in.shape[0]
    past_len = page_table.shape[1] * PAGE_SIZE
    assert dec_seq == SUBLANES and state_pool.shape[2] == POOL_HIST

    tm_p = 512
    seq_tiles = seq // tm_p
    rows_s = dec_batch * dec_seq
    tables_p = _rope_tables(np.arange(seq))
    tables_s = _rope_tables(past_len + (np.arange(rows_s) % dec_seq))

    n_c = batch + dec_batch
    c_all = jnp.concatenate(
        [c_sample, c_prompt, jnp.zeros((-n_c % SUBLANES, D_MODEL), F32)], axis=0)

    xp = x_prompt.reshape(batch * seq, D_MODEL)
    xs = x_sample.reshape(rows_s, D_MODEL)
    outs = [[] for _ in range(6)]
    for l in range(depth):
        lam_init = 0.8 - 0.6 * math.exp(-0.3 * l)
        lams = [v[l].reshape(1, HEAD_DIM) for v in (lambda_q1, lambda_k1, lambda_q2, lambda_k2)]
        w_in_bf = w_in[l].astype(BF16)
        w_out_bf = w_out[l].astype(BF16)
        w_pool_bf = w_pool[l].astype(BF16)
        w1_bf = w_ff1[l].astype(BF16)
        w2_bf = w_ff2[l].astype(BF16)

        m_all = _ada(c_all, w_ada[l], b_ada[l])
        mod_p = m_all[dec_batch:n_c].reshape(batch, 1, N_MOD * D_MODEL)
        mod_s = m_all

        q_bf, k4, k_bf, v4, v_bf, u = _mixer_in(
            xp, mod_p, g_pre_mix[l], w_in_bf, tables_p, tm=tm_p, seq_tiles=seq_tiles, decode=False)
        a_bf = _flash(q_bf, k_bf, v_bf, lams, g_subln[l], batch=batch, seq=seq, lam_init=lam_init)
        u3 = u.reshape(batch, seq, POOL_WIDTH)
        xp = _mixer_out(xp, a_bf, u3, u3, mod_p, g_post_mix[l], w_out_bf, w_pool_bf, pool_scale[l],
                        nb=1, t=tm_p, seq_tiles=seq_tiles, decode=False)
        xp = _ffn(xp, mod_p, g_pre_ffn[l], g_post_ffn[l], w1_bf, w2_bf,
                  tm=tm_p, tf=1024, seq_tiles=seq_tiles, decode=False)
        outs[0].append(k4.reshape(batch, seq, N_HEADS, V_DIM))
        outs[1].append(v4.reshape(batch, seq, N_HEADS, V_DIM))
        outs[2].append(u3[:, seq - POOL_HIST:])

        q4s, k4s, v4s, us = _mixer_in(
            xs, mod_s, g_pre_mix[l], w_in_bf, tables_s, tm=tm_p, seq_tiles=1, decode=True)
        a4s = _decode(q4s, k4s, v4s, cache_k, cache_v, l, page_table, lams, g_subln[l],
                      dec_batch=dec_batch, dec_seq=dec_seq, lam_init=lam_init)
        us3 = us.reshape(dec_batch, dec_seq, POOL_WIDTH)
        xs = _mixer_out(xs, a4s, us3, state_pool[l], mod_s, g_post_mix[l], w_out_bf, w_pool_bf,
                        pool_scale[l], nb=tm_p // dec_seq, t=dec_seq, seq_tiles=1, decode=True,
                        pos_base=past_len)
        xs = _ffn(xs, mod_s, g_pre_ffn[l], g_post_ffn[l], w1_bf, w2_bf,
                  tm=tm_p, tf=1024, seq_tiles=1, decode=True)
        outs[3].append(k4s.reshape(dec_batch, dec_seq, N_HEADS, V_DIM))
        outs[4].append(v4s.reshape(dec_batch, dec_seq, N_HEADS, V_DIM))
        outs[5].append(jnp.concatenate([state_pool[l][:, dec_seq:], us3], axis=1))

    kp, vp, sp, ks, vs, ss = (jnp.stack(o) for o in outs)
    return (xp.reshape(batch, seq, D_MODEL), xs.reshape(dec_batch, dec_seq, D_MODEL),
            kp, vp, sp, ks, vs, ss)
```

```python
import functools
import math

import jax
import jax.numpy as jnp
import numpy as np
from jax import lax
from jax.experimental import pallas as pl
from jax.experimental.pallas import tpu as pltpu

F32 = jnp.float32
BF16 = jnp.bfloat16

D_MODEL = 2048
N_HEADS = 8
HEAD_DIM = 64
V_DIM = 2 * HEAD_DIM
ATTN_WIDTH = N_HEADS * V_DIM
POOL_WIDTH = 1024
IN_WIDTH = 3 * ATTN_WIDTH + POOL_WIDTH
ROT_DIM = HEAD_DIM // 4
ROPE_THETA = 500000.0
POOL_WINDOWS = (2, 4, 8, 16)
POOL_GROUP_WIDTH = POOL_WIDTH // len(POOL_WINDOWS)
POOL_HIST = max(POOL_WINDOWS) - 1
HIST_PAD = 16
D_FF = 4 * D_MODEL
N_MOD = 6
PAGE_SIZE = 128
EPS = 1e-6
NEG_BIG = -1e30
LOG2_E = math.log2(math.e)

LANES = 128
SUBLANES = 8
BF16_ROWS = 16
VMEM_LIMIT = 60 * 1024 * 1024
RING_SLOTS = 3
ROW_CHUNK = 128


def _cparams(sem):
    return pltpu.CompilerParams(dimension_semantics=sem, vmem_limit_bytes=VMEM_LIMIT)


def _rms(x, g):
    ms = jnp.mean(x * x, axis=-1, keepdims=True)
    return x * lax.rsqrt(ms + EPS) * g


def _mod_rows(ref, r0, nrows, tile_rows):
    n, d = ref.shape
    if n == 1:
        return ref[...]
    per = tile_rows // n
    m = ref[r0 // per:(r0 + nrows) // per, :]
    return jnp.broadcast_to(m[:, None, :], (nrows // per, per, d)).reshape(nrows, d)


def _modulate_into(h_ref, x_ref, g_ref, shift_ref, scale_ref):
    tm = x_ref.shape[0]
    for r in range(0, tm, ROW_CHUNK):
        h = (_rms(x_ref[r:r + ROW_CHUNK, :], g_ref[...])
             * (1.0 + _mod_rows(scale_ref, r, ROW_CHUNK, tm)) + _mod_rows(shift_ref, r, ROW_CHUNK, tm))
        h_ref[r:r + ROW_CHUNK, :] = h.astype(BF16)


def _gated_residual_into(o_ref, x_ref, m_ref, g_ref, gate_ref):
    tm = x_ref.shape[0]
    for r in range(0, tm, ROW_CHUNK):
        rows = slice(r, r + ROW_CHUNK)
        o_ref[rows, :] = x_ref[rows, :] + _mod_rows(gate_ref, r, ROW_CHUNK, tm) * _rms(m_ref[rows, :], g_ref[...])


def _ada_kernel(c_ref, w_ref, b_ref, o_ref):
    c = c_ref[...]
    s = c * (1.0 / (1.0 + jnp.exp(-c)))
    o_ref[...] = jnp.dot(s.astype(BF16), w_ref[...].astype(BF16),
                         preferred_element_type=F32) + b_ref[...]


def _ada(c_all, w_ada, b_ada):
    rows = c_all.shape[0]
    tn = 1024
    return pl.pallas_call(
        _ada_kernel,
        grid=(N_MOD * D_MODEL // tn,),
        in_specs=[
            pl.BlockSpec((rows, D_MODEL), lambda j: (0, 0)),
            pl.BlockSpec((D_MODEL, tn), lambda j: (0, j)),
            pl.BlockSpec((1, tn), lambda j: (0, j)),
        ],
        out_specs=pl.BlockSpec((rows, tn), lambda j: (0, j)),
        out_shape=jax.ShapeDtypeStruct((rows, N_MOD * D_MODEL), F32),
        compiler_params=_cparams(("arbitrary",)),
        name="ada",
    )(c_all, w_ada, b_ada.reshape(1, -1))


def _rope(z, cos_ref, sa_ref, sb_ref):
    return (z * cos_ref[...]
            + pltpu.roll(z, LANES - ROT_DIM // 2, 1) * sa_ref[...]
            + pltpu.roll(z, ROT_DIM // 2, 1) * sb_ref[...])


def _mixer_in_kernel(x_ref, shift_ref, scale_ref, g_ref, w_ref, cos_ref, sa_ref, sb_ref,
                     *refs, tm, decode):
    j = pl.program_id(1)
    if decode:
        q4_ref, k4_ref, v4_ref, u_ref, wbf_ref, h_scr, wc_scr = refs
        wc_scr[...] = w_ref[...].astype(BF16)

        @pl.when(pl.program_id(0) == 0)
        def _():
            wbf_ref[...] = wc_scr[...]
        w_ref = wc_scr
    else:
        qb_ref, k4_ref, kb_ref, v4_ref, vb_ref, u_ref, h_scr = refs

    @pl.when(j == 0)
    def _():
        _modulate_into(h_scr, x_ref, g_ref, shift_ref, scale_ref)

    def heads():
        for hp in range(N_HEADS // 2):
            z2 = jnp.dot(h_scr[...], w_ref[:, 2 * hp * V_DIM:2 * (hp + 1) * V_DIM],
                         preferred_element_type=F32)
            for h in (2 * hp, 2 * hp + 1):
                yield h, slice(h * V_DIM, (h + 1) * V_DIM), z2[:, (h % 2) * V_DIM:(h % 2 + 1) * V_DIM]

    @pl.when(j == 0)
    def _():
        for h, cols, zh in heads():
            q = _rope(zh, cos_ref, sa_ref, sb_ref) * (HEAD_DIM ** -0.5)
            if decode:
                q4_ref[pl.ds(h, tm, stride=N_HEADS), :] = q
            else:
                qb_ref[:, cols] = (q * LOG2_E).astype(BF16)

    @pl.when(j == 1)
    def _():
        for h, cols, zh in heads():
            k = _rope(zh, cos_ref, sa_ref, sb_ref)
            k4_ref[pl.ds(h, tm, stride=N_HEADS), :] = k
            if not decode:
                kb_ref[:, cols] = k.astype(BF16)

    @pl.when(j == 2)
    def _():
        for h, cols, zh in heads():
            v4_ref[pl.ds(h, tm, stride=N_HEADS), :] = zh
            if not decode:
                vb_ref[:, cols] = zh.astype(BF16)

    @pl.when(j == 3)
    def _():
        u_ref[...] = jnp.dot(h_scr[...], w_ref[...], preferred_element_type=F32)


def _mixer_in(x, mod, g_pre, w_in, tables, *, tm, seq_tiles, decode):
    rows = x.shape[0]
    nt = rows // tm
    if decode:
        mod_spec = lambda c: pl.BlockSpec((tm // SUBLANES, D_MODEL), lambda i, j: (i, c))
        tab_spec = pl.BlockSpec((tm, LANES), lambda i, j: (i, 0))
    else:
        mod_spec = lambda c: pl.BlockSpec((None, 1, D_MODEL), lambda i, j: (i // seq_tiles, 0, c))
        tab_spec = pl.BlockSpec((tm, LANES), lambda i, j: (i % seq_tiles, 0))
    row_bf = pl.BlockSpec((tm, ATTN_WIDTH), lambda i, j: (i, 0))
    row4 = pl.BlockSpec((tm * N_HEADS, V_DIM), lambda i, j: (i, 0))
    sd_bf = jax.ShapeDtypeStruct((rows, ATTN_WIDTH), BF16)
    sd4 = jax.ShapeDtypeStruct((rows * N_HEADS, V_DIM), F32)
    sd_u = jax.ShapeDtypeStruct((rows, POOL_WIDTH), F32)
    n_sec = IN_WIDTH // ATTN_WIDTH
    scratch = [pltpu.VMEM((tm, D_MODEL), BF16)]
    if decode:
        w_out_spec = pl.BlockSpec((D_MODEL, ATTN_WIDTH),
                                  lambda i, j: (0, jnp.where(i == 0, j, n_sec - 1)))
        out_specs = [row4, row4, row4, row_bf, w_out_spec]
        out_shape = [sd4, sd4, sd4, sd_u, jax.ShapeDtypeStruct((D_MODEL, IN_WIDTH), BF16)]
        scratch.append(pltpu.VMEM((D_MODEL, ATTN_WIDTH), BF16))
    else:
        out_specs = [row_bf, row4, row_bf, row4, row_bf, row_bf]
        out_shape = [sd_bf, sd4, sd_bf, sd4, sd_bf, sd_u]
    return pl.pallas_call(
        functools.partial(_mixer_in_kernel, tm=tm, decode=decode),
        grid=(nt, n_sec),
        in_specs=[
            pl.BlockSpec((tm, D_MODEL), lambda i, j: (i, 0)),
            mod_spec(0), mod_spec(1),
            pl.BlockSpec((1, D_MODEL), lambda i, j: (0, 0)),
            pl.BlockSpec((D_MODEL, ATTN_WIDTH), lambda i, j: (0, j)),
            tab_spec, tab_spec, tab_spec,
        ],
        out_specs=out_specs,
        out_shape=out_shape,
        scratch_shapes=scratch,
        compiler_params=_cparams(("arbitrary", "arbitrary")),
        name="mixer_in_decode" if decode else "mixer_in_prompt",
    )(x, mod, mod, g_pre.reshape(1, -1), w_in, *tables)


def _rope_tables(pos):
    half = ROT_DIM // 2
    inv = 1.0 / (ROPE_THETA ** (np.arange(0, ROT_DIM, 2, dtype=np.float64) / ROT_DIM))
    ang = np.asarray(pos, np.float64)[:, None] * inv[None, :]
    cos, sin = np.cos(ang), np.sin(ang)
    lane = np.arange(LANES) % HEAD_DIM
    first = (lane < half)[None, :]
    second = ((lane >= half) & (lane < ROT_DIM))[None, :]
    cos_l = cos[:, lane % half]
    sin_l = sin[:, lane % half]
    c = np.where(first | second, cos_l, 1.0)
    sa = np.where(first, -sin_l, 0.0)
    sb = np.where(second, sin_l, 0.0)
    return tuple(jnp.asarray(t, F32) for t in (c, sa, sb))


def _lam(lq1, lk1, lq2, lk2, lam_init):
    a = jnp.sum(lq1[...] * lk1[...], axis=-1, keepdims=True)
    b = jnp.sum(lq2[...] * lk2[...], axis=-1, keepdims=True)
    return jnp.exp(a) - jnp.exp(b) + lam_init


def _flash_kernel(q_ref, k_ref, v_ref, lq1, lk1, lq2, lk2, g_ref, o_ref,
                  vt_scr, s_scr, *, tq, nq, lam_init):
    seq = nq * tq
    vq = 2 * tq
    grp = tq // SUBLANES
    vt_scr[0:V_DIM, :] = v_ref[...].astype(F32).T.astype(BF16)
    vt_scr[V_DIM:, :] = jnp.ones((BF16_ROWS, seq), BF16)
    lam = _lam(lq1, lk1, lq2, lk2, lam_init)
    lane = lax.broadcasted_iota(jnp.int32, (tq, V_DIM), 1)
    kv = lax.broadcasted_iota(jnp.int32, (tq, vq), 0)
    qp = lax.broadcasted_iota(jnp.int32, (tq, vq), 1) & (tq - 1)
    causal = kv <= qp

    for ii in range(nq):
        q = q_ref[ii * tq:(ii + 1) * tq, :].astype(F32)
        qq = jnp.concatenate([jnp.where(lane < HEAD_DIM, q, 0.0),
                              jnp.where(lane >= HEAD_DIM, q, 0.0)], axis=0).astype(BF16)
        sbuf = s_scr.at[ii % 2]
        mx = jnp.full((SUBLANES, vq), NEG_BIG, F32)
        for c in range(ii + 1):
            rows = slice(c * tq, (c + 1) * tq)
            s = lax.dot_general(k_ref[rows, :], qq, (((1,), (1,)), ((), ())),
                                preferred_element_type=F32)
            if c == ii:
                s = jnp.where(causal, s, NEG_BIG)
            sbuf[rows, :] = s
            mx = jnp.maximum(mx, jnp.max(s.reshape(grp, SUBLANES, vq), axis=0))
        m = jnp.max(mx, axis=0, keepdims=True)
        o_t = jnp.zeros((V_DIM + BF16_ROWS, vq), F32)
        for c in range(ii + 1):
            rows = slice(c * tq, (c + 1) * tq)
            p = jnp.exp2(sbuf[rows, :] - m)
            o_t = o_t + jnp.dot(vt_scr[:, rows], p.astype(BF16), preferred_element_type=F32)
        n_t = o_t[:V_DIM] / o_t[V_DIM:V_DIM + 1]
        out = (n_t[:, :tq] - lam * n_t[:, tq:]).T
        o_ref[ii * tq:(ii + 1) * tq, :] = (_rms(out, g_ref[...]) * (1.0 - lam_init)).astype(BF16)


def _flash(q_bf, k_bf, v_bf, lams, g_subln, *, batch, seq, lam_init, tq=256):
    nq = seq // tq
    lam_spec = pl.BlockSpec((1, HEAD_DIM), lambda b, h: (0, 0))
    head_spec = pl.BlockSpec((seq, V_DIM), lambda b, h: (b, h))
    return pl.pallas_call(
        functools.partial(_flash_kernel, tq=tq, nq=nq, lam_init=lam_init),
        grid=(batch, N_HEADS),
        in_specs=[head_spec, head_spec, head_spec, lam_spec, lam_spec, lam_spec, lam_spec,
                  pl.BlockSpec((1, V_DIM), lambda b, h: (0, 0))],
        out_specs=head_spec,
        out_shape=jax.ShapeDtypeStruct((batch * seq, ATTN_WIDTH), BF16),
        scratch_shapes=[pltpu.VMEM((V_DIM + BF16_ROWS, seq), BF16),
                        pltpu.VMEM((2, seq, 2 * tq), F32)],
        compiler_params=_cparams(("parallel", "parallel")),
        name="flash_prompt",
    )(q_bf, k_bf, v_bf, *lams, g_subln.reshape(1, -1))


def _decode_kernel(pt_ref, q_ref, kn_ref, vn_ref, ck_hbm, cv_hbm, lq1, lk1, lq2, lk2, g_ref, o_ref,
                   kbuf, vbuf, ksem, vsem, m_scr, l_scr, acc_scr, *, layer, n_pages, n_groups,
                   n_steps, dec_seq, lam_init):
    n_seqs = 1
    g = pl.program_id(1)
    t = pl.program_id(0) * n_groups + g
    qh = dec_seq * N_HEADS
    page_rows = PAGE_SIZE * N_HEADS

    def page_copies(step, slot):
        cps = []
        for p in range(n_pages):
            page = pt_ref[step * n_pages + p]
            cps.append(pltpu.make_async_copy(ck_hbm.at[layer, page], kbuf.at[slot, p], ksem.at[slot]))
            cps.append(pltpu.make_async_copy(cv_hbm.at[layer, page], vbuf.at[slot, p], vsem.at[slot]))
        return cps

    ahead = RING_SLOTS - 1

    @pl.when(t == 0)
    def _():
        for s in range(min(ahead, n_steps)):
            for cp in page_copies(s, s):
                cp.start()

    @pl.when(t + ahead < n_steps)
    def _():
        for cp in page_copies(t + ahead, lax.rem(t + ahead, RING_SLOTS)):
            cp.start()

    slot = lax.rem(t, RING_SLOTS)
    for cp in page_copies(t, slot):
        cp.wait()
    k_refs = [kbuf.at[slot, p] for p in range(n_pages)]
    v_refs = [vbuf.at[slot, p] for p in range(n_pages)]

    @pl.when(g == 0)
    def _():
        m_scr[...] = jnp.full(m_scr.shape, NEG_BIG, F32)
        l_scr[...] = jnp.zeros(l_scr.shape, F32)
        acc_scr[...] = jnp.zeros(acc_scr.shape, F32)

    sub = lax.broadcasted_iota(jnp.int32, (SUBLANES, LANES), 0)
    lane = lax.broadcasted_iota(jnp.int32, (SUBLANES, LANES), 1)
    diag = (lane & (N_HEADS - 1)) == sub
    lane_q = lax.broadcasted_iota(jnp.int32, (qh, V_DIM), 1)
    lam = _lam(lq1, lk1, lq2, lk2, lam_init)

    def query_cols(b):
        r = q_ref[b * qh:(b + 1) * qh, :]
        return jnp.concatenate([jnp.where(lane_q < HEAD_DIM, r, 0.0),
                                jnp.where(lane_q >= HEAD_DIM, r, 0.0)], axis=0).astype(BF16)

    def update(b, s_list, v_list):
        m_prev = m_scr[b]
        m_new = m_prev
        for s in s_list:
            m_new = jnp.maximum(m_new, jnp.max(s, axis=0))
        alpha = jnp.exp(m_prev - m_new)
        m_eff = jnp.where(diag, m_new, -NEG_BIG)
        l_new = alpha * l_scr[b]
        pv = jnp.zeros((V_DIM, LANES), F32)
        for s, v in zip(s_list, v_list):
            p = jnp.exp(s - m_eff[None])
            l_new = l_new + jnp.sum(p, axis=0)
            pm = p.reshape(v.shape[0], LANES).astype(BF16)
            pv = pv + lax.dot_general(v.astype(BF16), pm, (((0,), (0,)), ((), ())),
                                      preferred_element_type=F32)
        alpha_row = jnp.sum(jnp.where(diag, alpha, 0.0), axis=0, keepdims=True)
        acc_scr[b] = acc_scr[b] * alpha_row + pv
        l_scr[b] = l_new
        m_scr[b] = m_new

    for b in range(n_seqs):
        qt = query_cols(b)
        pages = range(b * n_pages, (b + 1) * n_pages)
        s_list = [
            lax.dot_general(k_refs[p][...].reshape(page_rows, V_DIM).astype(BF16), qt,
                            (((1,), (1,)), ((), ())), preferred_element_type=F32
                            ).reshape(PAGE_SIZE, N_HEADS, LANES) for p in pages]
        update(b, s_list, [v_refs[p][...].reshape(page_rows, V_DIM) for p in pages])

    @pl.when(g == n_groups - 1)
    def _():
        for b in range(n_seqs):
            tok = slice(b * qh, (b + 1) * qh)
            s_new = lax.dot_general(kn_ref[tok, :].astype(BF16), query_cols(b),
                                    (((1,), (1,)), ((), ())), preferred_element_type=F32)
            s_new = s_new.reshape(dec_seq, N_HEADS, LANES)
            pos = lax.broadcasted_iota(jnp.int32, s_new.shape, 0)
            qidx = (lax.broadcasted_iota(jnp.int32, s_new.shape, 2) & (qh - 1)) >> 3
            update(b, [jnp.where(pos <= qidx, s_new, NEG_BIG)], [vn_ref[tok, :]])
            l_row = jnp.sum(jnp.where(diag, l_scr[b], 0.0), axis=0, keepdims=True)
            nt = (acc_scr[b] / l_row).T
            out = nt[:qh] - lam * nt[qh:]
            o_ref[tok, :] = _rms(out, g_ref[...]) * (1.0 - lam_init)


def _decode(q4, k4, v4, cache_k, cache_v, layer, page_table, lams, g_subln, *,
            dec_batch, dec_seq, lam_init, n_pages=8):
    pages_per_seq = page_table.shape[1]
    assert pages_per_seq % n_pages == 0
    n_groups = pages_per_seq // n_pages
    qh = dec_seq * N_HEADS
    tok_spec = pl.BlockSpec((qh, V_DIM), lambda b, g, pt: (b, 0))
    lam_spec = pl.BlockSpec((1, HEAD_DIM), lambda b, g, pt: (0, 0))
    hbm_spec = pl.BlockSpec(memory_space=pl.ANY)
    page_buf = pltpu.VMEM((RING_SLOTS, n_pages, PAGE_SIZE, N_HEADS, V_DIM), F32)
    grid_spec = pltpu.PrefetchScalarGridSpec(
        num_scalar_prefetch=1,
        grid=(dec_batch, n_groups),
        in_specs=[tok_spec, tok_spec, tok_spec, hbm_spec, hbm_spec]
        + [lam_spec] * 4 + [pl.BlockSpec((1, V_DIM), lambda b, g, pt: (0, 0))],
        out_specs=tok_spec,
        scratch_shapes=[page_buf, page_buf,
                        pltpu.SemaphoreType.DMA((RING_SLOTS,)), pltpu.SemaphoreType.DMA((RING_SLOTS,)),
                        pltpu.VMEM((1, SUBLANES, LANES), F32), pltpu.VMEM((1, SUBLANES, LANES), F32),
                        pltpu.VMEM((1, V_DIM, LANES), F32)],
    )
    return pl.pallas_call(
        functools.partial(_decode_kernel, layer=layer, n_pages=n_pages, n_groups=n_groups,
                          n_steps=dec_batch * n_groups, dec_seq=dec_seq, lam_init=lam_init),
        grid_spec=grid_spec,
        out_shape=jax.ShapeDtypeStruct((dec_batch * qh, V_DIM), F32),
        compiler_params=_cparams(("arbitrary", "arbitrary")),
        name="decode_attn",
    )(page_table.reshape(-1), q4, k4, v4, cache_k, cache_v, *lams, g_subln.reshape(1, -1))


def _mixer_out_kernel(x_ref, a_ref, u_ref, hist_ref, gate_ref, gpost_ref, wout_ref, wpool_ref,
                      ps_ref, o_ref, ext_scr, *, nb, t, seq_tiles, decode, pos_base):
    rows = nb * t
    cur = u_ref[...]
    ext_scr[:, HIST_PAD:HIST_PAD + t, :] = cur
    if decode:
        ext_scr[:, HIST_PAD - POOL_HIST:HIST_PAD, :] = hist_ref[...]
        base = pos_base
    else:
        first = (pl.program_id(0) % seq_tiles) == 0
        ext_scr[:, 0:HIST_PAD, :] = jnp.where(first, 0.0, hist_ref[...])
        base = (pl.program_id(0) % seq_tiles) * t
    pos = base + lax.broadcasted_iota(jnp.int32, (1, t, POOL_GROUP_WIDTH), 1)

    ys = []
    for gi, w in enumerate(POOL_WINDOWS):
        cs = slice(gi * POOL_GROUP_WIDTH, (gi + 1) * POOL_GROUP_WIDTH)
        cur_g = cur[:, :, cs]
        win = cur_g
        for k in range(1, w):
            win = win + ext_scr[:, HIST_PAD - k:HIST_PAD - k + t, cs]
        cnt = jnp.minimum(w, pos + 1).astype(F32)
        pooled = (win / cnt - cur_g).reshape(rows, POOL_GROUP_WIDTH)
        ys.append(jnp.dot(pooled.astype(BF16), wpool_ref[gi], preferred_element_type=F32))
    y = jnp.concatenate(ys, axis=1) * ps_ref[...]

    if decode:
        a = jnp.concatenate([a_ref[pl.ds(h, rows, stride=N_HEADS), :] for h in range(N_HEADS)],
                            axis=1).astype(BF16)
    else:
        a = a_ref[...]
    m = (jnp.dot(a, wout_ref[0:ATTN_WIDTH, :], preferred_element_type=F32)
         + jnp.dot(y.astype(BF16), wout_ref[ATTN_WIDTH:, :], preferred_element_type=F32))
    o_ref[...] = x_ref[...] + _mod_rows(gate_ref, 0, rows, rows) * _rms(m, gpost_ref[...])


def _mixer_out(x, a, u3, hist, mod, g_post, w_out_bf, w_pool_bf, pool_scale, *,
               nb, t, seq_tiles, decode, pos_base=0):
    rows_total = x.shape[0]
    tm = nb * t
    nt = rows_total // tm
    if decode:
        a_spec = pl.BlockSpec((tm * N_HEADS, V_DIM), lambda i: (i, 0))
        u_spec = pl.BlockSpec((nb, t, POOL_WIDTH), lambda i: (i, 0, 0))
        hist_spec = pl.BlockSpec((nb, POOL_HIST, POOL_WIDTH), lambda i: (i, 0, 0))
        gate_spec = pl.BlockSpec((nb, D_MODEL), lambda i: (i, 2))
    else:
        a_spec = pl.BlockSpec((tm, ATTN_WIDTH), lambda i: (i, 0))
        u_spec = pl.BlockSpec((1, t, POOL_WIDTH), lambda i: (i // seq_tiles, i % seq_tiles, 0))
        hpt = t // HIST_PAD
        hist_spec = pl.BlockSpec(
            (1, HIST_PAD, POOL_WIDTH),
            lambda i: (i // seq_tiles, jnp.maximum((i % seq_tiles) * hpt - 1, 0), 0))
        gate_spec = pl.BlockSpec((None, 1, D_MODEL), lambda i: (i // seq_tiles, 0, 2))
    return pl.pallas_call(
        functools.partial(_mixer_out_kernel, nb=nb, t=t, seq_tiles=seq_tiles, decode=decode,
                          pos_base=pos_base),
        grid=(nt,),
        in_specs=[
            pl.BlockSpec((tm, D_MODEL), lambda i: (i, 0)),
            a_spec, u_spec, hist_spec, gate_spec,
            pl.BlockSpec((1, D_MODEL), lambda i: (0, 0)),
            pl.BlockSpec((D_MODEL, D_MODEL), lambda i: (0, 0)),
            pl.BlockSpec((len(POOL_WINDOWS), POOL_GROUP_WIDTH, POOL_GROUP_WIDTH), lambda i: (0, 0, 0)),
            pl.BlockSpec((1, POOL_WIDTH), lambda i: (0, 0)),
        ],
        out_specs=pl.BlockSpec((tm, D_MODEL), lambda i: (i, 0)),
        out_shape=jax.ShapeDtypeStruct((rows_total, D_MODEL), F32),
        scratch_shapes=[pltpu.VMEM((nb, HIST_PAD + t, POOL_WIDTH), F32)],
        compiler_params=_cparams(("parallel",)),
        name="mixer_out_decode" if decode else "mixer_out_prompt",
    )(x, a, u3, hist, mod, g_post.reshape(1, -1), w_out_bf, w_pool_bf, pool_scale.reshape(1, -1))


def _ffn_kernel(x_ref, shift_ref, scale_ref, gate_ref, gpre_ref, gpost_ref, w1_ref, w2_ref,
                *refs, n_f, decode):
    j = pl.program_id(1)
    if decode:
        o_ref, w1bf_ref, w2bf_ref, h_scr, acc_scr, w1c_scr, w2c_scr = refs
        w1c_scr[...] = w1_ref[...].astype(BF16)
        w2c_scr[...] = w2_ref[...].astype(BF16)

        @pl.when(pl.program_id(0) == 0)
        def _():
            w1bf_ref[...] = w1c_scr[...]
            w2bf_ref[...] = w2c_scr[...]
        w1_ref, w2_ref = w1c_scr, w2c_scr
    else:
        o_ref, h_scr, acc_scr = refs

    @pl.when(j == 0)
    def _():
        _modulate_into(h_scr, x_ref, gpre_ref, shift_ref, scale_ref)
        acc_scr[...] = jnp.zeros(acc_scr.shape, F32)

    t = jnp.dot(h_scr[...], w1_ref[...], preferred_element_type=F32)
    t = jnp.square(jnp.maximum(t, 0.0))
    acc_scr[...] += jnp.dot(t.astype(BF16), w2_ref[...], preferred_element_type=F32)

    @pl.when(j == n_f - 1)
    def _():
        _gated_residual_into(o_ref, x_ref, acc_scr, gpost_ref, gate_ref)


def _ffn(x, mod, g_pre, g_post, w1, w2, *, tm, tf, seq_tiles, decode):
    rows = x.shape[0]
    nt = rows // tm
    n_f = D_FF // tf
    out_specs = pl.BlockSpec((tm, D_MODEL), lambda i, j: (i, 0))
    out_shape = jax.ShapeDtypeStruct((rows, D_MODEL), F32)
    scratch = [pltpu.VMEM((tm, D_MODEL), BF16), pltpu.VMEM((tm, D_MODEL), F32)]
    if decode:
        mod_spec = lambda c: pl.BlockSpec((tm // SUBLANES, D_MODEL), lambda i, j: (i, c))
        last = lambda i, j: jnp.where(i == 0, j, n_f - 1)
        out_specs = [out_specs, pl.BlockSpec((D_MODEL, tf), lambda i, j: (0, last(i, j))),
                     pl.BlockSpec((tf, D_MODEL), lambda i, j: (last(i, j), 0))]
        out_shape = [out_shape, jax.ShapeDtypeStruct((D_MODEL, D_FF), BF16),
                     jax.ShapeDtypeStruct((D_FF, D_MODEL), BF16)]
        scratch += [pltpu.VMEM((D_MODEL, tf), BF16), pltpu.VMEM((tf, D_MODEL), BF16)]
    else:
        mod_spec = lambda c: pl.BlockSpec((None, 1, D_MODEL), lambda i, j: (i // seq_tiles, 0, c))
    vec_spec = pl.BlockSpec((1, D_MODEL), lambda i, j: (0, 0))
    return pl.pallas_call(
        functools.partial(_ffn_kernel, n_f=n_f, decode=decode),
        grid=(nt, n_f),
        in_specs=[
            pl.BlockSpec((tm, D_MODEL), lambda i, j: (i, 0)),
            mod_spec(3), mod_spec(4), mod_spec(5),
            vec_spec, vec_spec,
            pl.BlockSpec((D_MODEL, tf), lambda i, j: (0, j)),
            pl.BlockSpec((tf, D_MODEL), lambda i, j: (j, 0)),
        ],
        out_specs=out_specs,
        out_shape=out_shape,
        scratch_shapes=scratch,
        compiler_params=_cparams(("arbitrary", "arbitrary")),
        name="ffn_decode" if decode else "ffn_prompt",
    )(x, mod, mod, mod, g_pre.reshape(1, -1), g_post.reshape(1, -1), w1, w2)


def kernel(x_prompt, x_sample, cache_k, cache_v, state_pool, page_table, c_prompt, c_sample,
           w_ada, b_ada, g_pre_mix, w_in, lambda_q1, lambda_k1, lambda_q2, lambda_k2, g_subln,
           w_pool, pool_scale, w_out, g_post_mix, g_pre_ffn, w_ff1, w_ff2, g_post_ffn):
    batch, seq, _ = x_prompt.shape
    dec_batch, dec_seq, _ = x_sample.shape
    depth = w_in.shape[0]
    past_len = page_table.shape[1] * PAGE_SIZE
    assert dec_seq == SUBLANES and state_pool.shape[2] == POOL_HIST

    tm_p = 512
    seq_tiles = seq // tm_p
    rows_s = dec_batch * dec_seq
    tables_p = _rope_tables(np.arange(seq))
    tables_s = _rope_tables(past_len + (np.arange(rows_s) % dec_seq))

    n_c = batch + dec_batch
    c_all = jnp.concatenate(
        [c_sample, c_prompt, jnp.zeros((-n_c % SUBLANES, D_MODEL), F32)], axis=0)

    xp = x_prompt.reshape(batch * seq, D_MODEL)
    xs = x_sample.reshape(rows_s, D_MODEL)
    outs = [[] for _ in range(6)]
    for l in range(depth):
        lam_init = 0.8 - 0.6 * math.exp(-0.3 * l)
        lams = [v[l].reshape(1, HEAD_DIM) for v in (lambda_q1, lambda_k1, lambda_q2, lambda_k2)]
        w_out_bf = w_out[l].astype(BF16)
        w_pool_bf = w_pool[l].astype(BF16)

        m_all = _ada(c_all, w_ada[l], b_ada[l])
        mod_p = m_all[dec_batch:n_c].reshape(batch, 1, N_MOD * D_MODEL)
        mod_s = m_all

        q4s, k4s, v4s, us, w_in_bf = _mixer_in(
            xs, mod_s, g_pre_mix[l], w_in[l], tables_s, tm=tm_p, seq_tiles=1, decode=True)
        a4s = _decode(q4s, k4s, v4s, cache_k, cache_v, l, page_table, lams, g_subln[l],
                      dec_batch=dec_batch, dec_seq=dec_seq, lam_init=lam_init)
        us3 = us.reshape(dec_batch, dec_seq, POOL_WIDTH)
        xs = _mixer_out(xs, a4s, us3, state_pool[l], mod_s, g_post_mix[l], w_out_bf, w_pool_bf,
                        pool_scale[l], nb=tm_p // dec_seq, t=dec_seq, seq_tiles=1, decode=True,
                        pos_base=past_len)
        xs, w1_bf, w2_bf = _ffn(xs, mod_s, g_pre_ffn[l], g_post_ffn[l], w_ff1[l], w_ff2[l],
                                tm=tm_p, tf=512, seq_tiles=1, decode=True)
        outs[3].append(k4s.reshape(dec_batch, dec_seq, N_HEADS, V_DIM))
        outs[4].append(v4s.reshape(dec_batch, dec_seq, N_HEADS, V_DIM))
        outs[5].append(jnp.concatenate([state_pool[l][:, dec_seq:], us3], axis=1))

        q_bf, k4, k_bf, v4, v_bf, u = _mixer_in(
            xp, mod_p, g_pre_mix[l], w_in_bf, tables_p, tm=tm_p, seq_tiles=seq_tiles, decode=False)
        a_bf = _flash(q_bf, k_bf, v_bf, lams, g_subln[l], batch=batch, seq=seq, lam_init=lam_init)
        u3 = u.reshape(batch, seq, POOL_WIDTH)
        xp = _mixer_out(xp, a_bf, u3, u3, mod_p, g_post_mix[l], w_out_bf, w_pool_bf, pool_scale[l],
                        nb=1, t=tm_p, seq_tiles=seq_tiles, decode=False)
        xp = _ffn(xp, mod_p, g_pre_ffn[l], g_post_ffn[l], w1_bf, w2_bf,
                  tm=tm_p, tf=1024, seq_tiles=seq_tiles, decode=False)
        outs[0].append(k4.reshape(batch, seq, N_HEADS, V_DIM))
        outs[1].append(v4.reshape(batch, seq, N_HEADS, V_DIM))
        outs[2].append(u3[:, seq - POOL_HIST:])

    kp, vp, sp, ks, vs, ss = (jnp.stack(o) for o in outs)
    return (xp.reshape(batch, seq, D_MODEL), xs.reshape(dec_batch, dec_seq, D_MODEL),
            kp, vp, sp, ks, vs, ss)
```

```python
import functools
import math

import jax
import jax.numpy as jnp
import numpy as np
from jax import lax
from jax.experimental import pallas as pl
from jax.experimental.pallas import tpu as pltpu

F32 = jnp.float32
BF16 = jnp.bfloat16

D_MODEL = 2048
N_HEADS = 8
HEAD_DIM = 64
V_DIM = 2 * HEAD_DIM
ATTN_WIDTH = N_HEADS * V_DIM
POOL_WIDTH = 1024
IN_WIDTH = 3 * ATTN_WIDTH + POOL_WIDTH
ROT_DIM = HEAD_DIM // 4
ROPE_THETA = 500000.0
POOL_WINDOWS = (2, 4, 8, 16)
POOL_GROUP_WIDTH = POOL_WIDTH // len(POOL_WINDOWS)
POOL_HIST = max(POOL_WINDOWS) - 1
HIST_PAD = 16
D_FF = 4 * D_MODEL
N_MOD = 6
PAGE_SIZE = 128
EPS = 1e-6
NEG_BIG = -1e30
LOG2_E = math.log2(math.e)

LANES = 128
SUBLANES = 8
BF16_ROWS = 16
VMEM_LIMIT = 60 * 1024 * 1024
RING_SLOTS = 3
ROW_CHUNK = 128


def _cparams(sem):
    return pltpu.CompilerParams(dimension_semantics=sem, vmem_limit_bytes=VMEM_LIMIT)


def _rms(x, g):
    ms = jnp.mean(x * x, axis=-1, keepdims=True)
    return x * lax.rsqrt(ms + EPS) * g


def _mod_rows(ref, r0, nrows, tile_rows):
    n, d = ref.shape
    if n == 1:
        return ref[...]
    per = tile_rows // n
    m = ref[r0 // per:(r0 + nrows) // per, :]
    return jnp.broadcast_to(m[:, None, :], (nrows // per, per, d)).reshape(nrows, d)


def _modulate_into(h_ref, x_ref, g_ref, shift_ref, scale_ref):
    tm = x_ref.shape[0]
    for r in range(0, tm, ROW_CHUNK):
        h = (_rms(x_ref[r:r + ROW_CHUNK, :], g_ref[...])
             * (1.0 + _mod_rows(scale_ref, r, ROW_CHUNK, tm)) + _mod_rows(shift_ref, r, ROW_CHUNK, tm))
        h_ref[r:r + ROW_CHUNK, :] = h.astype(BF16)


def _gated_residual_into(o_ref, x_ref, m_ref, g_ref, gate_ref):
    tm = x_ref.shape[0]
    for r in range(0, tm, ROW_CHUNK):
        rows = slice(r, r + ROW_CHUNK)
        o_ref[rows, :] = x_ref[rows, :] + _mod_rows(gate_ref, r, ROW_CHUNK, tm) * _rms(m_ref[rows, :], g_ref[...])


def _ada_kernel(c_ref, w_ref, b_ref, o_ref):
    c = c_ref[...]
    s = c * (1.0 / (1.0 + jnp.exp(-c)))
    o_ref[...] = jnp.dot(s.astype(BF16), w_ref[...].astype(BF16),
                         preferred_element_type=F32) + b_ref[...]


def _ada(c_all, w_ada, b_ada):
    rows = c_all.shape[0]
    tn = 1024
    return pl.pallas_call(
        _ada_kernel,
        grid=(N_MOD * D_MODEL // tn,),
        in_specs=[
            pl.BlockSpec((rows, D_MODEL), lambda j: (0, 0)),
            pl.BlockSpec((D_MODEL, tn), lambda j: (0, j)),
            pl.BlockSpec((1, tn), lambda j: (0, j)),
        ],
        out_specs=pl.BlockSpec((rows, tn), lambda j: (0, j)),
        out_shape=jax.ShapeDtypeStruct((rows, N_MOD * D_MODEL), F32),
        compiler_params=_cparams(("arbitrary",)),
        name="ada",
    )(c_all, w_ada, b_ada.reshape(1, -1))


def _rope(z, cos_ref, sa_ref, sb_ref):
    return (z * cos_ref[...]
            + pltpu.roll(z, LANES - ROT_DIM // 2, 1) * sa_ref[...]
            + pltpu.roll(z, ROT_DIM // 2, 1) * sb_ref[...])


def _mixer_in_kernel(x_ref, shift_ref, scale_ref, g_ref, w_ref, cos_ref, sa_ref, sb_ref,
                     *refs, tm, decode):
    j = pl.program_id(1)
    if decode:
        q4_ref, k4_ref, v4_ref, u_ref, wbf_ref, h_scr, wc_scr = refs
        wc_scr[...] = w_ref[...].astype(BF16)

        @pl.when(pl.program_id(0) == 0)
        def _():
            wbf_ref[...] = wc_scr[...]
        w_ref = wc_scr
    else:
        qb_ref, k4_ref, kb_ref, v4_ref, vb_ref, u_ref, h_scr = refs

    @pl.when(j == 0)
    def _():
        _modulate_into(h_scr, x_ref, g_ref, shift_ref, scale_ref)

    def heads():
        for hp in range(N_HEADS // 2):
            z2 = jnp.dot(h_scr[...], w_ref[:, 2 * hp * V_DIM:2 * (hp + 1) * V_DIM],
                         preferred_element_type=F32)
            for h in (2 * hp, 2 * hp + 1):
                yield h, slice(h * V_DIM, (h + 1) * V_DIM), z2[:, (h % 2) * V_DIM:(h % 2 + 1) * V_DIM]

    @pl.when(j == 0)
    def _():
        for h, cols, zh in heads():
            q = _rope(zh, cos_ref, sa_ref, sb_ref) * (HEAD_DIM ** -0.5)
            if decode:
                q4_ref[pl.ds(h, tm, stride=N_HEADS), :] = q
            else:
                qb_ref[:, cols] = (q * LOG2_E).astype(BF16)

    @pl.when(j == 1)
    def _():
        for h, cols, zh in heads():
            k = _rope(zh, cos_ref, sa_ref, sb_ref)
            k4_ref[pl.ds(h, tm, stride=N_HEADS), :] = k
            if not decode:
                kb_ref[:, cols] = k.astype(BF16)

    @pl.when(j == 2)
    def _():
        for h, cols, zh in heads():
            v4_ref[pl.ds(h, tm, stride=N_HEADS), :] = zh
            if not decode:
                vb_ref[:, cols] = zh.astype(BF16)

    @pl.when(j == 3)
    def _():
        u_ref[...] = jnp.dot(h_scr[...], w_ref[...], preferred_element_type=F32)


def _mixer_in(x, mod, g_pre, w_in, tables, *, tm, seq_tiles, decode):
    rows = x.shape[0]
    nt = rows // tm
    if decode:
        mod_spec = lambda c: pl.BlockSpec((tm // SUBLANES, D_MODEL), lambda i, j: (i, c))
        tab_spec = pl.BlockSpec((tm, LANES), lambda i, j: (i, 0))
    else:
        mod_spec = lambda c: pl.BlockSpec((None, 1, D_MODEL), lambda i, j: (i // seq_tiles, 0, c))
        tab_spec = pl.BlockSpec((tm, LANES), lambda i, j: (i % seq_tiles, 0))
    row_bf = pl.BlockSpec((tm, ATTN_WIDTH), lambda i, j: (i, 0))
    row4 = pl.BlockSpec((tm * N_HEADS, V_DIM), lambda i, j: (i, 0))
    sd_bf = jax.ShapeDtypeStruct((rows, ATTN_WIDTH), BF16)
    sd4 = jax.ShapeDtypeStruct((rows * N_HEADS, V_DIM), F32)
    sd_u = jax.ShapeDtypeStruct((rows, POOL_WIDTH), F32)
    n_sec = IN_WIDTH // ATTN_WIDTH
    scratch = [pltpu.VMEM((tm, D_MODEL), BF16)]
    if decode:
        w_out_spec = pl.BlockSpec((D_MODEL, ATTN_WIDTH),
                                  lambda i, j: (0, jnp.where(i == 0, j, n_sec - 1)))
        out_specs = [row4, row4, row4, row_bf, w_out_spec]
        out_shape = [sd4, sd4, sd4, sd_u, jax.ShapeDtypeStruct((D_MODEL, IN_WIDTH), BF16)]
        scratch.append(pltpu.VMEM((D_MODEL, ATTN_WIDTH), BF16))
    else:
        out_specs = [row_bf, row4, row_bf, row4, row_bf, row_bf]
        out_shape = [sd_bf, sd4, sd_bf, sd4, sd_bf, sd_u]
    return pl.pallas_call(
        functools.partial(_mixer_in_kernel, tm=tm, decode=decode),
        grid=(nt, n_sec),
        in_specs=[
            pl.BlockSpec((tm, D_MODEL), lambda i, j: (i, 0)),
            mod_spec(0), mod_spec(1),
            pl.BlockSpec((1, D_MODEL), lambda i, j: (0, 0)),
            pl.BlockSpec((D_MODEL, ATTN_WIDTH), lambda i, j: (0, j)),
            tab_spec, tab_spec, tab_spec,
        ],
        out_specs=out_specs,
        out_shape=out_shape,
        scratch_shapes=scratch,
        compiler_params=_cparams(("arbitrary", "arbitrary")),
        name="mixer_in_decode" if decode else "mixer_in_prompt",
    )(x, mod, mod, g_pre.reshape(1, -1), w_in, *tables)


def _rope_tables(pos):
    half = ROT_DIM // 2
    inv = 1.0 / (ROPE_THETA ** (np.arange(0, ROT_DIM, 2, dtype=np.float64) / ROT_DIM))
    ang = np.asarray(pos, np.float64)[:, None] * inv[None, :]
    cos, sin = np.cos(ang), np.sin(ang)
    lane = np.arange(LANES) % HEAD_DIM
    first = (lane < half)[None, :]
    second = ((lane >= half) & (lane < ROT_DIM))[None, :]
    cos_l = cos[:, lane % half]
    sin_l = sin[:, lane % half]
    c = np.where(first | second, cos_l, 1.0)
    sa = np.where(first, -sin_l, 0.0)
    sb = np.where(second, sin_l, 0.0)
    return tuple(jnp.asarray(t, F32) for t in (c, sa, sb))


def _lam(lq1, lk1, lq2, lk2, lam_init):
    a = jnp.sum(lq1[...] * lk1[...], axis=-1, keepdims=True)
    b = jnp.sum(lq2[...] * lk2[...], axis=-1, keepdims=True)
    return jnp.exp(a) - jnp.exp(b) + lam_init


def _flash_kernel(q_ref, k_ref, v_ref, lq1, lk1, lq2, lk2, g_ref, o_ref,
                  vt_scr, s_scr, *, tq, nq, lam_init):
    seq = nq * tq
    vq = 2 * tq
    grp = tq // SUBLANES
    vt_scr[0:V_DIM, :] = v_ref[...].astype(F32).T.astype(BF16)
    vt_scr[V_DIM:, :] = jnp.ones((BF16_ROWS, seq), BF16)
    lam = _lam(lq1, lk1, lq2, lk2, lam_init)
    lane = lax.broadcasted_iota(jnp.int32, (tq, V_DIM), 1)
    kv = lax.broadcasted_iota(jnp.int32, (tq, vq), 0)
    qp = lax.broadcasted_iota(jnp.int32, (tq, vq), 1) & (tq - 1)
    causal = kv <= qp

    def stacked_queries(ii):
        q = q_ref[ii * tq:(ii + 1) * tq, :].astype(F32)
        return jnp.concatenate([jnp.where(lane < HEAD_DIM, q, 0.0),
                                jnp.where(lane >= HEAD_DIM, q, 0.0)], axis=0).astype(BF16)

    def score_chunk(ii, c, qq, mx):
        rows = slice(c * tq, (c + 1) * tq)
        s = lax.dot_general(k_ref[rows, :], qq, (((1,), (1,)), ((), ())),
                            preferred_element_type=F32)
        if c == ii:
            s = jnp.where(causal, s, NEG_BIG)
        s_scr[ii % 2, rows, :] = s
        return jnp.maximum(mx, jnp.max(s.reshape(grp, SUBLANES, vq), axis=0))

    def value_chunk(ii, c, m, o_t):
        rows = slice(c * tq, (c + 1) * tq)
        p = jnp.exp2(s_scr[ii % 2, rows, :] - m)
        return o_t + jnp.dot(vt_scr[:, rows], p.astype(BF16), preferred_element_type=F32)

    mx_init = jnp.full((SUBLANES, vq), NEG_BIG, F32)
    mx = score_chunk(0, 0, stacked_queries(0), mx_init)
    for ii in range(nq):
        m = jnp.max(mx, axis=0, keepdims=True)
        o_t = jnp.zeros((V_DIM + BF16_ROWS, vq), F32)
        nxt = ii + 1 < nq
        if nxt:
            qq_next = stacked_queries(ii + 1)
            mx = mx_init
        for c in range(ii + 2):
            if nxt:
                mx = score_chunk(ii + 1, c, qq_next, mx)
            if c <= ii:
                o_t = value_chunk(ii, c, m, o_t)
        n_t = o_t[:V_DIM] / o_t[V_DIM:V_DIM + 1]
        out = (n_t[:, :tq] - lam * n_t[:, tq:]).T
        o_ref[ii * tq:(ii + 1) * tq, :] = (_rms(out, g_ref[...]) * (1.0 - lam_init)).astype(BF16)


def _flash(q_bf, k_bf, v_bf, lams, g_subln, *, batch, seq, lam_init, tq=256):
    nq = seq // tq
    lam_spec = pl.BlockSpec((1, HEAD_DIM), lambda b, h: (0, 0))
    head_spec = pl.BlockSpec((seq, V_DIM), lambda b, h: (b, h))
    return pl.pallas_call(
        functools.partial(_flash_kernel, tq=tq, nq=nq, lam_init=lam_init),
        grid=(batch, N_HEADS),
        in_specs=[head_spec, head_spec, head_spec, lam_spec, lam_spec, lam_spec, lam_spec,
                  pl.BlockSpec((1, V_DIM), lambda b, h: (0, 0))],
        out_specs=head_spec,
        out_shape=jax.ShapeDtypeStruct((batch * seq, ATTN_WIDTH), BF16),
        scratch_shapes=[pltpu.VMEM((V_DIM + BF16_ROWS, seq), BF16),
                        pltpu.VMEM((2, seq, 2 * tq), F32)],
        compiler_params=_cparams(("parallel", "parallel")),
        name="flash_prompt",
    )(q_bf, k_bf, v_bf, *lams, g_subln.reshape(1, -1))


def _decode_kernel(pt_ref, q_ref, kn_ref, vn_ref, ck_hbm, cv_hbm, lq1, lk1, lq2, lk2, g_ref, o_ref,
                   kbuf, vbuf, ksem, vsem, m_scr, l_scr, acc_scr, *, layer, n_pages, n_groups,
                   n_steps, dec_seq, lam_init):
    n_seqs = 1
    g = pl.program_id(1)
    t = pl.program_id(0) * n_groups + g
    qh = dec_seq * N_HEADS
    page_rows = PAGE_SIZE * N_HEADS

    def page_copies(step, slot):
        cps = []
        for p in range(n_pages):
            page = pt_ref[step * n_pages + p]
            cps.append(pltpu.make_async_copy(ck_hbm.at[layer, page], kbuf.at[slot, p], ksem.at[slot]))
            cps.append(pltpu.make_async_copy(cv_hbm.at[layer, page], vbuf.at[slot, p], vsem.at[slot]))
        return cps

    ahead = RING_SLOTS - 1

    @pl.when(t == 0)
    def _():
        for s in range(min(ahead, n_steps)):
            for cp in page_copies(s, s):
                cp.start()

    @pl.when(t + ahead < n_steps)
    def _():
        for cp in page_copies(t + ahead, lax.rem(t + ahead, RING_SLOTS)):
            cp.start()

    slot = lax.rem(t, RING_SLOTS)
    for cp in page_copies(t, slot):
        cp.wait()
    k_refs = [kbuf.at[slot, p] for p in range(n_pages)]
    v_refs = [vbuf.at[slot, p] for p in range(n_pages)]

    @pl.when(g == 0)
    def _():
        m_scr[...] = jnp.full(m_scr.shape, NEG_BIG, F32)
        l_scr[...] = jnp.zeros(l_scr.shape, F32)
        acc_scr[...] = jnp.zeros(acc_scr.shape, F32)

    sub = lax.broadcasted_iota(jnp.int32, (SUBLANES, LANES), 0)
    lane = lax.broadcasted_iota(jnp.int32, (SUBLANES, LANES), 1)
    diag = (lane & (N_HEADS - 1)) == sub
    lane_q = lax.broadcasted_iota(jnp.int32, (qh, V_DIM), 1)
    lam = _lam(lq1, lk1, lq2, lk2, lam_init)

    def query_cols(b):
        r = q_ref[b * qh:(b + 1) * qh, :]
        return jnp.concatenate([jnp.where(lane_q < HEAD_DIM, r, 0.0),
                                jnp.where(lane_q >= HEAD_DIM, r, 0.0)], axis=0).astype(BF16)

    def update(b, s_list, v_list):
        m_prev = m_scr[b]
        m_new = m_prev
        for s in s_list:
            m_new = jnp.maximum(m_new, jnp.max(s, axis=0))
        alpha = jnp.exp(m_prev - m_new)
        m_eff = jnp.where(diag, m_new, -NEG_BIG)
        l_new = alpha * l_scr[b]
        pv = jnp.zeros((V_DIM, LANES), F32)
        for s, v in zip(s_list, v_list):
            p = jnp.exp(s - m_eff[None])
            l_new = l_new + jnp.sum(p, axis=0)
            pm = p.reshape(v.shape[0], LANES).astype(BF16)
            pv = pv + lax.dot_general(v.astype(BF16), pm, (((0,), (0,)), ((), ())),
                                      preferred_element_type=F32)
        alpha_row = jnp.sum(jnp.where(diag, alpha, 0.0), axis=0, keepdims=True)
        acc_scr[b] = acc_scr[b] * alpha_row + pv
        l_scr[b] = l_new
        m_scr[b] = m_new

    for b in range(n_seqs):
        qt = query_cols(b)
        pages = range(b * n_pages, (b + 1) * n_pages)
        s_list = [
            lax.dot_general(k_refs[p][...].reshape(page_rows, V_DIM).astype(BF16), qt,
                            (((1,), (1,)), ((), ())), preferred_element_type=F32
                            ).reshape(PAGE_SIZE, N_HEADS, LANES) for p in pages]
        update(b, s_list, [v_refs[p][...].reshape(page_rows, V_DIM) for p in pages])

    @pl.when(g == n_groups - 1)
    def _():
        for b in range(n_seqs):
            tok = slice(b * qh, (b + 1) * qh)
            s_new = lax.dot_general(kn_ref[tok, :].astype(BF16), query_cols(b),
                                    (((1,), (1,)), ((), ())), preferred_element_type=F32)
            s_new = s_new.reshape(dec_seq, N_HEADS, LANES)
            pos = lax.broadcasted_iota(jnp.int32, s_new.shape, 0)
            qidx = (lax.broadcasted_iota(jnp.int32, s_new.shape, 2) & (qh - 1)) >> 3
            update(b, [jnp.where(pos <= qidx, s_new, NEG_BIG)], [vn_ref[tok, :]])
            l_row = jnp.sum(jnp.where(diag, l_scr[b], 0.0), axis=0, keepdims=True)
            nt = (acc_scr[b] / l_row).T
            out = nt[:qh] - lam * nt[qh:]
            o_ref[tok, :] = _rms(out, g_ref[...]) * (1.0 - lam_init)


def _decode(q4, k4, v4, cache_k, cache_v, layer, page_table, lams, g_subln, *,
            dec_batch, dec_seq, lam_init, n_pages=8):
    pages_per_seq = page_table.shape[1]
    assert pages_per_seq % n_pages == 0
    n_groups = pages_per_seq // n_pages
    qh = dec_seq * N_HEADS
    tok_spec = pl.BlockSpec((qh, V_DIM), lambda b, g, pt: (b, 0))
    lam_spec = pl.BlockSpec((1, HEAD_DIM), lambda b, g, pt: (0, 0))
    hbm_spec = pl.BlockSpec(memory_space=pl.ANY)
    page_buf = pltpu.VMEM((RING_SLOTS, n_pages, PAGE_SIZE, N_HEADS, V_DIM), F32)
    grid_spec = pltpu.PrefetchScalarGridSpec(
        num_scalar_prefetch=1,
        grid=(dec_batch, n_groups),
        in_specs=[tok_spec, tok_spec, tok_spec, hbm_spec, hbm_spec]
        + [lam_spec] * 4 + [pl.BlockSpec((1, V_DIM), lambda b, g, pt: (0, 0))],
        out_specs=tok_spec,
        scratch_shapes=[page_buf, page_buf,
                        pltpu.SemaphoreType.DMA((RING_SLOTS,)), pltpu.SemaphoreType.DMA((RING_SLOTS,)),
                        pltpu.VMEM((1, SUBLANES, LANES), F32), pltpu.VMEM((1, SUBLANES, LANES), F32),
                        pltpu.VMEM((1, V_DIM, LANES), F32)],
    )
    return pl.pallas_call(
        functools.partial(_decode_kernel, layer=layer, n_pages=n_pages, n_groups=n_groups,
                          n_steps=dec_batch * n_groups, dec_seq=dec_seq, lam_init=lam_init),
        grid_spec=grid_spec,
        out_shape=jax.ShapeDtypeStruct((dec_batch * qh, V_DIM), F32),
        compiler_params=_cparams(("arbitrary", "arbitrary")),
        name="decode_attn",
    )(page_table.reshape(-1), q4, k4, v4, cache_k, cache_v, *lams, g_subln.reshape(1, -1))


def _mixer_out_kernel(x_ref, a_ref, u_ref, hist_ref, gate_ref, gpost_ref, wout_ref, wpool_ref,
                      ps_ref, o_ref, ext_scr, *, nb, t, seq_tiles, decode, pos_base):
    rows = nb * t
    cur = u_ref[...]
    ext_scr[:, HIST_PAD:HIST_PAD + t, :] = cur
    if decode:
        ext_scr[:, HIST_PAD - POOL_HIST:HIST_PAD, :] = hist_ref[...]
        base = pos_base
    else:
        first = (pl.program_id(0) % seq_tiles) == 0
        ext_scr[:, 0:HIST_PAD, :] = jnp.where(first, 0.0, hist_ref[...])
        base = (pl.program_id(0) % seq_tiles) * t
    pos = base + lax.broadcasted_iota(jnp.int32, (1, t, POOL_GROUP_WIDTH), 1)

    ys = []
    for gi, w in enumerate(POOL_WINDOWS):
        cs = slice(gi * POOL_GROUP_WIDTH, (gi + 1) * POOL_GROUP_WIDTH)
        cur_g = cur[:, :, cs]
        win = cur_g
        for k in range(1, w):
            win = win + ext_scr[:, HIST_PAD - k:HIST_PAD - k + t, cs]
        cnt = jnp.minimum(w, pos + 1).astype(F32)
        pooled = (win / cnt - cur_g).reshape(rows, POOL_GROUP_WIDTH)
        ys.append(jnp.dot(pooled.astype(BF16), wpool_ref[gi], preferred_element_type=F32))
    y = jnp.concatenate(ys, axis=1) * ps_ref[...]

    if decode:
        a = jnp.concatenate([a_ref[pl.ds(h, rows, stride=N_HEADS), :] for h in range(N_HEADS)],
                            axis=1).astype(BF16)
    else:
        a = a_ref[...]
    m = (jnp.dot(a, wout_ref[0:ATTN_WIDTH, :], preferred_element_type=F32)
         + jnp.dot(y.astype(BF16), wout_ref[ATTN_WIDTH:, :], preferred_element_type=F32))
    o_ref[...] = x_ref[...] + _mod_rows(gate_ref, 0, rows, rows) * _rms(m, gpost_ref[...])


def _mixer_out(x, a, u3, hist, mod, g_post, w_out_bf, w_pool_bf, pool_scale, *,
               nb, t, seq_tiles, decode, pos_base=0):
    rows_total = x.shape[0]
    tm = nb * t
    nt = rows_total // tm
    if decode:
        a_spec = pl.BlockSpec((tm * N_HEADS, V_DIM), lambda i: (i, 0))
        u_spec = pl.BlockSpec((nb, t, POOL_WIDTH), lambda i: (i, 0, 0))
        hist_spec = pl.BlockSpec((nb, POOL_HIST, POOL_WIDTH), lambda i: (i, 0, 0))
        gate_spec = pl.BlockSpec((nb, D_MODEL), lambda i: (i, 2))
    else:
        a_spec = pl.BlockSpec((tm, ATTN_WIDTH), lambda i: (i, 0))
        u_spec = pl.BlockSpec((1, t, POOL_WIDTH), lambda i: (i // seq_tiles, i % seq_tiles, 0))
        hpt = t // HIST_PAD
        hist_spec = pl.BlockSpec(
            (1, HIST_PAD, POOL_WIDTH),
            lambda i: (i // seq_tiles, jnp.maximum((i % seq_tiles) * hpt - 1, 0), 0))
        gate_spec = pl.BlockSpec((None, 1, D_MODEL), lambda i: (i // seq_tiles, 0, 2))
    return pl.pallas_call(
        functools.partial(_mixer_out_kernel, nb=nb, t=t, seq_tiles=seq_tiles, decode=decode,
                          pos_base=pos_base),
        grid=(nt,),
        in_specs=[
            pl.BlockSpec((tm, D_MODEL), lambda i: (i, 0)),
            a_spec, u_spec, hist_spec, gate_spec,
            pl.BlockSpec((1, D_MODEL), lambda i: (0, 0)),
            pl.BlockSpec((D_MODEL, D_MODEL), lambda i: (0, 0)),
            pl.BlockSpec((len(POOL_WINDOWS), POOL_GROUP_WIDTH, POOL_GROUP_WIDTH), lambda i: (0, 0, 0)),
            pl.BlockSpec((1, POOL_WIDTH), lambda i: (0, 0)),
        ],
        out_specs=pl.BlockSpec((tm, D_MODEL), lambda i: (i, 0)),
        out_shape=jax.ShapeDtypeStruct((rows_total, D_MODEL), F32),
        scratch_shapes=[pltpu.VMEM((nb, HIST_PAD + t, POOL_WIDTH), F32)],
        compiler_params=_cparams(("parallel",)),
        name="mixer_out_decode" if decode else "mixer_out_prompt",
    )(x, a, u3, hist, mod, g_post.reshape(1, -1), w_out_bf, w_pool_bf, pool_scale.reshape(1, -1))


def _ffn_kernel(x_ref, shift_ref, scale_ref, gate_ref, gpre_ref, gpost_ref, w1_ref, w2_ref,
                *refs, n_f, decode):
    j = pl.program_id(1)
    if decode:
        o_ref, w1bf_ref, w2bf_ref, h_scr, acc_scr = refs
        w1bf_ref[...] = w1_ref[...].astype(BF16)
        w2bf_ref[...] = w2_ref[...].astype(BF16)
        w1_ref, w2_ref = w1bf_ref, w2bf_ref
    else:
        o_ref, h_scr, acc_scr = refs

    @pl.when(j == 0)
    def _():
        _modulate_into(h_scr, x_ref, gpre_ref, shift_ref, scale_ref)
        acc_scr[...] = jnp.zeros(acc_scr.shape, F32)

    t = jnp.dot(h_scr[...], w1_ref[...], preferred_element_type=F32)
    t = jnp.square(jnp.maximum(t, 0.0))
    acc_scr[...] += jnp.dot(t.astype(BF16), w2_ref[...], preferred_element_type=F32)

    @pl.when(j == n_f - 1)
    def _():
        _gated_residual_into(o_ref, x_ref, acc_scr, gpost_ref, gate_ref)


def _ffn(x, mod, g_pre, g_post, w1, w2, *, tm, tf, seq_tiles, decode):
    rows = x.shape[0]
    nt = rows // tm
    n_f = D_FF // tf
    row_mode = dict(pipeline_mode=pl.Buffered(1)) if nt == 1 else {}
    x_spec = pl.BlockSpec((tm, D_MODEL), lambda i, j: (i, 0), **row_mode)
    out_specs = pl.BlockSpec((tm, D_MODEL), lambda i, j: (i, 0), **row_mode)
    out_shape = jax.ShapeDtypeStruct((rows, D_MODEL), F32)
    scratch = [pltpu.VMEM((tm, D_MODEL), BF16), pltpu.VMEM((tm, D_MODEL), F32)]
    if decode:
        assert nt == 1
        mod_spec = lambda c: pl.BlockSpec((tm // SUBLANES, D_MODEL), lambda i, j: (i, c))
        out_specs = [out_specs, pl.BlockSpec((D_MODEL, tf), lambda i, j: (0, j)),
                     pl.BlockSpec((tf, D_MODEL), lambda i, j: (j, 0))]
        out_shape = [out_shape, jax.ShapeDtypeStruct((D_MODEL, D_FF), BF16),
                     jax.ShapeDtypeStruct((D_FF, D_MODEL), BF16)]
    else:
        mod_spec = lambda c: pl.BlockSpec((None, 1, D_MODEL), lambda i, j: (i // seq_tiles, 0, c))
    vec_spec = pl.BlockSpec((1, D_MODEL), lambda i, j: (0, 0))
    return pl.pallas_call(
        functools.partial(_ffn_kernel, n_f=n_f, decode=decode),
        grid=(nt, n_f),
        in_specs=[
            x_spec,
            mod_spec(3), mod_spec(4), mod_spec(5),
            vec_spec, vec_spec,
            pl.BlockSpec((D_MODEL, tf), lambda i, j: (0, j)),
            pl.BlockSpec((tf, D_MODEL), lambda i, j: (j, 0)),
        ],
        out_specs=out_specs,
        out_shape=out_shape,
        scratch_shapes=scratch,
        compiler_params=_cparams(("arbitrary", "arbitrary")),
        name="ffn_decode" if decode else "ffn_prompt",
    )(x, mod, mod, mod, g_pre.reshape(1, -1), g_post.reshape(1, -1), w1, w2)


def kernel(x_prompt, x_sample, cache_k, cache_v, state_pool, page_table, c_prompt, c_sample,
           w_ada, b_ada, g_pre_mix, w_in, lambda_q1, lambda_k1, lambda_q2, lambda_k2, g_subln,
           w_pool, pool_scale, w_out, g_post_mix, g_pre_ffn, w_ff1, w_ff2, g_post_ffn):
    batch, seq, _ = x_prompt.shape
    dec_batch, dec_seq, _ = x_sample.shape
    depth = w_in.shape[0]
    past_len = page_table.shape[1] * PAGE_SIZE
    assert dec_seq == SUBLANES and state_pool.shape[2] == POOL_HIST

    tm_p = 512
    seq_tiles = seq // tm_p
    rows_s = dec_batch * dec_seq
    tables_p = _rope_tables(np.arange(seq))
    tables_s = _rope_tables(past_len + (np.arange(rows_s) % dec_seq))

    n_c = batch + dec_batch
    c_all = jnp.concatenate(
        [c_sample, c_prompt, jnp.zeros((-n_c % SUBLANES, D_MODEL), F32)], axis=0)

    xp = x_prompt.reshape(batch * seq, D_MODEL)
    xs = x_sample.reshape(rows_s, D_MODEL)
    outs = [[] for _ in range(6)]
    for l in range(depth):
        lam_init = 0.8 - 0.6 * math.exp(-0.3 * l)
        lams = [v[l].reshape(1, HEAD_DIM) for v in (lambda_q1, lambda_k1, lambda_q2, lambda_k2)]
        w_out_bf = w_out[l].astype(BF16)
        w_pool_bf = w_pool[l].astype(BF16)

        m_all = _ada(c_all, w_ada[l], b_ada[l])
        mod_p = m_all[dec_batch:n_c].reshape(batch, 1, N_MOD * D_MODEL)
        mod_s = m_all

        q4s, k4s, v4s, us, w_in_bf = _mixer_in(
            xs, mod_s, g_pre_mix[l], w_in[l], tables_s, tm=tm_p, seq_tiles=1, decode=True)
        a4s = _decode(q4s, k4s, v4s, cache_k, cache_v, l, page_table, lams, g_subln[l],
                      dec_batch=dec_batch, dec_seq=dec_seq, lam_init=lam_init)
        us3 = us.reshape(dec_batch, dec_seq, POOL_WIDTH)
        xs = _mixer_out(xs, a4s, us3, state_pool[l], mod_s, g_post_mix[l], w_out_bf, w_pool_bf,
                        pool_scale[l], nb=tm_p // dec_seq, t=dec_seq, seq_tiles=1, decode=True,
                        pos_base=past_len)
        xs, w1_bf, w2_bf = _ffn(xs, mod_s, g_pre_ffn[l], g_post_ffn[l], w_ff1[l], w_ff2[l],
                                tm=rows_s, tf=512, seq_tiles=1, decode=True)
        outs[3].append(k4s.reshape(dec_batch, dec_seq, N_HEADS, V_DIM))
        outs[4].append(v4s.reshape(dec_batch, dec_seq, N_HEADS, V_DIM))
        outs[5].append(jnp.concatenate([state_pool[l][:, dec_seq:], us3], axis=1))

        q_bf, k4, k_bf, v4, v_bf, u = _mixer_in(
            xp, mod_p, g_pre_mix[l], w_in_bf, tables_p, tm=tm_p, seq_tiles=seq_tiles, decode=False)
        a_bf = _flash(q_bf, k_bf, v_bf, lams, g_subln[l], batch=batch, seq=seq, lam_init=lam_init)
        u3 = u.reshape(batch, seq, POOL_WIDTH)
        xp = _mixer_out(xp, a_bf, u3, u3, mod_p, g_post_mix[l], w_out_bf, w_pool_bf, pool_scale[l],
                        nb=1, t=tm_p, seq_tiles=seq_tiles, decode=False)
        xp = _ffn(xp, mod_p, g_pre_ffn[l], g_post_ffn[l], w1_bf, w2_bf,
                  tm=tm_p, tf=1024, seq_tiles=seq_tiles, decode=False)
        outs[0].append(k4.reshape(batch, seq, N_HEADS, V_DIM))
        outs[1].append(v4.reshape(batch, seq, N_HEADS, V_DIM))
        outs[2].append(u3[:, seq - POOL_HIST:])

    kp, vp, sp, ks, vs, ss = (jnp.stack(o) for o in outs)
    return (xp.reshape(batch, seq, D_MODEL), xs.reshape(dec_batch, dec_seq, D_MODEL),
            kp, vp, sp, ks, vs, ss)
```

```python
import functools
import math

import jax
import jax.numpy as jnp
import numpy as np
from jax import lax
from jax.experimental import pallas as pl
from jax.experimental.pallas import tpu as pltpu

F32 = jnp.float32
BF16 = jnp.bfloat16

D_MODEL = 2048
N_HEADS = 8
HEAD_DIM = 64
V_DIM = 2 * HEAD_DIM
ATTN_WIDTH = N_HEADS * V_DIM
POOL_WIDTH = 1024
IN_WIDTH = 3 * ATTN_WIDTH + POOL_WIDTH
ROT_DIM = HEAD_DIM // 4
ROPE_THETA = 500000.0
POOL_WINDOWS = (2, 4, 8, 16)
POOL_GROUP_WIDTH = POOL_WIDTH // len(POOL_WINDOWS)
POOL_HIST = max(POOL_WINDOWS) - 1
HIST_PAD = 16
D_FF = 4 * D_MODEL
N_MOD = 6
PAGE_SIZE = 128
EPS = 1e-6
NEG_BIG = -1e30
LOG2_E = math.log2(math.e)

LANES = 128
SUBLANES = 8
BF16_ROWS = 16
VMEM_LIMIT = 60 * 1024 * 1024
RING_SLOTS = 3
ROW_CHUNK = 16
POOL_ROWS = 64


def _cparams(sem):
    return pltpu.CompilerParams(dimension_semantics=sem, vmem_limit_bytes=VMEM_LIMIT)


def _rms(x, g):
    ms = jnp.mean(x * x, axis=-1, keepdims=True)
    return x * lax.rsqrt(ms + EPS) * g


def _mod_rows(ref, r0, nrows, tile_rows):
    n, d = ref.shape
    if n == 1:
        return ref[...]
    per = tile_rows // n
    m = ref[r0 // per:(r0 + nrows) // per, :]
    return jnp.broadcast_to(m[:, None, :], (nrows // per, per, d)).reshape(nrows, d)


def _modulate_into(h_ref, x_ref, g_ref, shift_ref, scale_ref):
    tm = x_ref.shape[0]
    for r in range(0, tm, ROW_CHUNK):
        h = (_rms(x_ref[r:r + ROW_CHUNK, :], g_ref[...])
             * (1.0 + _mod_rows(scale_ref, r, ROW_CHUNK, tm)) + _mod_rows(shift_ref, r, ROW_CHUNK, tm))
        h_ref[r:r + ROW_CHUNK, :] = h.astype(BF16)


def _gated_residual_into(o_ref, x_ref, m_ref, g_ref, gate_ref):
    tm = x_ref.shape[0]
    for r in range(0, tm, ROW_CHUNK):
        rows = slice(r, r + ROW_CHUNK)
        o_ref[rows, :] = x_ref[rows, :] + _mod_rows(gate_ref, r, ROW_CHUNK, tm) * _rms(m_ref[rows, :], g_ref[...])


def _ada_kernel(c_ref, w_ref, b_ref, o_ref):
    c = c_ref[...]
    s = c * (1.0 / (1.0 + jnp.exp(-c)))
    o_ref[...] = jnp.dot(s.astype(BF16), w_ref[...].astype(BF16),
                         preferred_element_type=F32) + b_ref[...]


def _ada(c_all, w_ada, b_ada):
    rows = c_all.shape[0]
    tn = 1024
    return pl.pallas_call(
        _ada_kernel,
        grid=(N_MOD * D_MODEL // tn,),
        in_specs=[
            pl.BlockSpec((rows, D_MODEL), lambda j: (0, 0)),
            pl.BlockSpec((D_MODEL, tn), lambda j: (0, j)),
            pl.BlockSpec((1, tn), lambda j: (0, j)),
        ],
        out_specs=pl.BlockSpec((rows, tn), lambda j: (0, j)),
        out_shape=jax.ShapeDtypeStruct((rows, N_MOD * D_MODEL), F32),
        compiler_params=_cparams(("arbitrary",)),
        name="ada",
    )(c_all, w_ada, b_ada.reshape(1, -1))


def _rope(z, cos_ref, sa_ref, sb_ref):
    return (z * cos_ref[...]
            + pltpu.roll(z, LANES - ROT_DIM // 2, 1) * sa_ref[...]
            + pltpu.roll(z, ROT_DIM // 2, 1) * sb_ref[...])


def _mixer_in_kernel(x_ref, shift_ref, scale_ref, g_ref, w_ref, cos_ref, sa_ref, sb_ref,
                     *refs, tm, decode):
    j = pl.program_id(1)
    if decode:
        q4_ref, k4_ref, v4_ref, u_ref, wbf_ref, h_scr, wc_scr = refs
        wc_scr[...] = w_ref[...].astype(BF16)

        @pl.when(pl.program_id(0) == 0)
        def _():
            wbf_ref[...] = wc_scr[...]
    else:
        qb_ref, k4_ref, kb_ref, v4_ref, vb_ref, u_ref, h_scr = refs

    def weights(sec, lo, hi):
        if decode:
            return wc_scr[:, lo:hi]
        return w_ref[:, sec * ATTN_WIDTH + lo:sec * ATTN_WIDTH + hi]

    def heads(sec):
        for hp in range(N_HEADS // 2):
            z2 = jnp.dot(h_scr[...], weights(sec, 2 * hp * V_DIM, 2 * (hp + 1) * V_DIM),
                         preferred_element_type=F32)
            for h in (2 * hp, 2 * hp + 1):
                yield h, slice(h * V_DIM, (h + 1) * V_DIM), z2[:, (h % 2) * V_DIM:(h % 2 + 1) * V_DIM]

    def section(sec):
        if sec == 0:
            for h, cols, zh in heads(sec):
                q = _rope(zh, cos_ref, sa_ref, sb_ref) * (HEAD_DIM ** -0.5)
                if decode:
                    q4_ref[pl.ds(h, tm, stride=N_HEADS), :] = q
                else:
                    qb_ref[:, cols] = (q * LOG2_E).astype(BF16)
        elif sec == 1:
            for h, cols, zh in heads(sec):
                k = _rope(zh, cos_ref, sa_ref, sb_ref)
                k4_ref[pl.ds(h, tm, stride=N_HEADS), :] = k
                if not decode:
                    kb_ref[:, cols] = k.astype(BF16)
        elif sec == 2:
            for h, cols, zh in heads(sec):
                v4_ref[pl.ds(h, tm, stride=N_HEADS), :] = zh
                if not decode:
                    vb_ref[:, cols] = zh.astype(BF16)
        else:
            u_ref[...] = jnp.dot(h_scr[...], weights(sec, 0, POOL_WIDTH), preferred_element_type=F32)

    n_sec = IN_WIDTH // ATTN_WIDTH
    if decode:
        pl.when(j == 0)(functools.partial(_modulate_into, h_scr, x_ref, g_ref, shift_ref, scale_ref))
        for sec in range(n_sec):
            pl.when(j == sec)(functools.partial(section, sec))
    else:
        _modulate_into(h_scr, x_ref, g_ref, shift_ref, scale_ref)
        for sec in range(n_sec):
            section(sec)


def _mixer_in(x, mod, g_pre, w_in, tables, *, tm, seq_tiles, decode):
    rows = x.shape[0]
    nt = rows // tm
    if decode:
        mod_spec = lambda c: pl.BlockSpec((tm // SUBLANES, D_MODEL), lambda i, j: (i, c))
        tab_spec = pl.BlockSpec((tm, LANES), lambda i, j: (i, 0))
    else:
        mod_spec = lambda c: pl.BlockSpec((None, 1, D_MODEL), lambda i, j: (i // seq_tiles, 0, c))
        tab_spec = pl.BlockSpec((tm, LANES), lambda i, j: (i % seq_tiles, 0))
    row_bf = pl.BlockSpec((tm, ATTN_WIDTH), lambda i, j: (i, 0))
    row4 = pl.BlockSpec((tm * N_HEADS, V_DIM), lambda i, j: (i, 0))
    sd_bf = jax.ShapeDtypeStruct((rows, ATTN_WIDTH), BF16)
    sd4 = jax.ShapeDtypeStruct((rows * N_HEADS, V_DIM), F32)
    sd_u = jax.ShapeDtypeStruct((rows, POOL_WIDTH), F32)
    n_sec = IN_WIDTH // ATTN_WIDTH
    scratch = [pltpu.VMEM((tm, D_MODEL), BF16)]
    if decode:
        w_out_spec = pl.BlockSpec((D_MODEL, ATTN_WIDTH),
                                  lambda i, j: (0, jnp.where(i == 0, j, n_sec - 1)))
        out_specs = [row4, row4, row4, row_bf, w_out_spec]
        out_shape = [sd4, sd4, sd4, sd_u, jax.ShapeDtypeStruct((D_MODEL, IN_WIDTH), BF16)]
        scratch.append(pltpu.VMEM((D_MODEL, ATTN_WIDTH), BF16))
        steps = n_sec
        w_spec = pl.BlockSpec((D_MODEL, ATTN_WIDTH), lambda i, j: (0, j))
    else:
        out_specs = [row_bf, row4, row_bf, row4, row_bf, row_bf]
        out_shape = [sd_bf, sd4, sd_bf, sd4, sd_bf, sd_u]
        steps = 1
        w_spec = pl.BlockSpec((D_MODEL, IN_WIDTH), lambda i, j: (0, 0), pipeline_mode=pl.Buffered(1))
    return pl.pallas_call(
        functools.partial(_mixer_in_kernel, tm=tm, decode=decode),
        grid=(nt, steps),
        in_specs=[
            pl.BlockSpec((tm, D_MODEL), lambda i, j: (i, 0)),
            mod_spec(0), mod_spec(1),
            pl.BlockSpec((1, D_MODEL), lambda i, j: (0, 0)),
            w_spec,
            tab_spec, tab_spec, tab_spec,
        ],
        out_specs=out_specs,
        out_shape=out_shape,
        scratch_shapes=scratch,
        compiler_params=_cparams(("arbitrary", "arbitrary")),
        name="mixer_in_decode" if decode else "mixer_in_prompt",
    )(x, mod, mod, g_pre.reshape(1, -1), w_in, *tables)


def _rope_tables(pos):
    half = ROT_DIM // 2
    inv = 1.0 / (ROPE_THETA ** (np.arange(0, ROT_DIM, 2, dtype=np.float64) / ROT_DIM))
    ang = np.asarray(pos, np.float64)[:, None] * inv[None, :]
    cos, sin = np.cos(ang), np.sin(ang)
    lane = np.arange(LANES) % HEAD_DIM
    first = (lane < half)[None, :]
    second = ((lane >= half) & (lane < ROT_DIM))[None, :]
    cos_l = cos[:, lane % half]
    sin_l = sin[:, lane % half]
    c = np.where(first | second, cos_l, 1.0)
    sa = np.where(first, -sin_l, 0.0)
    sb = np.where(second, sin_l, 0.0)
    return tuple(jnp.asarray(t, F32) for t in (c, sa, sb))


def _lam(lq1, lk1, lq2, lk2, lam_init):
    a = jnp.sum(lq1[...] * lk1[...], axis=-1, keepdims=True)
    b = jnp.sum(lq2[...] * lk2[...], axis=-1, keepdims=True)
    return jnp.exp(a) - jnp.exp(b) + lam_init


def _flash_kernel(q_ref, k_ref, v_ref, lq1, lk1, lq2, lk2, g_ref, o_ref,
                  vt_scr, s_scr, *, tq, nq, lam_init):
    seq = nq * tq
    vq = 2 * tq
    grp = tq // SUBLANES
    vt_scr[0:V_DIM, :] = v_ref[...].astype(F32).T.astype(BF16)
    vt_scr[V_DIM:, :] = jnp.ones((BF16_ROWS, seq), BF16)
    lam = _lam(lq1, lk1, lq2, lk2, lam_init)
    lane = lax.broadcasted_iota(jnp.int32, (tq, V_DIM), 1)
    kv = lax.broadcasted_iota(jnp.int32, (tq, vq), 0)
    qp = lax.broadcasted_iota(jnp.int32, (tq, vq), 1) & (tq - 1)
    causal = kv <= qp

    def stacked_queries(ii):
        q = q_ref[ii * tq:(ii + 1) * tq, :].astype(F32)
        return jnp.concatenate([jnp.where(lane < HEAD_DIM, q, 0.0),
                                jnp.where(lane >= HEAD_DIM, q, 0.0)], axis=0).astype(BF16)

    def score_chunk(ii, c, qq, mx):
        rows = slice(c * tq, (c + 1) * tq)
        s = lax.dot_general(k_ref[rows, :], qq, (((1,), (1,)), ((), ())),
                            preferred_element_type=F32)
        if c == ii:
            s = jnp.where(causal, s, NEG_BIG)
        s_scr[ii % 2, rows, :] = s
        return jnp.maximum(mx, jnp.max(s.reshape(grp, SUBLANES, vq), axis=0))

    def value_chunk(ii, c, m, o_t):
        rows = slice(c * tq, (c + 1) * tq)
        p = jnp.exp2(s_scr[ii % 2, rows, :] - m)
        return o_t + jnp.dot(vt_scr[:, rows], p.astype(BF16), preferred_element_type=F32)

    mx_init = jnp.full((SUBLANES, vq), NEG_BIG, F32)
    mx = score_chunk(0, 0, stacked_queries(0), mx_init)
    for ii in range(nq):
        m = jnp.max(mx, axis=0, keepdims=True)
        o_t = jnp.zeros((V_DIM + BF16_ROWS, vq), F32)
        nxt = ii + 1 < nq
        if nxt:
            qq_next = stacked_queries(ii + 1)
            mx = mx_init
        for c in range(ii + 2):
            if nxt:
                mx = score_chunk(ii + 1, c, qq_next, mx)
            if c <= ii:
                o_t = value_chunk(ii, c, m, o_t)
        n_t = o_t[:V_DIM] / o_t[V_DIM:V_DIM + 1]
        out = (n_t[:, :tq] - lam * n_t[:, tq:]).T
        o_ref[ii * tq:(ii + 1) * tq, :] = (_rms(out, g_ref[...]) * (1.0 - lam_init)).astype(BF16)


def _flash(q_bf, k_bf, v_bf, lams, g_subln, *, batch, seq, lam_init, tq=256):
    nq = seq // tq
    lam_spec = pl.BlockSpec((1, HEAD_DIM), lambda b, h: (0, 0))
    head_spec = pl.BlockSpec((seq, V_DIM), lambda b, h: (b, h))
    return pl.pallas_call(
        functools.partial(_flash_kernel, tq=tq, nq=nq, lam_init=lam_init),
        grid=(batch, N_HEADS),
        in_specs=[head_spec, head_spec, head_spec, lam_spec, lam_spec, lam_spec, lam_spec,
                  pl.BlockSpec((1, V_DIM), lambda b, h: (0, 0))],
        out_specs=head_spec,
        out_shape=jax.ShapeDtypeStruct((batch * seq, ATTN_WIDTH), BF16),
        scratch_shapes=[pltpu.VMEM((V_DIM + BF16_ROWS, seq), BF16),
                        pltpu.VMEM((2, seq, 2 * tq), F32)],
        compiler_params=_cparams(("parallel", "parallel")),
        name="flash_prompt",
    )(q_bf, k_bf, v_bf, *lams, g_subln.reshape(1, -1))


def _decode_kernel(pt_ref, q_ref, kn_ref, vn_ref, ck_hbm, cv_hbm, lq1, lk1, lq2, lk2, g_ref, o_ref,
                   kbuf, vbuf, ksem, vsem, m_scr, l_scr, acc_scr, *, layer, n_pages, n_groups,
                   n_steps, dec_seq, lam_init):
    n_seqs = 1
    g = pl.program_id(1)
    t = pl.program_id(0) * n_groups + g
    qh = dec_seq * N_HEADS
    page_rows = PAGE_SIZE * N_HEADS

    def page_copies(step, slot):
        cps = []
        for p in range(n_pages):
            page = pt_ref[step * n_pages + p]
            cps.append(pltpu.make_async_copy(ck_hbm.at[layer, page], kbuf.at[slot, p], ksem.at[slot]))
            cps.append(pltpu.make_async_copy(cv_hbm.at[layer, page], vbuf.at[slot, p], vsem.at[slot]))
        return cps

    ahead = RING_SLOTS - 1

    @pl.when(t == 0)
    def _():
        for s in range(min(ahead, n_steps)):
            for cp in page_copies(s, s):
                cp.start()

    @pl.when(t + ahead < n_steps)
    def _():
        for cp in page_copies(t + ahead, lax.rem(t + ahead, RING_SLOTS)):
            cp.start()

    slot = lax.rem(t, RING_SLOTS)
    for cp in page_copies(t, slot):
        cp.wait()
    k_refs = [kbuf.at[slot, p] for p in range(n_pages)]
    v_refs = [vbuf.at[slot, p] for p in range(n_pages)]

    @pl.when(g == 0)
    def _():
        m_scr[...] = jnp.full(m_scr.shape, NEG_BIG, F32)
        l_scr[...] = jnp.zeros(l_scr.shape, F32)
        acc_scr[...] = jnp.zeros(acc_scr.shape, F32)

    sub = lax.broadcasted_iota(jnp.int32, (SUBLANES, LANES), 0)
    lane = lax.broadcasted_iota(jnp.int32, (SUBLANES, LANES), 1)
    diag = (lane & (N_HEADS - 1)) == sub
    lane_q = lax.broadcasted_iota(jnp.int32, (qh, V_DIM), 1)
    lam = _lam(lq1, lk1, lq2, lk2, lam_init)

    def query_cols(b):
        r = q_ref[b * qh:(b + 1) * qh, :]
        return jnp.concatenate([jnp.where(lane_q < HEAD_DIM, r, 0.0),
                                jnp.where(lane_q >= HEAD_DIM, r, 0.0)], axis=0).astype(BF16)

    def update(b, s_list, v_list):
        m_prev = m_scr[b]
        m_new = m_prev
        for s in s_list:
            m_new = jnp.maximum(m_new, jnp.max(s, axis=0))
        alpha = jnp.exp(m_prev - m_new)
        m_eff = jnp.where(diag, m_new, -NEG_BIG)
        l_new = alpha * l_scr[b]
        pv = jnp.zeros((V_DIM, LANES), F32)
        for s, v in zip(s_list, v_list):
            p = jnp.exp(s - m_eff[None])
            l_new = l_new + jnp.sum(p, axis=0)
            pm = p.reshape(v.shape[0], LANES).astype(BF16)
            pv = pv + lax.dot_general(v.astype(BF16), pm, (((0,), (0,)), ((), ())),
                                      preferred_element_type=F32)
        alpha_row = jnp.sum(jnp.where(diag, alpha, 0.0), axis=0, keepdims=True)
        acc_scr[b] = acc_scr[b] * alpha_row + pv
        l_scr[b] = l_new
        m_scr[b] = m_new

    for b in range(n_seqs):
        qt = query_cols(b)
        pages = range(b * n_pages, (b + 1) * n_pages)
        s_list = [
            lax.dot_general(k_refs[p][...].reshape(page_rows, V_DIM).astype(BF16), qt,
                            (((1,), (1,)), ((), ())), preferred_element_type=F32
                            ).reshape(PAGE_SIZE, N_HEADS, LANES) for p in pages]
        update(b, s_list, [v_refs[p][...].reshape(page_rows, V_DIM) for p in pages])

    @pl.when(g == n_groups - 1)
    def _():
        for b in range(n_seqs):
            tok = slice(b * qh, (b + 1) * qh)
            s_new = lax.dot_general(kn_ref[tok, :].astype(BF16), query_cols(b),
                                    (((1,), (1,)), ((), ())), preferred_element_type=F32)
            s_new = s_new.reshape(dec_seq, N_HEADS, LANES)
            pos = lax.broadcasted_iota(jnp.int32, s_new.shape, 0)
            qidx = (lax.broadcasted_iota(jnp.int32, s_new.shape, 2) & (qh - 1)) >> 3
            update(b, [jnp.where(pos <= qidx, s_new, NEG_BIG)], [vn_ref[tok, :]])
            l_row = jnp.sum(jnp.where(diag, l_scr[b], 0.0), axis=0, keepdims=True)
            nt = (acc_scr[b] / l_row).T
            out = nt[:qh] - lam * nt[qh:]
            o_ref[tok, :] = _rms(out, g_ref[...]) * (1.0 - lam_init)


def _decode(q4, k4, v4, cache_k, cache_v, layer, page_table, lams, g_subln, *,
            dec_batch, dec_seq, lam_init, n_pages=8):
    pages_per_seq = page_table.shape[1]
    assert pages_per_seq % n_pages == 0
    n_groups = pages_per_seq // n_pages
    qh = dec_seq * N_HEADS
    tok_spec = pl.BlockSpec((qh, V_DIM), lambda b, g, pt: (b, 0))
    lam_spec = pl.BlockSpec((1, HEAD_DIM), lambda b, g, pt: (0, 0))
    hbm_spec = pl.BlockSpec(memory_space=pl.ANY)
    page_buf = pltpu.VMEM((RING_SLOTS, n_pages, PAGE_SIZE, N_HEADS, V_DIM), F32)
    grid_spec = pltpu.PrefetchScalarGridSpec(
        num_scalar_prefetch=1,
        grid=(dec_batch, n_groups),
        in_specs=[tok_spec, tok_spec, tok_spec, hbm_spec, hbm_spec]
        + [lam_spec] * 4 + [pl.BlockSpec((1, V_DIM), lambda b, g, pt: (0, 0))],
        out_specs=tok_spec,
        scratch_shapes=[page_buf, page_buf,
                        pltpu.SemaphoreType.DMA((RING_SLOTS,)), pltpu.SemaphoreType.DMA((RING_SLOTS,)),
                        pltpu.VMEM((1, SUBLANES, LANES), F32), pltpu.VMEM((1, SUBLANES, LANES), F32),
                        pltpu.VMEM((1, V_DIM, LANES), F32)],
    )
    return pl.pallas_call(
        functools.partial(_decode_kernel, layer=layer, n_pages=n_pages, n_groups=n_groups,
                          n_steps=dec_batch * n_groups, dec_seq=dec_seq, lam_init=lam_init),
        grid_spec=grid_spec,
        out_shape=jax.ShapeDtypeStruct((dec_batch * qh, V_DIM), F32),
        compiler_params=_cparams(("arbitrary", "arbitrary")),
        name="decode_attn",
    )(page_table.reshape(-1), q4, k4, v4, cache_k, cache_v, *lams, g_subln.reshape(1, -1))


def _mixer_out_kernel(x_ref, a_ref, u_ref, hist_ref, gate_ref, gpost_ref, wout_ref, wpool_ref,
                      ps_ref, o_ref, ext_scr, m_scr, pool_scr, *, nb, t, seq_tiles, decode, pos_base):
    rows = nb * t
    if decode:
        a = jnp.concatenate([a_ref[pl.ds(h, rows, stride=N_HEADS), :] for h in range(N_HEADS)],
                            axis=1).astype(BF16)
    else:
        a = a_ref[...]
    m_scr[...] = jnp.dot(a, wout_ref[0:ATTN_WIDTH, :], preferred_element_type=F32)

    ext_scr[:, HIST_PAD:HIST_PAD + t, :] = u_ref[...]
    if decode:
        ext_scr[:, HIST_PAD - POOL_HIST:HIST_PAD, :] = hist_ref[...]
        base = pos_base
    else:
        first = (pl.program_id(0) % seq_tiles) == 0
        ext_scr[:, 0:HIST_PAD, :] = jnp.where(first, 0.0, hist_ref[...])
        base = (pl.program_id(0) % seq_tiles) * t

    tb = min(t, POOL_ROWS)
    bb = POOL_ROWS // tb
    for gi, w in enumerate(POOL_WINDOWS):
        cs = slice(gi * POOL_GROUP_WIDTH, (gi + 1) * POOL_GROUP_WIDTH)
        for b0 in range(0, nb, bb):
            for t0 in range(0, t, tb):
                lo = HIST_PAD + t0
                cur_g = ext_scr[b0:b0 + bb, lo:lo + tb, cs]
                win = cur_g
                for k in range(1, w):
                    win = win + ext_scr[b0:b0 + bb, lo - k:lo - k + tb, cs]
                pos = base + t0 + lax.broadcasted_iota(jnp.int32, (1, tb, POOL_GROUP_WIDTH), 1)
                cnt = jnp.minimum(w, pos + 1).astype(F32)
                r0 = b0 * t + t0
                pool_scr[r0:r0 + bb * tb, cs] = (win / cnt - cur_g).reshape(
                    bb * tb, POOL_GROUP_WIDTH).astype(BF16)
    ys = [jnp.dot(pool_scr[:, gi * POOL_GROUP_WIDTH:(gi + 1) * POOL_GROUP_WIDTH], wpool_ref[gi],
                  preferred_element_type=F32) for gi in range(len(POOL_WINDOWS))]
    y = jnp.concatenate(ys, axis=1) * ps_ref[...]

    m_scr[...] += jnp.dot(y.astype(BF16), wout_ref[ATTN_WIDTH:, :], preferred_element_type=F32)
    _gated_residual_into(o_ref, x_ref, m_scr, gpost_ref, gate_ref)


def _mixer_out(x, a, u3, hist, mod, g_post, w_out_bf, w_pool_bf, pool_scale, *,
               nb, t, seq_tiles, decode, pos_base=0):
    rows_total = x.shape[0]
    tm = nb * t
    nt = rows_total // tm
    if decode:
        a_spec = pl.BlockSpec((tm * N_HEADS, V_DIM), lambda i: (i, 0))
        u_spec = pl.BlockSpec((nb, t, POOL_WIDTH), lambda i: (i, 0, 0))
        hist_spec = pl.BlockSpec((nb, POOL_HIST, POOL_WIDTH), lambda i: (i, 0, 0))
        gate_spec = pl.BlockSpec((nb, D_MODEL), lambda i: (i, 2))
    else:
        a_spec = pl.BlockSpec((tm, ATTN_WIDTH), lambda i: (i, 0))
        u_spec = pl.BlockSpec((1, t, POOL_WIDTH), lambda i: (i // seq_tiles, i % seq_tiles, 0))
        hpt = t // HIST_PAD
        hist_spec = pl.BlockSpec(
            (1, HIST_PAD, POOL_WIDTH),
            lambda i: (i // seq_tiles, jnp.maximum((i % seq_tiles) * hpt - 1, 0), 0))
        gate_spec = pl.BlockSpec((None, 1, D_MODEL), lambda i: (i // seq_tiles, 0, 2))
    return pl.pallas_call(
        functools.partial(_mixer_out_kernel, nb=nb, t=t, seq_tiles=seq_tiles, decode=decode,
                          pos_base=pos_base),
        grid=(nt,),
        in_specs=[
            pl.BlockSpec((tm, D_MODEL), lambda i: (i, 0)),
            a_spec, u_spec, hist_spec, gate_spec,
            pl.BlockSpec((1, D_MODEL), lambda i: (0, 0)),
            pl.BlockSpec((D_MODEL, D_MODEL), lambda i: (0, 0)),
            pl.BlockSpec((len(POOL_WINDOWS), POOL_GROUP_WIDTH, POOL_GROUP_WIDTH), lambda i: (0, 0, 0)),
            pl.BlockSpec((1, POOL_WIDTH), lambda i: (0, 0)),
        ],
        out_specs=pl.BlockSpec((tm, D_MODEL), lambda i: (i, 0)),
        out_shape=jax.ShapeDtypeStruct((rows_total, D_MODEL), F32),
        scratch_shapes=[pltpu.VMEM((nb, HIST_PAD + t, POOL_WIDTH), F32),
                        pltpu.VMEM((tm, D_MODEL), F32), pltpu.VMEM((tm, POOL_WIDTH), BF16)],
        compiler_params=_cparams(("parallel",)),
        name="mixer_out_decode" if decode else "mixer_out_prompt",
    )(x, a, u3, hist, mod, g_post.reshape(1, -1), w_out_bf, w_pool_bf, pool_scale.reshape(1, -1))


def _ffn_kernel(x_ref, shift_ref, scale_ref, gate_ref, gpre_ref, gpost_ref, w1_ref, w2_ref,
                *refs, n_f, decode):
    j = pl.program_id(1)
    if decode:
        o_ref, w1bf_ref, w2bf_ref, h_scr, acc_scr = refs
        w1bf_ref[...] = w1_ref[...].astype(BF16)
        w2bf_ref[...] = w2_ref[...].astype(BF16)
        w1_ref, w2_ref = w1bf_ref, w2bf_ref
    else:
        o_ref, h_scr, acc_scr = refs

    @pl.when(j == 0)
    def _():
        _modulate_into(h_scr, x_ref, gpre_ref, shift_ref, scale_ref)
        acc_scr[...] = jnp.zeros(acc_scr.shape, F32)

    t = jnp.dot(h_scr[...], w1_ref[...], preferred_element_type=F32)
    t = jnp.square(jnp.maximum(t, 0.0))
    acc_scr[...] += jnp.dot(t.astype(BF16), w2_ref[...], preferred_element_type=F32)

    @pl.when(j == n_f - 1)
    def _():
        _gated_residual_into(o_ref, x_ref, acc_scr, gpost_ref, gate_ref)


def _ffn(x, mod, g_pre, g_post, w1, w2, *, tm, tf, seq_tiles, decode):
    rows = x.shape[0]
    nt = rows // tm
    n_f = D_FF // tf
    row_mode = dict(pipeline_mode=pl.Buffered(1)) if nt == 1 else {}
    x_spec = pl.BlockSpec((tm, D_MODEL), lambda i, j: (i, 0), **row_mode)
    out_specs = pl.BlockSpec((tm, D_MODEL), lambda i, j: (i, 0), **row_mode)
    out_shape = jax.ShapeDtypeStruct((rows, D_MODEL), F32)
    scratch = [pltpu.VMEM((tm, D_MODEL), BF16), pltpu.VMEM((tm, D_MODEL), F32)]
    if decode:
        assert nt == 1
        mod_spec = lambda c: pl.BlockSpec((tm // SUBLANES, D_MODEL), lambda i, j: (i, c))
        out_specs = [out_specs, pl.BlockSpec((D_MODEL, tf), lambda i, j: (0, j)),
                     pl.BlockSpec((tf, D_MODEL), lambda i, j: (j, 0))]
        out_shape = [out_shape, jax.ShapeDtypeStruct((D_MODEL, D_FF), BF16),
                     jax.ShapeDtypeStruct((D_FF, D_MODEL), BF16)]
    else:
        mod_spec = lambda c: pl.BlockSpec((None, 1, D_MODEL), lambda i, j: (i // seq_tiles, 0, c))
    vec_spec = pl.BlockSpec((1, D_MODEL), lambda i, j: (0, 0))
    return pl.pallas_call(
        functools.partial(_ffn_kernel, n_f=n_f, decode=decode),
        grid=(nt, n_f),
        in_specs=[
            x_spec,
            mod_spec(3), mod_spec(4), mod_spec(5),
            vec_spec, vec_spec,
            pl.BlockSpec((D_MODEL, tf), lambda i, j: (0, j)),
            pl.BlockSpec((tf, D_MODEL), lambda i, j: (j, 0)),
        ],
        out_specs=out_specs,
        out_shape=out_shape,
        scratch_shapes=scratch,
        compiler_params=_cparams(("arbitrary", "arbitrary")),
        name="ffn_decode" if decode else "ffn_prompt",
    )(x, mod, mod, mod, g_pre.reshape(1, -1), g_post.reshape(1, -1), w1, w2)


def kernel(x_prompt, x_sample, cache_k, cache_v, state_pool, page_table, c_prompt, c_sample,
           w_ada, b_ada, g_pre_mix, w_in, lambda_q1, lambda_k1, lambda_q2, lambda_k2, g_subln,
           w_pool, pool_scale, w_out, g_post_mix, g_pre_ffn, w_ff1, w_ff2, g_post_ffn):
    batch, seq, _ = x_prompt.shape
    dec_batch, dec_seq, _ = x_sample.shape
    depth = w_in.shape[0]
    past_len = page_table.shape[1] * PAGE_SIZE
    assert dec_seq == SUBLANES and state_pool.shape[2] == POOL_HIST

    tm_p = 512
    seq_tiles = seq // tm_p
    rows_s = dec_batch * dec_seq
    tables_p = _rope_tables(np.arange(seq))
    tables_s = _rope_tables(past_len + (np.arange(rows_s) % dec_seq))

    n_c = batch + dec_batch
    c_all = jnp.concatenate(
        [c_sample, c_prompt, jnp.zeros((-n_c % SUBLANES, D_MODEL), F32)], axis=0)

    xp = x_prompt.reshape(batch * seq, D_MODEL)
    xs = x_sample.reshape(rows_s, D_MODEL)
    outs = [[] for _ in range(6)]
    for l in range(depth):
        lam_init = 0.8 - 0.6 * math.exp(-0.3 * l)
        lams = [v[l].reshape(1, HEAD_DIM) for v in (lambda_q1, lambda_k1, lambda_q2, lambda_k2)]
        w_out_bf = w_out[l].astype(BF16)
        w_pool_bf = w_pool[l].astype(BF16)

        m_all = _ada(c_all, w_ada[l], b_ada[l])
        mod_p = m_all[dec_batch:n_c].reshape(batch, 1, N_MOD * D_MODEL)
        mod_s = m_all

        q4s, k4s, v4s, us, w_in_bf = _mixer_in(
            xs, mod_s, g_pre_mix[l], w_in[l], tables_s, tm=tm_p, seq_tiles=1, decode=True)
        a4s = _decode(q4s, k4s, v4s, cache_k, cache_v, l, page_table, lams, g_subln[l],
                      dec_batch=dec_batch, dec_seq=dec_seq, lam_init=lam_init)
        us3 = us.reshape(dec_batch, dec_seq, POOL_WIDTH)
        xs = _mixer_out(xs, a4s, us3, state_pool[l], mod_s, g_post_mix[l], w_out_bf, w_pool_bf,
                        pool_scale[l], nb=tm_p // dec_seq, t=dec_seq, seq_tiles=1, decode=True,
                        pos_base=past_len)
        xs, w1_bf, w2_bf = _ffn(xs, mod_s, g_pre_ffn[l], g_post_ffn[l], w_ff1[l], w_ff2[l],
                                tm=rows_s, tf=512, seq_tiles=1, decode=True)
        outs[3].append(k4s.reshape(dec_batch, dec_seq, N_HEADS, V_DIM))
        outs[4].append(v4s.reshape(dec_batch, dec_seq, N_HEADS, V_DIM))
        outs[5].append(jnp.concatenate([state_pool[l][:, dec_seq:], us3], axis=1))

        q_bf, k4, k_bf, v4, v_bf, u = _mixer_in(
            xp, mod_p, g_pre_mix[l], w_in_bf, tables_p, tm=tm_p, seq_tiles=seq_tiles, decode=False)
        a_bf = _flash(q_bf, k_bf, v_bf, lams, g_subln[l], batch=batch, seq=seq, lam_init=lam_init)
        u3 = u.reshape(batch, seq, POOL_WIDTH)
        xp = _mixer_out(xp, a_bf, u3, u3, mod_p, g_post_mix[l], w_out_bf, w_pool_bf, pool_scale[l],
                        nb=1, t=tm_p, seq_tiles=seq_tiles, decode=False)
        xp = _ffn(xp, mod_p, g_pre_ffn[l], g_post_ffn[l], w1_bf, w2_bf,
                  tm=tm_p, tf=1024, seq_tiles=seq_tiles, decode=False)
        outs[0].append(k4.reshape(batch, seq, N_HEADS, V_DIM))
        outs[1].append(v4.reshape(batch, seq, N_HEADS, V_DIM))
        outs[2].append(u3[:, seq - POOL_HIST:])

    kp, vp, sp, ks, vs, ss = (jnp.stack(o) for o in outs)
    return (xp.reshape(batch, seq, D_MODEL), xs.reshape(dec_batch, dec_seq, D_MODEL),
            kp, vp, sp, ks, vs, ss)
```

```python
import functools
import math

import jax
import jax.numpy as jnp
import numpy as np
from jax import lax
from jax.experimental import pallas as pl
from jax.experimental.pallas import tpu as pltpu

F32 = jnp.float32
BF16 = jnp.bfloat16

D_MODEL = 2048
N_HEADS = 8
HEAD_DIM = 64
V_DIM = 2 * HEAD_DIM
ATTN_WIDTH = N_HEADS * V_DIM
POOL_WIDTH = 1024
IN_WIDTH = 3 * ATTN_WIDTH + POOL_WIDTH
ROT_DIM = HEAD_DIM // 4
ROPE_THETA = 500000.0
POOL_WINDOWS = (2, 4, 8, 16)
POOL_GROUP_WIDTH = POOL_WIDTH // len(POOL_WINDOWS)
POOL_HIST = max(POOL_WINDOWS) - 1
HIST_PAD = 16
D_FF = 4 * D_MODEL
N_MOD = 6
PAGE_SIZE = 128
EPS = 1e-6
NEG_BIG = -1e30
LOG2_E = math.log2(math.e)

LANES = 128
SUBLANES = 8
BF16_ROWS = 16
VMEM_LIMIT = 60 * 1024 * 1024
RING_SLOTS = 3
ROW_CHUNK = 16
POOL_ROWS = 64


def _cparams(sem):
    return pltpu.CompilerParams(dimension_semantics=sem, vmem_limit_bytes=VMEM_LIMIT)


def _rms(x, g):
    ms = jnp.mean(x * x, axis=-1, keepdims=True)
    return x * lax.rsqrt(ms + EPS) * g


def _mod_rows(ref, r0, nrows, tile_rows):
    n, d = ref.shape
    if n == 1:
        return ref[...]
    per = tile_rows // n
    m = ref[r0 // per:(r0 + nrows) // per, :]
    return jnp.broadcast_to(m[:, None, :], (nrows // per, per, d)).reshape(nrows, d)


def _modulate_into(h_ref, x_ref, g_ref, shift_ref, scale_ref):
    tm = x_ref.shape[0]
    for r in range(0, tm, ROW_CHUNK):
        h = (_rms(x_ref[r:r + ROW_CHUNK, :], g_ref[...])
             * (1.0 + _mod_rows(scale_ref, r, ROW_CHUNK, tm)) + _mod_rows(shift_ref, r, ROW_CHUNK, tm))
        h_ref[r:r + ROW_CHUNK, :] = h.astype(BF16)


def _gated_residual_into(o_ref, x_ref, m_ref, g_ref, gate_ref):
    tm = x_ref.shape[0]
    for r in range(0, tm, ROW_CHUNK):
        rows = slice(r, r + ROW_CHUNK)
        o_ref[rows, :] = x_ref[rows, :] + _mod_rows(gate_ref, r, ROW_CHUNK, tm) * _rms(m_ref[rows, :], g_ref[...])


def _ada_kernel(c_ref, w_ref, b_ref, o_ref):
    c = c_ref[...]
    s = c * (1.0 / (1.0 + jnp.exp(-c)))
    o_ref[...] = jnp.dot(s.astype(BF16), w_ref[...].astype(BF16),
                         preferred_element_type=F32) + b_ref[...]


def _ada(c_all, w_ada, b_ada):
    rows = c_all.shape[0]
    tn = 1024
    return pl.pallas_call(
        _ada_kernel,
        grid=(N_MOD * D_MODEL // tn,),
        in_specs=[
            pl.BlockSpec((rows, D_MODEL), lambda j: (0, 0)),
            pl.BlockSpec((D_MODEL, tn), lambda j: (0, j)),
            pl.BlockSpec((1, tn), lambda j: (0, j)),
        ],
        out_specs=pl.BlockSpec((rows, tn), lambda j: (0, j)),
        out_shape=jax.ShapeDtypeStruct((rows, N_MOD * D_MODEL), F32),
        compiler_params=_cparams(("arbitrary",)),
        name="ada",
    )(c_all, w_ada, b_ada.reshape(1, -1))


def _rope(z, cos_ref, sa_ref, sb_ref):
    return (z * cos_ref[...]
            + pltpu.roll(z, LANES - ROT_DIM // 2, 1) * sa_ref[...]
            + pltpu.roll(z, ROT_DIM // 2, 1) * sb_ref[...])


def _mixer_in_kernel(x_ref, shift_ref, scale_ref, g_ref, w_ref, cos_ref, sa_ref, sb_ref,
                     *refs, tm, decode):
    j = pl.program_id(1)
    if decode:
        q4_ref, k4_ref, v4_ref, u_ref, wbf_ref, h_scr, wc_scr = refs
        wc_scr[...] = w_ref[...].astype(BF16)

        @pl.when(pl.program_id(0) == 0)
        def _():
            wbf_ref[...] = wc_scr[...]
    else:
        qb_ref, k4_ref, kb_ref, v4_ref, vb_ref, u_ref, h_scr = refs

    def weights(sec, lo, hi):
        if decode:
            return wc_scr[:, lo:hi]
        return w_ref[:, sec * ATTN_WIDTH + lo:sec * ATTN_WIDTH + hi]

    def heads(sec):
        for hp in range(N_HEADS // 2):
            z2 = jnp.dot(h_scr[...], weights(sec, 2 * hp * V_DIM, 2 * (hp + 1) * V_DIM),
                         preferred_element_type=F32)
            for h in (2 * hp, 2 * hp + 1):
                yield h, slice(h * V_DIM, (h + 1) * V_DIM), z2[:, (h % 2) * V_DIM:(h % 2 + 1) * V_DIM]

    def section(sec):
        if sec == 0:
            for h, cols, zh in heads(sec):
                q = _rope(zh, cos_ref, sa_ref, sb_ref) * (HEAD_DIM ** -0.5)
                if decode:
                    q4_ref[pl.ds(h, tm, stride=N_HEADS), :] = q
                else:
                    qb_ref[:, cols] = (q * LOG2_E).astype(BF16)
        elif sec == 1:
            for h, cols, zh in heads(sec):
                k = _rope(zh, cos_ref, sa_ref, sb_ref)
                k4_ref[pl.ds(h, tm, stride=N_HEADS), :] = k
                if not decode:
                    kb_ref[:, cols] = k.astype(BF16)
        elif sec == 2:
            for h, cols, zh in heads(sec):
                v4_ref[pl.ds(h, tm, stride=N_HEADS), :] = zh
                if not decode:
                    vb_ref[:, cols] = zh.astype(BF16)
        else:
            u_ref[...] = jnp.dot(h_scr[...], weights(sec, 0, POOL_WIDTH), preferred_element_type=F32)

    n_sec = IN_WIDTH // ATTN_WIDTH
    if decode:
        pl.when(j == 0)(functools.partial(_modulate_into, h_scr, x_ref, g_ref, shift_ref, scale_ref))
        for sec in range(n_sec):
            pl.when(j == sec)(functools.partial(section, sec))
    else:
        _modulate_into(h_scr, x_ref, g_ref, shift_ref, scale_ref)
        for sec in range(n_sec):
            section(sec)


def _mixer_in(x, mod, g_pre, w_in, tables, *, tm, seq_tiles, decode):
    rows = x.shape[0]
    nt = rows // tm
    if decode:
        mod_spec = lambda c: pl.BlockSpec((tm // SUBLANES, D_MODEL), lambda i, j: (i, c))
        tab_spec = pl.BlockSpec((tm, LANES), lambda i, j: (i, 0))
    else:
        mod_spec = lambda c: pl.BlockSpec((None, 1, D_MODEL), lambda i, j: (i // seq_tiles, 0, c))
        tab_spec = pl.BlockSpec((tm, LANES), lambda i, j: (i % seq_tiles, 0))
    row_bf = pl.BlockSpec((tm, ATTN_WIDTH), lambda i, j: (i, 0))
    row4 = pl.BlockSpec((tm * N_HEADS, V_DIM), lambda i, j: (i, 0))
    sd_bf = jax.ShapeDtypeStruct((rows, ATTN_WIDTH), BF16)
    sd4 = jax.ShapeDtypeStruct((rows * N_HEADS, V_DIM), F32)
    sd_u = jax.ShapeDtypeStruct((rows, POOL_WIDTH), F32)
    n_sec = IN_WIDTH // ATTN_WIDTH
    scratch = [pltpu.VMEM((tm, D_MODEL), BF16)]
    if decode:
        w_out_spec = pl.BlockSpec((D_MODEL, ATTN_WIDTH),
                                  lambda i, j: (0, jnp.where(i == 0, j, n_sec - 1)))
        out_specs = [row4, row4, row4, row_bf, w_out_spec]
        out_shape = [sd4, sd4, sd4, sd_u, jax.ShapeDtypeStruct((D_MODEL, IN_WIDTH), BF16)]
        scratch.append(pltpu.VMEM((D_MODEL, ATTN_WIDTH), BF16))
        steps = n_sec
        w_spec = pl.BlockSpec((D_MODEL, ATTN_WIDTH), lambda i, j: (0, j))
    else:
        out_specs = [row_bf, row4, row_bf, row4, row_bf, row_bf]
        out_shape = [sd_bf, sd4, sd_bf, sd4, sd_bf, sd_u]
        steps = 1
        w_spec = pl.BlockSpec((D_MODEL, IN_WIDTH), lambda i, j: (0, 0), pipeline_mode=pl.Buffered(1))
    return pl.pallas_call(
        functools.partial(_mixer_in_kernel, tm=tm, decode=decode),
        grid=(nt, steps),
        in_specs=[
            pl.BlockSpec((tm, D_MODEL), lambda i, j: (i, 0)),
            mod_spec(0), mod_spec(1),
            pl.BlockSpec((1, D_MODEL), lambda i, j: (0, 0)),
            w_spec,
            tab_spec, tab_spec, tab_spec,
        ],
        out_specs=out_specs,
        out_shape=out_shape,
        scratch_shapes=scratch,
        compiler_params=_cparams(("arbitrary", "arbitrary")),
        name="mixer_in_decode" if decode else "mixer_in_prompt",
    )(x, mod, mod, g_pre.reshape(1, -1), w_in, *tables)


def _rope_tables(pos):
    half = ROT_DIM // 2
    inv = 1.0 / (ROPE_THETA ** (np.arange(0, ROT_DIM, 2, dtype=np.float64) / ROT_DIM))
    ang = np.asarray(pos, np.float64)[:, None] * inv[None, :]
    cos, sin = np.cos(ang), np.sin(ang)
    lane = np.arange(LANES) % HEAD_DIM
    first = (lane < half)[None, :]
    second = ((lane >= half) & (lane < ROT_DIM))[None, :]
    cos_l = cos[:, lane % half]
    sin_l = sin[:, lane % half]
    c = np.where(first | second, cos_l, 1.0)
    sa = np.where(first, -sin_l, 0.0)
    sb = np.where(second, sin_l, 0.0)
    return tuple(jnp.asarray(t, F32) for t in (c, sa, sb))


def _lam(lq1, lk1, lq2, lk2, lam_init):
    a = jnp.sum(lq1[...] * lk1[...], axis=-1, keepdims=True)
    b = jnp.sum(lq2[...] * lk2[...], axis=-1, keepdims=True)
    return jnp.exp(a) - jnp.exp(b) + lam_init


def _flash_kernel(q_ref, k_ref, v_ref, lq1, lk1, lq2, lk2, g_ref, o_ref,
                  vt_scr, s_scr, *, tq, nq, lam_init):
    seq = nq * tq
    vq = 2 * tq
    grp = tq // SUBLANES
    vt_scr[0:V_DIM, :] = v_ref[...].astype(F32).T.astype(BF16)
    vt_scr[V_DIM:, :] = jnp.ones((BF16_ROWS, seq), BF16)
    lam = _lam(lq1, lk1, lq2, lk2, lam_init)
    lane = lax.broadcasted_iota(jnp.int32, (tq, V_DIM), 1)
    kv = lax.broadcasted_iota(jnp.int32, (tq, vq), 0)
    qp = lax.broadcasted_iota(jnp.int32, (tq, vq), 1) & (tq - 1)
    causal = kv <= qp

    def stacked_queries(ii):
        q = q_ref[ii * tq:(ii + 1) * tq, :].astype(F32)
        return jnp.concatenate([jnp.where(lane < HEAD_DIM, q, 0.0),
                                jnp.where(lane >= HEAD_DIM, q, 0.0)], axis=0).astype(BF16)

    def score_chunk(ii, c, qq, mx):
        rows = slice(c * tq, (c + 1) * tq)
        s = lax.dot_general(k_ref[rows, :], qq, (((1,), (1,)), ((), ())),
                            preferred_element_type=F32)
        if c == ii:
            s = jnp.where(causal, s, NEG_BIG)
        s_scr[ii % 2, rows, :] = s
        return jnp.maximum(mx, jnp.max(s.reshape(grp, SUBLANES, vq), axis=0))

    def value_chunk(ii, c, m, o_t):
        rows = slice(c * tq, (c + 1) * tq)
        p = jnp.exp2(s_scr[ii % 2, rows, :] - m)
        return o_t + jnp.dot(vt_scr[:, rows], p.astype(BF16), preferred_element_type=F32)

    mx_init = jnp.full((SUBLANES, vq), NEG_BIG, F32)
    mx = score_chunk(0, 0, stacked_queries(0), mx_init)
    for ii in range(nq):
        m = jnp.max(mx, axis=0, keepdims=True)
        o_t = jnp.zeros((V_DIM + BF16_ROWS, vq), F32)
        nxt = ii + 1 < nq
        if nxt:
            qq_next = stacked_queries(ii + 1)
            mx = mx_init
        for c in range(ii + 2):
            if nxt:
                mx = score_chunk(ii + 1, c, qq_next, mx)
            if c <= ii:
                o_t = value_chunk(ii, c, m, o_t)
        n_t = o_t[:V_DIM] / o_t[V_DIM:V_DIM + 1]
        out = (n_t[:, :tq] - lam * n_t[:, tq:]).T
        o_ref[ii * tq:(ii + 1) * tq, :] = (_rms(out, g_ref[...]) * (1.0 - lam_init)).astype(BF16)


def _flash(q_bf, k_bf, v_bf, lams, g_subln, *, batch, seq, lam_init, tq=256):
    nq = seq // tq
    lam_spec = pl.BlockSpec((1, HEAD_DIM), lambda b, h: (0, 0))
    head_spec = pl.BlockSpec((seq, V_DIM), lambda b, h: (b, h))
    return pl.pallas_call(
        functools.partial(_flash_kernel, tq=tq, nq=nq, lam_init=lam_init),
        grid=(batch, N_HEADS),
        in_specs=[head_spec, head_spec, head_spec, lam_spec, lam_spec, lam_spec, lam_spec,
                  pl.BlockSpec((1, V_DIM), lambda b, h: (0, 0))],
        out_specs=head_spec,
        out_shape=jax.ShapeDtypeStruct((batch * seq, ATTN_WIDTH), BF16),
        scratch_shapes=[pltpu.VMEM((V_DIM + BF16_ROWS, seq), BF16),
                        pltpu.VMEM((2, seq, 2 * tq), F32)],
        compiler_params=_cparams(("parallel", "parallel")),
        name="flash_prompt",
    )(q_bf, k_bf, v_bf, *lams, g_subln.reshape(1, -1))


def _decode_kernel(pt_ref, q_ref, kn_ref, vn_ref, ck_hbm, cv_hbm, lq1, lk1, lq2, lk2, g_ref, o_ref,
                   kbuf, vbuf, ksem, vsem, m_scr, l_scr, acc_scr, *, layer, n_pages, n_groups,
                   n_steps, dec_seq, lam_init):
    n_seqs = 1
    g = pl.program_id(1)
    t = pl.program_id(0) * n_groups + g
    qh = dec_seq * N_HEADS
    page_rows = PAGE_SIZE * N_HEADS

    def page_copies(step, slot):
        cps = []
        for p in range(n_pages):
            page = pt_ref[step * n_pages + p]
            cps.append(pltpu.make_async_copy(ck_hbm.at[layer, page], kbuf.at[slot, p], ksem.at[slot]))
            cps.append(pltpu.make_async_copy(cv_hbm.at[layer, page], vbuf.at[slot, p], vsem.at[slot]))
        return cps

    ahead = RING_SLOTS - 1

    @pl.when(t == 0)
    def _():
        for s in range(min(ahead, n_steps)):
            for cp in page_copies(s, s):
                cp.start()

    @pl.when(t + ahead < n_steps)
    def _():
        for cp in page_copies(t + ahead, lax.rem(t + ahead, RING_SLOTS)):
            cp.start()

    slot = lax.rem(t, RING_SLOTS)
    for cp in page_copies(t, slot):
        cp.wait()
    k_refs = [kbuf.at[slot, p] for p in range(n_pages)]
    v_refs = [vbuf.at[slot, p] for p in range(n_pages)]

    @pl.when(g == 0)
    def _():
        m_scr[...] = jnp.full(m_scr.shape, NEG_BIG, F32)
        l_scr[...] = jnp.zeros(l_scr.shape, F32)
        acc_scr[...] = jnp.zeros(acc_scr.shape, F32)

    sub = lax.broadcasted_iota(jnp.int32, (SUBLANES, LANES), 0)
    lane = lax.broadcasted_iota(jnp.int32, (SUBLANES, LANES), 1)
    diag = (lane & (N_HEADS - 1)) == sub
    lane_q = lax.broadcasted_iota(jnp.int32, (qh, V_DIM), 1)
    lam = _lam(lq1, lk1, lq2, lk2, lam_init)

    def query_cols(b):
        r = q_ref[b * qh:(b + 1) * qh, :]
        return jnp.concatenate([jnp.where(lane_q < HEAD_DIM, r, 0.0),
                                jnp.where(lane_q >= HEAD_DIM, r, 0.0)], axis=0).astype(BF16)

    def update(b, s_list, v_list):
        m_prev = m_scr[b]
        m_new = m_prev
        for s in s_list:
            m_new = jnp.maximum(m_new, jnp.max(s, axis=0))
        alpha = jnp.exp(m_prev - m_new)
        m_eff = jnp.where(diag, m_new, -NEG_BIG)
        l_new = alpha * l_scr[b]
        pv = jnp.zeros((V_DIM, LANES), F32)
        for s, v in zip(s_list, v_list):
            p = jnp.exp(s - m_eff[None])
            l_new = l_new + jnp.sum(p, axis=0)
            pm = p.reshape(v.shape[0], LANES).astype(BF16)
            pv = pv + lax.dot_general(v.astype(BF16), pm, (((0,), (0,)), ((), ())),
                                      preferred_element_type=F32)
        alpha_row = jnp.sum(jnp.where(diag, alpha, 0.0), axis=0, keepdims=True)
        acc_scr[b] = acc_scr[b] * alpha_row + pv
        l_scr[b] = l_new
        m_scr[b] = m_new

    for b in range(n_seqs):
        qt = query_cols(b)
        pages = range(b * n_pages, (b + 1) * n_pages)
        s_list = [
            lax.dot_general(k_refs[p][...].reshape(page_rows, V_DIM).astype(BF16), qt,
                            (((1,), (1,)), ((), ())), preferred_element_type=F32
                            ).reshape(PAGE_SIZE, N_HEADS, LANES) for p in pages]
        update(b, s_list, [v_refs[p][...].reshape(page_rows, V_DIM) for p in pages])

    @pl.when(g == n_groups - 1)
    def _():
        for b in range(n_seqs):
            tok = slice(b * qh, (b + 1) * qh)
            s_new = lax.dot_general(kn_ref[tok, :].astype(BF16), query_cols(b),
                                    (((1,), (1,)), ((), ())), preferred_element_type=F32)
            s_new = s_new.reshape(dec_seq, N_HEADS, LANES)
            pos = lax.broadcasted_iota(jnp.int32, s_new.shape, 0)
            qidx = (lax.broadcasted_iota(jnp.int32, s_new.shape, 2) & (qh - 1)) >> 3
            update(b, [jnp.where(pos <= qidx, s_new, NEG_BIG)], [vn_ref[tok, :]])
            l_row = jnp.sum(jnp.where(diag, l_scr[b], 0.0), axis=0, keepdims=True)
            nt = (acc_scr[b] / l_row).T
            out = nt[:qh] - lam * nt[qh:]
            o_ref[tok, :] = _rms(out, g_ref[...]) * (1.0 - lam_init)


def _decode(q4, k4, v4, cache_k, cache_v, layer, page_table, lams, g_subln, *,
            dec_batch, dec_seq, lam_init, n_pages=8):
    pages_per_seq = page_table.shape[1]
    assert pages_per_seq % n_pages == 0
    n_groups = pages_per_seq // n_pages
    qh = dec_seq * N_HEADS
    tok_spec = pl.BlockSpec((qh, V_DIM), lambda b, g, pt: (b, 0))
    lam_spec = pl.BlockSpec((1, HEAD_DIM), lambda b, g, pt: (0, 0))
    hbm_spec = pl.BlockSpec(memory_space=pl.ANY)
    page_buf = pltpu.VMEM((RING_SLOTS, n_pages, PAGE_SIZE, N_HEADS, V_DIM), F32)
    grid_spec = pltpu.PrefetchScalarGridSpec(
        num_scalar_prefetch=1,
        grid=(dec_batch, n_groups),
        in_specs=[tok_spec, tok_spec, tok_spec, hbm_spec, hbm_spec]
        + [lam_spec] * 4 + [pl.BlockSpec((1, V_DIM), lambda b, g, pt: (0, 0))],
        out_specs=tok_spec,
        scratch_shapes=[page_buf, page_buf,
                        pltpu.SemaphoreType.DMA((RING_SLOTS,)), pltpu.SemaphoreType.DMA((RING_SLOTS,)),
                        pltpu.VMEM((1, SUBLANES, LANES), F32), pltpu.VMEM((1, SUBLANES, LANES), F32),
                        pltpu.VMEM((1, V_DIM, LANES), F32)],
    )
    return pl.pallas_call(
        functools.partial(_decode_kernel, layer=layer, n_pages=n_pages, n_groups=n_groups,
                          n_steps=dec_batch * n_groups, dec_seq=dec_seq, lam_init=lam_init),
        grid_spec=grid_spec,
        out_shape=jax.ShapeDtypeStruct((dec_batch * qh, V_DIM), F32),
        compiler_params=_cparams(("arbitrary", "arbitrary")),
        name="decode_attn",
    )(page_table.reshape(-1), q4, k4, v4, cache_k, cache_v, *lams, g_subln.reshape(1, -1))


def _mixer_out_kernel(x_ref, a_ref, u_ref, hist_ref, gate_ref, gpost_ref, wout_ref, wpool_ref,
                      ps_ref, *refs, nb, t, seq_tiles, decode, pos_base):
    if decode:
        o_ref, state_ref, ext_scr, m_scr, pool_scr = refs
    else:
        o_ref, ext_scr, m_scr, pool_scr = refs
    rows = nb * t
    if decode:
        a = jnp.concatenate([a_ref[pl.ds(h, rows, stride=N_HEADS), :] for h in range(N_HEADS)],
                            axis=1).astype(BF16)
    else:
        a = a_ref[...]
    m_scr[...] = jnp.dot(a, wout_ref[0:ATTN_WIDTH, :], preferred_element_type=F32)

    ext_scr[:, HIST_PAD:HIST_PAD + t, :] = u_ref[...]
    if decode:
        ext_scr[:, HIST_PAD - POOL_HIST:HIST_PAD, :] = hist_ref[...]
        state_ref[...] = ext_scr[:, HIST_PAD + t - POOL_HIST:HIST_PAD + t, :]
        base = pos_base
    else:
        first = (pl.program_id(0) % seq_tiles) == 0
        ext_scr[:, 0:HIST_PAD, :] = jnp.where(first, 0.0, hist_ref[...])
        base = (pl.program_id(0) % seq_tiles) * t

    tb = min(t, POOL_ROWS)
    bb = POOL_ROWS // tb
    for gi, w in enumerate(POOL_WINDOWS):
        cs = slice(gi * POOL_GROUP_WIDTH, (gi + 1) * POOL_GROUP_WIDTH)
        for b0 in range(0, nb, bb):
            for t0 in range(0, t, tb):
                lo = HIST_PAD + t0
                cur_g = ext_scr[b0:b0 + bb, lo:lo + tb, cs]
                win = cur_g
                for k in range(1, w):
                    win = win + ext_scr[b0:b0 + bb, lo - k:lo - k + tb, cs]
                pos = base + t0 + lax.broadcasted_iota(jnp.int32, (1, tb, POOL_GROUP_WIDTH), 1)
                cnt = jnp.minimum(w, pos + 1).astype(F32)
                r0 = b0 * t + t0
                pool_scr[r0:r0 + bb * tb, cs] = (win / cnt - cur_g).reshape(
                    bb * tb, POOL_GROUP_WIDTH).astype(BF16)
    ys = [jnp.dot(pool_scr[:, gi * POOL_GROUP_WIDTH:(gi + 1) * POOL_GROUP_WIDTH], wpool_ref[gi],
                  preferred_element_type=F32) for gi in range(len(POOL_WINDOWS))]
    y = jnp.concatenate(ys, axis=1) * ps_ref[...]

    m_scr[...] += jnp.dot(y.astype(BF16), wout_ref[ATTN_WIDTH:, :], preferred_element_type=F32)
    _gated_residual_into(o_ref, x_ref, m_scr, gpost_ref, gate_ref)


def _mixer_out(x, a, u3, hist, mod, g_post, w_out_bf, w_pool_bf, pool_scale, *,
               nb, t, seq_tiles, decode, pos_base=0, layer=0):
    rows_total = x.shape[0]
    tm = nb * t
    nt = rows_total // tm
    out_specs = pl.BlockSpec((tm, D_MODEL), lambda i: (i, 0))
    out_shape = jax.ShapeDtypeStruct((rows_total, D_MODEL), F32)
    if decode:
        a_spec = pl.BlockSpec((tm * N_HEADS, V_DIM), lambda i: (i, 0))
        u_spec = pl.BlockSpec((nb, t, POOL_WIDTH), lambda i: (i, 0, 0))
        hist_spec = pl.BlockSpec((None, nb, POOL_HIST, POOL_WIDTH), lambda i: (layer, i, 0, 0))
        gate_spec = pl.BlockSpec((nb, D_MODEL), lambda i: (i, 2))
        out_specs = [out_specs, pl.BlockSpec((nb, POOL_HIST, POOL_WIDTH), lambda i: (i, 0, 0))]
        out_shape = [out_shape, jax.ShapeDtypeStruct((nt * nb, POOL_HIST, POOL_WIDTH), F32)]
    else:
        a_spec = pl.BlockSpec((tm, ATTN_WIDTH), lambda i: (i, 0))
        u_spec = pl.BlockSpec((1, t, POOL_WIDTH), lambda i: (i // seq_tiles, i % seq_tiles, 0))
        hpt = t // HIST_PAD
        hist_spec = pl.BlockSpec(
            (1, HIST_PAD, POOL_WIDTH),
            lambda i: (i // seq_tiles, jnp.maximum((i % seq_tiles) * hpt - 1, 0), 0))
        gate_spec = pl.BlockSpec((None, 1, D_MODEL), lambda i: (i // seq_tiles, 0, 2))
    return pl.pallas_call(
        functools.partial(_mixer_out_kernel, nb=nb, t=t, seq_tiles=seq_tiles, decode=decode,
                          pos_base=pos_base),
        grid=(nt,),
        in_specs=[
            pl.BlockSpec((tm, D_MODEL), lambda i: (i, 0)),
            a_spec, u_spec, hist_spec, gate_spec,
            pl.BlockSpec((1, D_MODEL), lambda i: (0, 0)),
            pl.BlockSpec((D_MODEL, D_MODEL), lambda i: (0, 0)),
            pl.BlockSpec((len(POOL_WINDOWS), POOL_GROUP_WIDTH, POOL_GROUP_WIDTH), lambda i: (0, 0, 0)),
            pl.BlockSpec((1, POOL_WIDTH), lambda i: (0, 0)),
        ],
        out_specs=out_specs,
        out_shape=out_shape,
        scratch_shapes=[pltpu.VMEM((nb, HIST_PAD + t, POOL_WIDTH), F32),
                        pltpu.VMEM((tm, D_MODEL), F32), pltpu.VMEM((tm, POOL_WIDTH), BF16)],
        compiler_params=_cparams(("parallel",)),
        name="mixer_out_decode" if decode else "mixer_out_prompt",
    )(x, a, u3, hist, mod, g_post.reshape(1, -1), w_out_bf, w_pool_bf, pool_scale.reshape(1, -1))


def _ffn_kernel(x_ref, shift_ref, scale_ref, gate_ref, gpre_ref, gpost_ref, w1_ref, w2_ref,
                *refs, n_f, decode):
    i = pl.program_id(0)
    j = pl.program_id(1)
    if decode:
        o_ref, w1bf_ref, w2bf_ref, h_scr, acc_scr = refs
        w1bf_ref[...] = w1_ref[...].astype(BF16)
        w2bf_ref[...] = w2_ref[...].astype(BF16)
        w1_ref, w2_ref = w1bf_ref, w2bf_ref

        @pl.when(j == 0)
        def _():
            _modulate_into(h_scr, x_ref, gpre_ref, shift_ref, scale_ref)
    else:
        xn_ref, shiftn_ref, scalen_ref, o_ref, h_scr, hb_scr, acc_scr = refs

        @pl.when((i == 0) & (j == 0))
        def _():
            _modulate_into(h_scr, x_ref, gpre_ref, shift_ref, scale_ref)

    @pl.when(j == 0)
    def _():
        acc_scr[...] = jnp.zeros(acc_scr.shape, F32)

    def step(h_cur, h_next):
        t = jnp.dot(h_cur[...], w1_ref[...], preferred_element_type=F32)
        t = jnp.square(jnp.maximum(t, 0.0))
        acc_scr[...] += jnp.dot(t.astype(BF16), w2_ref[...], preferred_element_type=F32)
        if h_next is not None:
            per_step = x_ref.shape[0] // n_f
            for r in range(0, per_step, ROW_CHUNK):
                rows = pl.ds(pl.multiple_of(j * per_step + r, ROW_CHUNK), ROW_CHUNK)
                h = (_rms(xn_ref[rows, :], gpre_ref[...]) * (1.0 + scalen_ref[...]) + shiftn_ref[...])
                h_next[rows, :] = h.astype(BF16)

    if decode:
        step(h_scr, None)
    else:
        even = lax.rem(i, 2) == 0
        pl.when(even)(functools.partial(step, h_scr, hb_scr))
        pl.when(jnp.logical_not(even))(functools.partial(step, hb_scr, h_scr))

    @pl.when(j == n_f - 1)
    def _():
        _gated_residual_into(o_ref, x_ref, acc_scr, gpost_ref, gate_ref)


def _ffn(x, mod, g_pre, g_post, w1, w2, *, tm, tf, seq_tiles, decode):
    rows = x.shape[0]
    nt = rows // tm
    n_f = D_FF // tf
    row_mode = dict(pipeline_mode=pl.Buffered(1)) if nt == 1 else {}
    x_spec = pl.BlockSpec((tm, D_MODEL), lambda i, j: (i, 0), **row_mode)
    out_specs = pl.BlockSpec((tm, D_MODEL), lambda i, j: (i, 0), **row_mode)
    out_shape = jax.ShapeDtypeStruct((rows, D_MODEL), F32)
    h_buf = pltpu.VMEM((tm, D_MODEL), BF16)
    scratch = ([h_buf] if decode else [h_buf, h_buf]) + [pltpu.VMEM((tm, D_MODEL), F32)]
    extra_in, extra_args = [], []
    if decode:
        assert nt == 1
        mod_spec = lambda c: pl.BlockSpec((tm // SUBLANES, D_MODEL), lambda i, j: (i, c))
        out_specs = [out_specs, pl.BlockSpec((D_MODEL, tf), lambda i, j: (0, j)),
                     pl.BlockSpec((tf, D_MODEL), lambda i, j: (j, 0))]
        out_shape = [out_shape, jax.ShapeDtypeStruct((D_MODEL, D_FF), BF16),
                     jax.ShapeDtypeStruct((D_FF, D_MODEL), BF16)]
    else:
        assert tm % (n_f * ROW_CHUNK) == 0
        mod_spec = lambda c: pl.BlockSpec((None, 1, D_MODEL), lambda i, j: (i // seq_tiles, 0, c))
        nxt = lambda i: jnp.minimum(i + 1, nt - 1)
        mod_next = lambda c: pl.BlockSpec((None, 1, D_MODEL), lambda i, j: (nxt(i) // seq_tiles, 0, c))
        extra_in = [pl.BlockSpec((tm, D_MODEL), lambda i, j: (nxt(i), 0)), mod_next(3), mod_next(4)]
        extra_args = [x, mod, mod]
    vec_spec = pl.BlockSpec((1, D_MODEL), lambda i, j: (0, 0))
    return pl.pallas_call(
        functools.partial(_ffn_kernel, n_f=n_f, decode=decode),
        grid=(nt, n_f),
        in_specs=[
            x_spec,
            mod_spec(3), mod_spec(4), mod_spec(5),
            vec_spec, vec_spec,
            pl.BlockSpec((D_MODEL, tf), lambda i, j: (0, j)),
            pl.BlockSpec((tf, D_MODEL), lambda i, j: (j, 0)),
        ] + extra_in,
        out_specs=out_specs,
        out_shape=out_shape,
        scratch_shapes=scratch,
        compiler_params=_cparams(("arbitrary", "arbitrary")),
        name="ffn_decode" if decode else "ffn_prompt",
    )(x, mod, mod, mod, g_pre.reshape(1, -1), g_post.reshape(1, -1), w1, w2, *extra_args)


def kernel(x_prompt, x_sample, cache_k, cache_v, state_pool, page_table, c_prompt, c_sample,
           w_ada, b_ada, g_pre_mix, w_in, lambda_q1, lambda_k1, lambda_q2, lambda_k2, g_subln,
           w_pool, pool_scale, w_out, g_post_mix, g_pre_ffn, w_ff1, w_ff2, g_post_ffn):
    batch, seq, _ = x_prompt.shape
    dec_batch, dec_seq, _ = x_sample.shape
    depth = w_in.shape[0]
    past_len = page_table.shape[1] * PAGE_SIZE
    assert dec_seq == SUBLANES and state_pool.shape[2] == POOL_HIST

    tm_p = 512
    seq_tiles = seq // tm_p
    rows_s = dec_batch * dec_seq
    tables_p = _rope_tables(np.arange(seq))
    tables_s = _rope_tables(past_len + (np.arange(rows_s) % dec_seq))

    n_c = batch + dec_batch
    c_all = jnp.concatenate(
        [c_sample, c_prompt, jnp.zeros((-n_c % SUBLANES, D_MODEL), F32)], axis=0)

    xp = x_prompt.reshape(batch * seq, D_MODEL)
    xs = x_sample.reshape(rows_s, D_MODEL)
    outs = [[] for _ in range(6)]
    for l in range(depth):
        lam_init = 0.8 - 0.6 * math.exp(-0.3 * l)
        lams = [v[l].reshape(1, HEAD_DIM) for v in (lambda_q1, lambda_k1, lambda_q2, lambda_k2)]
        w_out_bf = w_out[l].astype(BF16)
        w_pool_bf = w_pool[l].astype(BF16)

        m_all = _ada(c_all, w_ada[l], b_ada[l])
        mod_p = m_all[dec_batch:n_c].reshape(batch, 1, N_MOD * D_MODEL)
        mod_s = m_all

        q4s, k4s, v4s, us, w_in_bf = _mixer_in(
            xs, mod_s, g_pre_mix[l], w_in[l], tables_s, tm=tm_p, seq_tiles=1, decode=True)
        a4s = _decode(q4s, k4s, v4s, cache_k, cache_v, l, page_table, lams, g_subln[l],
                      dec_batch=dec_batch, dec_seq=dec_seq, lam_init=lam_init)
        us3 = us.reshape(dec_batch, dec_seq, POOL_WIDTH)
        xs, pool_s = _mixer_out(xs, a4s, us3, state_pool, mod_s, g_post_mix[l], w_out_bf, w_pool_bf,
                                pool_scale[l], nb=tm_p // (2 * dec_seq), t=dec_seq, seq_tiles=1,
                                decode=True, pos_base=past_len, layer=l)
        xs, w1_bf, w2_bf = _ffn(xs, mod_s, g_pre_ffn[l], g_post_ffn[l], w_ff1[l], w_ff2[l],
                                tm=rows_s, tf=512, seq_tiles=1, decode=True)
        outs[3].append(k4s.reshape(dec_batch, dec_seq, N_HEADS, V_DIM))
        outs[4].append(v4s.reshape(dec_batch, dec_seq, N_HEADS, V_DIM))
        outs[5].append(pool_s)

        q_bf, k4, k_bf, v4, v_bf, u = _mixer_in(
            xp, mod_p, g_pre_mix[l], w_in_bf, tables_p, tm=tm_p, seq_tiles=seq_tiles, decode=False)
        a_bf = _flash(q_bf, k_bf, v_bf, lams, g_subln[l], batch=batch, seq=seq, lam_init=lam_init)
        u3 = u.reshape(batch, seq, POOL_WIDTH)
        xp = _mixer_out(xp, a_bf, u3, u3, mod_p, g_post_mix[l], w_out_bf, w_pool_bf, pool_scale[l],
                        nb=1, t=tm_p, seq_tiles=seq_tiles, decode=False)
        xp = _ffn(xp, mod_p, g_pre_ffn[l], g_post_ffn[l], w1_bf, w2_bf,
                  tm=tm_p, tf=1024, seq_tiles=seq_tiles, decode=False)
        outs[0].append(k4.reshape(batch, seq, N_HEADS, V_DIM))
        outs[1].append(v4.reshape(batch, seq, N_HEADS, V_DIM))
        outs[2].append(u3[:, seq - POOL_HIST:])

    kp, vp, sp, ks, vs, ss = (jnp.stack(o) for o in outs)
    return (xp.reshape(batch, seq, D_MODEL), xs.reshape(dec_batch, dec_seq, D_MODEL),
            kp, vp, sp, ks, vs, ss)
```

```python
import functools
import math

import jax
import jax.numpy as jnp
import numpy as np
from jax import lax
from jax.experimental import pallas as pl
from jax.experimental.pallas import tpu as pltpu

F32 = jnp.float32
BF16 = jnp.bfloat16

D_MODEL = 2048
N_HEADS = 8
HEAD_DIM = 64
V_DIM = 2 * HEAD_DIM
ATTN_WIDTH = N_HEADS * V_DIM
POOL_WIDTH = 1024
IN_WIDTH = 3 * ATTN_WIDTH + POOL_WIDTH
ROT_DIM = HEAD_DIM // 4
ROPE_THETA = 500000.0
POOL_WINDOWS = (2, 4, 8, 16)
POOL_GROUP_WIDTH = POOL_WIDTH // len(POOL_WINDOWS)
POOL_HIST = max(POOL_WINDOWS) - 1
HIST_PAD = 16
D_FF = 4 * D_MODEL
N_MOD = 6
PAGE_SIZE = 128
EPS = 1e-6
NEG_BIG = -1e30
LOG2_E = math.log2(math.e)

LANES = 128
SUBLANES = 8
BF16_ROWS = 16
VMEM_LIMIT = 60 * 1024 * 1024
RING_SLOTS = 3
ROW_CHUNK = 16
POOL_ROWS = 64
KV_CHUNK = 512


def _cparams(sem):
    return pltpu.CompilerParams(dimension_semantics=sem, vmem_limit_bytes=VMEM_LIMIT)


def _rms(x, g):
    ms = jnp.mean(x * x, axis=-1, keepdims=True)
    return x * lax.rsqrt(ms + EPS) * g


def _mod_rows(ref, r0, nrows, tile_rows):
    n, d = ref.shape
    if n == 1:
        return ref[...]
    per = tile_rows // n
    m = ref[r0 // per:(r0 + nrows) // per, :]
    return jnp.broadcast_to(m[:, None, :], (nrows // per, per, d)).reshape(nrows, d)


def _modulate_into(h_ref, x_ref, g_ref, shift_ref, scale_ref):
    tm = x_ref.shape[0]
    for r in range(0, tm, ROW_CHUNK):
        h = (_rms(x_ref[r:r + ROW_CHUNK, :], g_ref[...])
             * (1.0 + _mod_rows(scale_ref, r, ROW_CHUNK, tm)) + _mod_rows(shift_ref, r, ROW_CHUNK, tm))
        h_ref[r:r + ROW_CHUNK, :] = h.astype(BF16)


def _gated_residual_into(o_ref, x_ref, m_ref, g_ref, gate_ref):
    tm = x_ref.shape[0]
    for r in range(0, tm, ROW_CHUNK):
        rows = slice(r, r + ROW_CHUNK)
        o_ref[rows, :] = x_ref[rows, :] + _mod_rows(gate_ref, r, ROW_CHUNK, tm) * _rms(m_ref[rows, :], g_ref[...])


def _ada_kernel(c_ref, w_ref, b_ref, o_ref):
    c = c_ref[...]
    s = c * (1.0 / (1.0 + jnp.exp(-c)))
    o_ref[...] = jnp.dot(s.astype(BF16), w_ref[...].astype(BF16),
                         preferred_element_type=F32) + b_ref[...]


def _ada(c_all, w_ada, b_ada):
    rows = c_all.shape[0]
    tn = 2048
    return pl.pallas_call(
        _ada_kernel,
        grid=(N_MOD * D_MODEL // tn,),
        in_specs=[
            pl.BlockSpec((rows, D_MODEL), lambda j: (0, 0)),
            pl.BlockSpec((D_MODEL, tn), lambda j: (0, j)),
            pl.BlockSpec((1, tn), lambda j: (0, j)),
        ],
        out_specs=pl.BlockSpec((rows, tn), lambda j: (0, j)),
        out_shape=jax.ShapeDtypeStruct((rows, N_MOD * D_MODEL), F32),
        compiler_params=_cparams(("arbitrary",)),
        name="ada",
    )(c_all, w_ada, b_ada.reshape(1, -1))


def _rope(z, cos_ref, sa_ref, sb_ref):
    return (z * cos_ref[...]
            + pltpu.roll(z, LANES - ROT_DIM // 2, 1) * sa_ref[...]
            + pltpu.roll(z, ROT_DIM // 2, 1) * sb_ref[...])


def _mixer_in_kernel(x_ref, shift_ref, scale_ref, g_ref, w_ref, cos_ref, sa_ref, sb_ref,
                     *refs, tm, decode):
    j = pl.program_id(1)
    if decode:
        q4_ref, k4_ref, v4_ref, u_ref, wbf_ref, h_scr, wc_scr = refs
        wc_scr[...] = w_ref[...].astype(BF16)

        @pl.when(pl.program_id(0) == 0)
        def _():
            wbf_ref[...] = wc_scr[...]
    else:
        qb_ref, k4_ref, kb_ref, v4_ref, vb_ref, u_ref, h_scr = refs

    def weights(sec, lo, hi):
        if decode:
            return wc_scr[:, lo:hi]
        return w_ref[:, sec * ATTN_WIDTH + lo:sec * ATTN_WIDTH + hi]

    def heads(sec):
        for hp in range(N_HEADS // 2):
            z2 = jnp.dot(h_scr[...], weights(sec, 2 * hp * V_DIM, 2 * (hp + 1) * V_DIM),
                         preferred_element_type=F32)
            for h in (2 * hp, 2 * hp + 1):
                yield h, slice(h * V_DIM, (h + 1) * V_DIM), z2[:, (h % 2) * V_DIM:(h % 2 + 1) * V_DIM]

    def section(sec):
        if sec == 0:
            for h, cols, zh in heads(sec):
                q = _rope(zh, cos_ref, sa_ref, sb_ref) * (HEAD_DIM ** -0.5)
                if decode:
                    q4_ref[pl.ds(h, tm, stride=N_HEADS), :] = q
                else:
                    qb_ref[:, cols] = (q * LOG2_E).astype(BF16)
        elif sec == 1:
            for h, cols, zh in heads(sec):
                k = _rope(zh, cos_ref, sa_ref, sb_ref)
                k4_ref[pl.ds(h, tm, stride=N_HEADS), :] = k
                if not decode:
                    kb_ref[:, cols] = k.astype(BF16)
        elif sec == 2:
            for h, cols, zh in heads(sec):
                v4_ref[pl.ds(h, tm, stride=N_HEADS), :] = zh
                if not decode:
                    vb_ref[:, cols] = zh.astype(BF16)
        else:
            u_ref[...] = jnp.dot(h_scr[...], weights(sec, 0, POOL_WIDTH), preferred_element_type=F32)

    n_sec = IN_WIDTH // ATTN_WIDTH
    if decode:
        pl.when(j == 0)(functools.partial(_modulate_into, h_scr, x_ref, g_ref, shift_ref, scale_ref))
        for sec in range(n_sec):
            pl.when(j == sec)(functools.partial(section, sec))
    else:
        _modulate_into(h_scr, x_ref, g_ref, shift_ref, scale_ref)
        for sec in range(n_sec):
            section(sec)


def _mixer_in(x, mod, g_pre, w_in, tables, *, tm, seq_tiles, decode):
    rows = x.shape[0]
    nt = rows // tm
    if decode:
        mod_spec = lambda c: pl.BlockSpec((tm // SUBLANES, D_MODEL), lambda i, j: (i, c))
        tab_spec = pl.BlockSpec((tm, LANES), lambda i, j: (i, 0))
    else:
        mod_spec = lambda c: pl.BlockSpec((None, 1, D_MODEL), lambda i, j: (i // seq_tiles, 0, c))
        tab_spec = pl.BlockSpec((tm, LANES), lambda i, j: (i % seq_tiles, 0))
    row_bf = pl.BlockSpec((tm, ATTN_WIDTH), lambda i, j: (i, 0))
    row4 = pl.BlockSpec((tm * N_HEADS, V_DIM), lambda i, j: (i, 0))
    sd_bf = jax.ShapeDtypeStruct((rows, ATTN_WIDTH), BF16)
    sd4 = jax.ShapeDtypeStruct((rows * N_HEADS, V_DIM), F32)
    sd_u = jax.ShapeDtypeStruct((rows, POOL_WIDTH), F32)
    n_sec = IN_WIDTH // ATTN_WIDTH
    scratch = [pltpu.VMEM((tm, D_MODEL), BF16)]
    if decode:
        w_out_spec = pl.BlockSpec((D_MODEL, ATTN_WIDTH),
                                  lambda i, j: (0, jnp.where(i == 0, j, n_sec - 1)))
        out_specs = [row4, row4, row4, row_bf, w_out_spec]
        out_shape = [sd4, sd4, sd4, sd_u, jax.ShapeDtypeStruct((D_MODEL, IN_WIDTH), BF16)]
        scratch.append(pltpu.VMEM((D_MODEL, ATTN_WIDTH), BF16))
        steps = n_sec
        w_spec = pl.BlockSpec((D_MODEL, ATTN_WIDTH), lambda i, j: (0, j))
    else:
        out_specs = [row_bf, row4, row_bf, row4, row_bf, row_bf]
        out_shape = [sd_bf, sd4, sd_bf, sd4, sd_bf, sd_u]
        steps = 1
        w_spec = pl.BlockSpec((D_MODEL, IN_WIDTH), lambda i, j: (0, 0), pipeline_mode=pl.Buffered(1))
    return pl.pallas_call(
        functools.partial(_mixer_in_kernel, tm=tm, decode=decode),
        grid=(nt, steps),
        in_specs=[
            pl.BlockSpec((tm, D_MODEL), lambda i, j: (i, 0)),
            mod_spec(0), mod_spec(1),
            pl.BlockSpec((1, D_MODEL), lambda i, j: (0, 0)),
            w_spec,
            tab_spec, tab_spec, tab_spec,
        ],
        out_specs=out_specs,
        out_shape=out_shape,
        scratch_shapes=scratch,
        compiler_params=_cparams(("arbitrary", "arbitrary")),
        name="mixer_in_decode" if decode else "mixer_in_prompt",
    )(x, mod, mod, g_pre.reshape(1, -1), w_in, *tables)


def _rope_tables(pos):
    half = ROT_DIM // 2
    inv = 1.0 / (ROPE_THETA ** (np.arange(0, ROT_DIM, 2, dtype=np.float64) / ROT_DIM))
    ang = np.asarray(pos, np.float64)[:, None] * inv[None, :]
    cos, sin = np.cos(ang), np.sin(ang)
    lane = np.arange(LANES) % HEAD_DIM
    first = (lane < half)[None, :]
    second = ((lane >= half) & (lane < ROT_DIM))[None, :]
    cos_l = cos[:, lane % half]
    sin_l = sin[:, lane % half]
    c = np.where(first | second, cos_l, 1.0)
    sa = np.where(first, -sin_l, 0.0)
    sb = np.where(second, sin_l, 0.0)
    return tuple(jnp.asarray(t, F32) for t in (c, sa, sb))


def _lam(lq1, lk1, lq2, lk2, lam_init):
    a = jnp.sum(lq1[...] * lk1[...], axis=-1, keepdims=True)
    b = jnp.sum(lq2[...] * lk2[...], axis=-1, keepdims=True)
    return jnp.exp(a) - jnp.exp(b) + lam_init


def _flash_kernel(q_ref, k_ref, v_ref, lq1, lk1, lq2, lk2, g_ref, o_ref,
                  vt_scr, s_scr, *, tq, nq, lam_init):
    seq = nq * tq
    vq = 2 * tq
    grp = tq // SUBLANES
    vt_scr[0:V_DIM, :] = v_ref[...].astype(F32).T.astype(BF16)
    vt_scr[V_DIM:, :] = jnp.ones((BF16_ROWS, seq), BF16)
    lam = _lam(lq1, lk1, lq2, lk2, lam_init)
    lane = lax.broadcasted_iota(jnp.int32, (tq, V_DIM), 1)
    kv = lax.broadcasted_iota(jnp.int32, (tq, vq), 0)
    qp = lax.broadcasted_iota(jnp.int32, (tq, vq), 1) & (tq - 1)
    causal = kv <= qp

    def stacked_queries(ii):
        q = q_ref[ii * tq:(ii + 1) * tq, :].astype(F32)
        return jnp.concatenate([jnp.where(lane < HEAD_DIM, q, 0.0),
                                jnp.where(lane >= HEAD_DIM, q, 0.0)], axis=0).astype(BF16)

    def kv_chunks(ii):
        out, r = [], 0
        while r < ii * tq:
            size = min(KV_CHUNK, ii * tq - r)
            out.append((r, size, False))
            r += size
        return out + [(ii * tq, tq, True)]

    def score_pass(ii):
        qq = stacked_queries(ii)
        mx = jnp.full((SUBLANES, vq), NEG_BIG, F32)
        for r, size, masked in kv_chunks(ii):
            s = lax.dot_general(k_ref[r:r + size, :], qq, (((1,), (1,)), ((), ())),
                                preferred_element_type=F32)
            if masked:
                s = jnp.where(causal, s, NEG_BIG)
            s_scr[ii % 2, r:r + size, :] = s
            mx = jnp.maximum(mx, jnp.max(s.reshape(size // SUBLANES, SUBLANES, vq), axis=0))
        return jnp.max(mx, axis=0, keepdims=True)

    def value_pass(ii, m):
        o_t = jnp.zeros((V_DIM + BF16_ROWS, vq), F32)
        for r, size, _ in kv_chunks(ii):
            p = jnp.exp2(s_scr[ii % 2, r:r + size, :] - m)
            o_t = o_t + jnp.dot(vt_scr[:, r:r + size], p.astype(BF16), preferred_element_type=F32)
        return o_t

    m = score_pass(0)
    for ii in range(nq):
        m_next = score_pass(ii + 1) if ii + 1 < nq else None
        o_t = value_pass(ii, m)
        m = m_next
        n_t = o_t[:V_DIM] / o_t[V_DIM:V_DIM + 1]
        out = (n_t[:, :tq] - lam * n_t[:, tq:]).T
        o_ref[ii * tq:(ii + 1) * tq, :] = (_rms(out, g_ref[...]) * (1.0 - lam_init)).astype(BF16)


def _flash(q_bf, k_bf, v_bf, lams, g_subln, *, batch, seq, lam_init, tq=256):
    nq = seq // tq
    lam_spec = pl.BlockSpec((1, HEAD_DIM), lambda b, h: (0, 0))
    head_spec = pl.BlockSpec((seq, V_DIM), lambda b, h: (b, h))
    return pl.pallas_call(
        functools.partial(_flash_kernel, tq=tq, nq=nq, lam_init=lam_init),
        grid=(batch, N_HEADS),
        in_specs=[head_spec, head_spec, head_spec, lam_spec, lam_spec, lam_spec, lam_spec,
                  pl.BlockSpec((1, V_DIM), lambda b, h: (0, 0))],
        out_specs=head_spec,
        out_shape=jax.ShapeDtypeStruct((batch * seq, ATTN_WIDTH), BF16),
        scratch_shapes=[pltpu.VMEM((V_DIM + BF16_ROWS, seq), BF16),
                        pltpu.VMEM((2, seq, 2 * tq), F32)],
        compiler_params=_cparams(("parallel", "parallel")),
        name="flash_prompt",
    )(q_bf, k_bf, v_bf, *lams, g_subln.reshape(1, -1))


def _decode_kernel(pt_ref, q_ref, kn_ref, vn_ref, ck_hbm, cv_hbm, lq1, lk1, lq2, lk2, g_ref, o_ref,
                   kbuf, vbuf, ksem, vsem, m_scr, l_scr, acc_scr, *, layer, n_pages, n_groups,
                   n_steps, dec_seq, lam_init):
    n_seqs = 1
    g = pl.program_id(1)
    t = pl.program_id(0) * n_groups + g
    qh = dec_seq * N_HEADS
    page_rows = PAGE_SIZE * N_HEADS

    def page_copies(step, slot):
        cps = []
        for p in range(n_pages):
            page = pt_ref[step * n_pages + p]
            cps.append(pltpu.make_async_copy(ck_hbm.at[layer, page], kbuf.at[slot, p], ksem.at[slot]))
            cps.append(pltpu.make_async_copy(cv_hbm.at[layer, page], vbuf.at[slot, p], vsem.at[slot]))
        return cps

    ahead = RING_SLOTS - 1

    @pl.when(t == 0)
    def _():
        for s in range(min(ahead, n_steps)):
            for cp in page_copies(s, s):
                cp.start()

    @pl.when(t + ahead < n_steps)
    def _():
        for cp in page_copies(t + ahead, lax.rem(t + ahead, RING_SLOTS)):
            cp.start()

    slot = lax.rem(t, RING_SLOTS)
    for cp in page_copies(t, slot):
        cp.wait()
    k_refs = [kbuf.at[slot, p] for p in range(n_pages)]
    v_refs = [vbuf.at[slot, p] for p in range(n_pages)]

    @pl.when(g == 0)
    def _():
        m_scr[...] = jnp.full(m_scr.shape, NEG_BIG, F32)
        l_scr[...] = jnp.zeros(l_scr.shape, F32)
        acc_scr[...] = jnp.zeros(acc_scr.shape, F32)

    sub = lax.broadcasted_iota(jnp.int32, (SUBLANES, LANES), 0)
    lane = lax.broadcasted_iota(jnp.int32, (SUBLANES, LANES), 1)
    diag = (lane & (N_HEADS - 1)) == sub
    lane_q = lax.broadcasted_iota(jnp.int32, (qh, V_DIM), 1)
    lam = _lam(lq1, lk1, lq2, lk2, lam_init)

    def query_cols(b):
        r = q_ref[b * qh:(b + 1) * qh, :]
        return jnp.concatenate([jnp.where(lane_q < HEAD_DIM, r, 0.0),
                                jnp.where(lane_q >= HEAD_DIM, r, 0.0)], axis=0).astype(BF16)

    def update(b, s_list, v_list):
        m_prev = m_scr[b]
        m_new = m_prev
        for s in s_list:
            m_new = jnp.maximum(m_new, jnp.max(s, axis=0))
        alpha = jnp.exp(m_prev - m_new)
        m_eff = jnp.where(diag, m_new, -NEG_BIG)
        l_new = alpha * l_scr[b]
        pv = jnp.zeros((V_DIM, LANES), F32)
        for s, v in zip(s_list, v_list):
            p = jnp.exp(s - m_eff[None])
            l_new = l_new + jnp.sum(p, axis=0)
            pm = p.reshape(v.shape[0], LANES).astype(BF16)
            pv = pv + lax.dot_general(v.astype(BF16), pm, (((0,), (0,)), ((), ())),
                                      preferred_element_type=F32)
        alpha_row = jnp.sum(jnp.where(diag, alpha, 0.0), axis=0, keepdims=True)
        acc_scr[b] = acc_scr[b] * alpha_row + pv
        l_scr[b] = l_new
        m_scr[b] = m_new

    for b in range(n_seqs):
        qt = query_cols(b)
        pages = range(b * n_pages, (b + 1) * n_pages)
        s_list = [
            lax.dot_general(k_refs[p][...].reshape(page_rows, V_DIM).astype(BF16), qt,
                            (((1,), (1,)), ((), ())), preferred_element_type=F32
                            ).reshape(PAGE_SIZE, N_HEADS, LANES) for p in pages]
        update(b, s_list, [v_refs[p][...].reshape(page_rows, V_DIM) for p in pages])

    @pl.when(g == n_groups - 1)
    def _():
        for b in range(n_seqs):
            tok = slice(b * qh, (b + 1) * qh)
            s_new = lax.dot_general(kn_ref[tok, :].astype(BF16), query_cols(b),
                                    (((1,), (1,)), ((), ())), preferred_element_type=F32)
            s_new = s_new.reshape(dec_seq, N_HEADS, LANES)
            pos = lax.broadcasted_iota(jnp.int32, s_new.shape, 0)
            qidx = (lax.broadcasted_iota(jnp.int32, s_new.shape, 2) & (qh - 1)) >> 3
            update(b, [jnp.where(pos <= qidx, s_new, NEG_BIG)], [vn_ref[tok, :]])
            l_row = jnp.sum(jnp.where(diag, l_scr[b], 0.0), axis=0, keepdims=True)
            nt = (acc_scr[b] / l_row).T
            out = nt[:qh] - lam * nt[qh:]
            o_ref[tok, :] = _rms(out, g_ref[...]) * (1.0 - lam_init)


def _decode(q4, k4, v4, cache_k, cache_v, layer, page_table, lams, g_subln, *,
            dec_batch, dec_seq, lam_init, n_pages=8):
    pages_per_seq = page_table.shape[1]
    assert pages_per_seq % n_pages == 0
    n_groups = pages_per_seq // n_pages
    qh = dec_seq * N_HEADS
    tok_spec = pl.BlockSpec((qh, V_DIM), lambda b, g, pt: (b, 0))
    lam_spec = pl.BlockSpec((1, HEAD_DIM), lambda b, g, pt: (0, 0))
    hbm_spec = pl.BlockSpec(memory_space=pl.ANY)
    page_buf = pltpu.VMEM((RING_SLOTS, n_pages, PAGE_SIZE, N_HEADS, V_DIM), F32)
    grid_spec = pltpu.PrefetchScalarGridSpec(
        num_scalar_prefetch=1,
        grid=(dec_batch, n_groups),
        in_specs=[tok_spec, tok_spec, tok_spec, hbm_spec, hbm_spec]
        + [lam_spec] * 4 + [pl.BlockSpec((1, V_DIM), lambda b, g, pt: (0, 0))],
        out_specs=tok_spec,
        scratch_shapes=[page_buf, page_buf,
                        pltpu.SemaphoreType.DMA((RING_SLOTS,)), pltpu.SemaphoreType.DMA((RING_SLOTS,)),
                        pltpu.VMEM((1, SUBLANES, LANES), F32), pltpu.VMEM((1, SUBLANES, LANES), F32),
                        pltpu.VMEM((1, V_DIM, LANES), F32)],
    )
    return pl.pallas_call(
        functools.partial(_decode_kernel, layer=layer, n_pages=n_pages, n_groups=n_groups,
                          n_steps=dec_batch * n_groups, dec_seq=dec_seq, lam_init=lam_init),
        grid_spec=grid_spec,
        out_shape=jax.ShapeDtypeStruct((dec_batch * qh, V_DIM), F32),
        compiler_params=_cparams(("arbitrary", "arbitrary")),
        name="decode_attn",
    )(page_table.reshape(-1), q4, k4, v4, cache_k, cache_v, *lams, g_subln.reshape(1, -1))


def _mixer_out_kernel(x_ref, a_ref, u_ref, hist_ref, gate_ref, gpost_ref, wout_ref, wpool_ref,
                      ps_ref, *refs, nb, t, seq_tiles, decode, pos_base):
    if decode:
        o_ref, state_ref, ext_scr, m_scr, pool_scr = refs
    else:
        o_ref, ext_scr, m_scr, pool_scr = refs
    rows = nb * t
    if decode:
        a = jnp.concatenate([a_ref[pl.ds(h, rows, stride=N_HEADS), :] for h in range(N_HEADS)],
                            axis=1).astype(BF16)
    else:
        a = a_ref[...]
    m_scr[...] = jnp.dot(a, wout_ref[0:ATTN_WIDTH, :], preferred_element_type=F32)

    ext_scr[:, HIST_PAD:HIST_PAD + t, :] = u_ref[...]
    if decode:
        ext_scr[:, HIST_PAD - POOL_HIST:HIST_PAD, :] = hist_ref[...]
        state_ref[...] = ext_scr[:, HIST_PAD + t - POOL_HIST:HIST_PAD + t, :]
        base = pos_base
    else:
        first = (pl.program_id(0) % seq_tiles) == 0
        ext_scr[:, 0:HIST_PAD, :] = jnp.where(first, 0.0, hist_ref[...])
        base = (pl.program_id(0) % seq_tiles) * t

    tb = min(t, POOL_ROWS)
    bb = POOL_ROWS // tb
    for gi, w in enumerate(POOL_WINDOWS):
        cs = slice(gi * POOL_GROUP_WIDTH, (gi + 1) * POOL_GROUP_WIDTH)
        for b0 in range(0, nb, bb):
            for t0 in range(0, t, tb):
                lo = HIST_PAD + t0
                cur_g = ext_scr[b0:b0 + bb, lo:lo + tb, cs]
                win = cur_g
                for k in range(1, w):
                    win = win + ext_scr[b0:b0 + bb, lo - k:lo - k + tb, cs]
                pos = base + t0 + lax.broadcasted_iota(jnp.int32, (1, tb, POOL_GROUP_WIDTH), 1)
                cnt = jnp.minimum(w, pos + 1).astype(F32)
                r0 = b0 * t + t0
                pool_scr[r0:r0 + bb * tb, cs] = (win / cnt - cur_g).reshape(
                    bb * tb, POOL_GROUP_WIDTH).astype(BF16)
    ys = [jnp.dot(pool_scr[:, gi * POOL_GROUP_WIDTH:(gi + 1) * POOL_GROUP_WIDTH], wpool_ref[gi],
                  preferred_element_type=F32) for gi in range(len(POOL_WINDOWS))]
    y = jnp.concatenate(ys, axis=1) * ps_ref[...]

    m_scr[...] += jnp.dot(y.astype(BF16), wout_ref[ATTN_WIDTH:, :], preferred_element_type=F32)
    _gated_residual_into(o_ref, x_ref, m_scr, gpost_ref, gate_ref)


def _mixer_out(x, a, u3, hist, mod, g_post, w_out_bf, w_pool_bf, pool_scale, *,
               nb, t, seq_tiles, decode, pos_base=0, layer=0):
    rows_total = x.shape[0]
    tm = nb * t
    nt = rows_total // tm
    out_specs = pl.BlockSpec((tm, D_MODEL), lambda i: (i, 0))
    out_shape = jax.ShapeDtypeStruct((rows_total, D_MODEL), F32)
    if decode:
        a_spec = pl.BlockSpec((tm * N_HEADS, V_DIM), lambda i: (i, 0))
        u_spec = pl.BlockSpec((nb, t, POOL_WIDTH), lambda i: (i, 0, 0))
        hist_spec = pl.BlockSpec((None, nb, POOL_HIST, POOL_WIDTH), lambda i: (layer, i, 0, 0))
        gate_spec = pl.BlockSpec((nb, D_MODEL), lambda i: (i, 2))
        out_specs = [out_specs, pl.BlockSpec((nb, POOL_HIST, POOL_WIDTH), lambda i: (i, 0, 0))]
        out_shape = [out_shape, jax.ShapeDtypeStruct((nt * nb, POOL_HIST, POOL_WIDTH), F32)]
    else:
        a_spec = pl.BlockSpec((tm, ATTN_WIDTH), lambda i: (i, 0))
        u_spec = pl.BlockSpec((1, t, POOL_WIDTH), lambda i: (i // seq_tiles, i % seq_tiles, 0))
        hpt = t // HIST_PAD
        hist_spec = pl.BlockSpec(
            (1, HIST_PAD, POOL_WIDTH),
            lambda i: (i // seq_tiles, jnp.maximum((i % seq_tiles) * hpt - 1, 0), 0))
        gate_spec = pl.BlockSpec((None, 1, D_MODEL), lambda i: (i // seq_tiles, 0, 2))
    return pl.pallas_call(
        functools.partial(_mixer_out_kernel, nb=nb, t=t, seq_tiles=seq_tiles, decode=decode,
                          pos_base=pos_base),
        grid=(nt,),
        in_specs=[
            pl.BlockSpec((tm, D_MODEL), lambda i: (i, 0)),
            a_spec, u_spec, hist_spec, gate_spec,
            pl.BlockSpec((1, D_MODEL), lambda i: (0, 0)),
            pl.BlockSpec((D_MODEL, D_MODEL), lambda i: (0, 0)),
            pl.BlockSpec((len(POOL_WINDOWS), POOL_GROUP_WIDTH, POOL_GROUP_WIDTH), lambda i: (0, 0, 0)),
            pl.BlockSpec((1, POOL_WIDTH), lambda i: (0, 0)),
        ],
        out_specs=out_specs,
        out_shape=out_shape,
        scratch_shapes=[pltpu.VMEM((nb, HIST_PAD + t, POOL_WIDTH), F32),
                        pltpu.VMEM((tm, D_MODEL), F32), pltpu.VMEM((tm, POOL_WIDTH), BF16)],
        compiler_params=_cparams(("parallel",)),
        name="mixer_out_decode" if decode else "mixer_out_prompt",
    )(x, a, u3, hist, mod, g_post.reshape(1, -1), w_out_bf, w_pool_bf, pool_scale.reshape(1, -1))


def _ffn_kernel(x_ref, shift_ref, scale_ref, gate_ref, gpre_ref, gpost_ref, w1_ref, w2_ref,
                *refs, n_f, decode):
    i = pl.program_id(0)
    j = pl.program_id(1)
    if decode:
        o_ref, w1bf_ref, w2bf_ref, h_scr, acc_scr = refs
        w1bf_ref[...] = w1_ref[...].astype(BF16)
        w2bf_ref[...] = w2_ref[...].astype(BF16)
        w1_ref, w2_ref = w1bf_ref, w2bf_ref

        @pl.when(j == 0)
        def _():
            _modulate_into(h_scr, x_ref, gpre_ref, shift_ref, scale_ref)
    else:
        xn_ref, shiftn_ref, scalen_ref, o_ref, h_scr, hb_scr, acc_scr = refs

        @pl.when((i == 0) & (j == 0))
        def _():
            _modulate_into(h_scr, x_ref, gpre_ref, shift_ref, scale_ref)

    @pl.when(j == 0)
    def _():
        acc_scr[...] = jnp.zeros(acc_scr.shape, F32)

    def step(h_cur, h_next):
        t = jnp.dot(h_cur[...], w1_ref[...], preferred_element_type=F32)
        t = jnp.square(jnp.maximum(t, 0.0))
        acc_scr[...] += jnp.dot(t.astype(BF16), w2_ref[...], preferred_element_type=F32)
        if h_next is not None:
            per_step = x_ref.shape[0] // n_f
            for r in range(0, per_step, ROW_CHUNK):
                rows = pl.ds(pl.multiple_of(j * per_step + r, ROW_CHUNK), ROW_CHUNK)
                h = (_rms(xn_ref[rows, :], gpre_ref[...]) * (1.0 + scalen_ref[...]) + shiftn_ref[...])
                h_next[rows, :] = h.astype(BF16)

    if decode:
        step(h_scr, None)
    else:
        even = lax.rem(i, 2) == 0
        pl.when(even)(functools.partial(step, h_scr, hb_scr))
        pl.when(jnp.logical_not(even))(functools.partial(step, hb_scr, h_scr))

    @pl.when(j == n_f - 1)
    def _():
        _gated_residual_into(o_ref, x_ref, acc_scr, gpost_ref, gate_ref)


def _ffn(x, mod, g_pre, g_post, w1, w2, *, tm, tf, seq_tiles, decode):
    rows = x.shape[0]
    nt = rows // tm
    n_f = D_FF // tf
    row_mode = dict(pipeline_mode=pl.Buffered(1)) if nt == 1 else {}
    x_spec = pl.BlockSpec((tm, D_MODEL), lambda i, j: (i, 0), **row_mode)
    out_specs = pl.BlockSpec((tm, D_MODEL), lambda i, j: (i, 0), **row_mode)
    out_shape = jax.ShapeDtypeStruct((rows, D_MODEL), F32)
    h_buf = pltpu.VMEM((tm, D_MODEL), BF16)
    scratch = ([h_buf] if decode else [h_buf, h_buf]) + [pltpu.VMEM((tm, D_MODEL), F32)]
    extra_in, extra_args = [], []
    if decode:
        assert nt == 1
        mod_spec = lambda c: pl.BlockSpec((tm // SUBLANES, D_MODEL), lambda i, j: (i, c))
        out_specs = [out_specs, pl.BlockSpec((D_MODEL, tf), lambda i, j: (0, j)),
                     pl.BlockSpec((tf, D_MODEL), lambda i, j: (j, 0))]
        out_shape = [out_shape, jax.ShapeDtypeStruct((D_MODEL, D_FF), BF16),
                     jax.ShapeDtypeStruct((D_FF, D_MODEL), BF16)]
    else:
        assert tm % (n_f * ROW_CHUNK) == 0
        mod_spec = lambda c: pl.BlockSpec((None, 1, D_MODEL), lambda i, j: (i // seq_tiles, 0, c))
        nxt = lambda i: jnp.minimum(i + 1, nt - 1)
        mod_next = lambda c: pl.BlockSpec((None, 1, D_MODEL), lambda i, j: (nxt(i) // seq_tiles, 0, c))
        extra_in = [pl.BlockSpec((tm, D_MODEL), lambda i, j: (nxt(i), 0)), mod_next(3), mod_next(4)]
        extra_args = [x, mod, mod]
    vec_spec = pl.BlockSpec((1, D_MODEL), lambda i, j: (0, 0))
    return pl.pallas_call(
        functools.partial(_ffn_kernel, n_f=n_f, decode=decode),
        grid=(nt, n_f),
        in_specs=[
            x_spec,
            mod_spec(3), mod_spec(4), mod_spec(5),
            vec_spec, vec_spec,
            pl.BlockSpec((D_MODEL, tf), lambda i, j: (0, j)),
            pl.BlockSpec((tf, D_MODEL), lambda i, j: (j, 0)),
        ] + extra_in,
        out_specs=out_specs,
        out_shape=out_shape,
        scratch_shapes=scratch,
        compiler_params=_cparams(("arbitrary", "arbitrary")),
        name="ffn_decode" if decode else "ffn_prompt",
    )(x, mod, mod, mod, g_pre.reshape(1, -1), g_post.reshape(1, -1), w1, w2, *extra_args)


def kernel(x_prompt, x_sample, cache_k, cache_v, state_pool, page_table, c_prompt, c_sample,
           w_ada, b_ada, g_pre_mix, w_in, lambda_q1, lambda_k1, lambda_q2, lambda_k2, g_subln,
           w_pool, pool_scale, w_out, g_post_mix, g_pre_ffn, w_ff1, w_ff2, g_post_ffn):
    batch, seq, _ = x_prompt.shape
    dec_batch, dec_seq, _ = x_sample.shape
    depth = w_in.shape[0]
    past_len = page_table.shape[1] * PAGE_SIZE
    assert dec_seq == SUBLANES and state_pool.shape[2] == POOL_HIST

    tm_p = 512
    seq_tiles = seq // tm_p
    rows_s = dec_batch * dec_seq
    tables_p = _rope_tables(np.arange(seq))
    tables_s = _rope_tables(past_len + (np.arange(rows_s) % dec_seq))

    n_c = batch + dec_batch
    c_all = jnp.concatenate(
        [c_sample, c_prompt, jnp.zeros((-n_c % SUBLANES, D_MODEL), F32)], axis=0)

    xp = x_prompt.reshape(batch * seq, D_MODEL)
    xs = x_sample.reshape(rows_s, D_MODEL)
    outs = [[] for _ in range(6)]
    for l in range(depth):
        lam_init = 0.8 - 0.6 * math.exp(-0.3 * l)
        lams = [v[l].reshape(1, HEAD_DIM) for v in (lambda_q1, lambda_k1, lambda_q2, lambda_k2)]
        w_out_bf = w_out[l].astype(BF16)
        w_pool_bf = w_pool[l].astype(BF16)

        m_all = _ada(c_all, w_ada[l], b_ada[l])
        mod_p = m_all[dec_batch:n_c].reshape(batch, 1, N_MOD * D_MODEL)
        mod_s = m_all

        q4s, k4s, v4s, us, w_in_bf = _mixer_in(
            xs, mod_s, g_pre_mix[l], w_in[l], tables_s, tm=tm_p, seq_tiles=1, decode=True)
        a4s = _decode(q4s, k4s, v4s, cache_k, cache_v, l, page_table, lams, g_subln[l],
                      dec_batch=dec_batch, dec_seq=dec_seq, lam_init=lam_init)
        us3 = us.reshape(dec_batch, dec_seq, POOL_WIDTH)
        xs, pool_s = _mixer_out(xs, a4s, us3, state_pool, mod_s, g_post_mix[l], w_out_bf, w_pool_bf,
                                pool_scale[l], nb=tm_p // (2 * dec_seq), t=dec_seq, seq_tiles=1,
                                decode=True, pos_base=past_len, layer=l)
        xs, w1_bf, w2_bf = _ffn(xs, mod_s, g_pre_ffn[l], g_post_ffn[l], w_ff1[l], w_ff2[l],
                                tm=rows_s, tf=512, seq_tiles=1, decode=True)
        outs[3].append(k4s.reshape(dec_batch, dec_seq, N_HEADS, V_DIM))
        outs[4].append(v4s.reshape(dec_batch, dec_seq, N_HEADS, V_DIM))
        outs[5].append(pool_s)

        q_bf, k4, k_bf, v4, v_bf, u = _mixer_in(
            xp, mod_p, g_pre_mix[l], w_in_bf, tables_p, tm=tm_p, seq_tiles=seq_tiles, decode=False)
        a_bf = _flash(q_bf, k_bf, v_bf, lams, g_subln[l], batch=batch, seq=seq, lam_init=lam_init)
        u3 = u.reshape(batch, seq, POOL_WIDTH)
        xp = _mixer_out(xp, a_bf, u3, u3, mod_p, g_post_mix[l], w_out_bf, w_pool_bf, pool_scale[l],
                        nb=1, t=tm_p, seq_tiles=seq_tiles, decode=False)
        xp = _ffn(xp, mod_p, g_pre_ffn[l], g_post_ffn[l], w1_bf, w2_bf,
                  tm=tm_p, tf=1024, seq_tiles=seq_tiles, decode=False)
        outs[0].append(k4.reshape(batch, seq, N_HEADS, V_DIM))
        outs[1].append(v4.reshape(batch, seq, N_HEADS, V_DIM))
        outs[2].append(u3[:, seq - POOL_HIST:])

    kp, vp, sp, ks, vs, ss = (jnp.stack(o) for o in outs)
    return (xp.reshape(batch, seq, D_MODEL), xs.reshape(dec_batch, dec_seq, D_MODEL),
            kp, vp, sp, ks, vs, ss)
```

```python
import functools
import math

import jax
import jax.numpy as jnp
import numpy as np
from jax import lax
from jax.experimental import pallas as pl
from jax.experimental.pallas import tpu as pltpu

F32 = jnp.float32
BF16 = jnp.bfloat16

D_MODEL = 2048
N_HEADS = 8
HEAD_DIM = 64
V_DIM = 2 * HEAD_DIM
ATTN_WIDTH = N_HEADS * V_DIM
POOL_WIDTH = 1024
IN_WIDTH = 3 * ATTN_WIDTH + POOL_WIDTH
ROT_DIM = HEAD_DIM // 4
ROPE_THETA = 500000.0
POOL_WINDOWS = (2, 4, 8, 16)
POOL_GROUP_WIDTH = POOL_WIDTH // len(POOL_WINDOWS)
POOL_HIST = max(POOL_WINDOWS) - 1
HIST_PAD = 16
D_FF = 4 * D_MODEL
N_MOD = 6
PAGE_SIZE = 128
EPS = 1e-6
NEG_BIG = -1e30
LOG2_E = math.log2(math.e)

LANES = 128
SUBLANES = 8
BF16_ROWS = 16
VMEM_LIMIT = 60 * 1024 * 1024
RING_SLOTS = 3
ROW_CHUNK = 16
POOL_ROWS = 64
KV_CHUNK = 512


def _cparams(sem, **kwargs):
    return pltpu.CompilerParams(dimension_semantics=sem, vmem_limit_bytes=VMEM_LIMIT, **kwargs)


def _rms(x, g):
    ms = jnp.mean(x * x, axis=-1, keepdims=True)
    return x * lax.rsqrt(ms + EPS) * g


def _mod_rows(ref, r0, nrows, tile_rows):
    n, d = ref.shape
    if n == 1:
        return ref[...]
    per = tile_rows // n
    m = ref[r0 // per:(r0 + nrows) // per, :]
    return jnp.broadcast_to(m[:, None, :], (nrows // per, per, d)).reshape(nrows, d)


def _modulate_into(h_ref, x_ref, g_ref, shift_ref, scale_ref):
    tm = x_ref.shape[0]
    for r in range(0, tm, ROW_CHUNK):
        h = (_rms(x_ref[r:r + ROW_CHUNK, :], g_ref[...])
             * (1.0 + _mod_rows(scale_ref, r, ROW_CHUNK, tm)) + _mod_rows(shift_ref, r, ROW_CHUNK, tm))
        h_ref[r:r + ROW_CHUNK, :] = h.astype(BF16)


def _gated_residual_into(o_ref, x_ref, m_ref, g_ref, gate_ref):
    tm = x_ref.shape[0]
    for r in range(0, tm, ROW_CHUNK):
        rows = slice(r, r + ROW_CHUNK)
        o_ref[rows, :] = x_ref[rows, :] + _mod_rows(gate_ref, r, ROW_CHUNK, tm) * _rms(m_ref[rows, :], g_ref[...])


def _ada_kernel(c_ref, w_ref, b_ref, o_ref):
    c = c_ref[...]
    s = c * (1.0 / (1.0 + jnp.exp(-c)))
    o_ref[...] = jnp.dot(s.astype(BF16), w_ref[...].astype(BF16),
                         preferred_element_type=F32) + b_ref[...]


def _ada(c_all, w_ada, b_ada):
    rows = c_all.shape[0]
    tn = 1024
    return pl.pallas_call(
        _ada_kernel,
        grid=(N_MOD * D_MODEL // tn,),
        in_specs=[
            pl.BlockSpec((rows, D_MODEL), lambda j: (0, 0)),
            pl.BlockSpec((D_MODEL, tn), lambda j: (0, j)),
            pl.BlockSpec((1, tn), lambda j: (0, j)),
        ],
        out_specs=pl.BlockSpec((rows, tn), lambda j: (0, j)),
        out_shape=jax.ShapeDtypeStruct((rows, N_MOD * D_MODEL), F32),
        compiler_params=_cparams(("arbitrary",)),
        name="ada",
    )(c_all, w_ada, b_ada.reshape(1, -1))


def _rope(z, cos_ref, sa_ref, sb_ref):
    return (z * cos_ref[...]
            + pltpu.roll(z, LANES - ROT_DIM // 2, 1) * sa_ref[...]
            + pltpu.roll(z, ROT_DIM // 2, 1) * sb_ref[...])


def _mixer_in_kernel(x_ref, shift_ref, scale_ref, g_ref, w_ref, cos_ref, sa_ref, sb_ref,
                     *refs, tm, decode):
    j = pl.program_id(1)
    if decode:
        q4_ref, k4_ref, v4_ref, u_ref, wbf_ref, h_scr, wc_scr = refs
        wc_scr[...] = w_ref[...].astype(BF16)

        @pl.when(pl.program_id(0) == 0)
        def _():
            wbf_ref[...] = wc_scr[...]
    else:
        qb_ref, k4_ref, kb_ref, v4_ref, vb_ref, u_ref, h_scr = refs

    def weights(sec, lo, hi):
        if decode:
            return wc_scr[:, lo:hi]
        return w_ref[:, sec * ATTN_WIDTH + lo:sec * ATTN_WIDTH + hi]

    def heads(sec):
        for hp in range(N_HEADS // 2):
            z2 = jnp.dot(h_scr[...], weights(sec, 2 * hp * V_DIM, 2 * (hp + 1) * V_DIM),
                         preferred_element_type=F32)
            for h in (2 * hp, 2 * hp + 1):
                yield h, slice(h * V_DIM, (h + 1) * V_DIM), z2[:, (h % 2) * V_DIM:(h % 2 + 1) * V_DIM]

    def section(sec):
        if sec == 0:
            for h, cols, zh in heads(sec):
                q = _rope(zh, cos_ref, sa_ref, sb_ref) * (HEAD_DIM ** -0.5)
                if decode:
                    q4_ref[pl.ds(h, tm, stride=N_HEADS), :] = q
                else:
                    qb_ref[:, cols] = (q * LOG2_E).astype(BF16)
        elif sec == 1:
            for h, cols, zh in heads(sec):
                k = _rope(zh, cos_ref, sa_ref, sb_ref)
                k4_ref[pl.ds(h, tm, stride=N_HEADS), :] = k
                if not decode:
                    kb_ref[:, cols] = k.astype(BF16)
        elif sec == 2:
            for h, cols, zh in heads(sec):
                v4_ref[pl.ds(h, tm, stride=N_HEADS), :] = zh
                if not decode:
                    vb_ref[:, cols] = zh.astype(BF16)
        else:
            u_ref[...] = jnp.dot(h_scr[...], weights(sec, 0, POOL_WIDTH), preferred_element_type=F32)

    n_sec = IN_WIDTH // ATTN_WIDTH
    if decode:
        pl.when(j == 0)(functools.partial(_modulate_into, h_scr, x_ref, g_ref, shift_ref, scale_ref))
        for sec in range(n_sec):
            pl.when(j == sec)(functools.partial(section, sec))
    else:
        _modulate_into(h_scr, x_ref, g_ref, shift_ref, scale_ref)
        for sec in range(n_sec):
            section(sec)


def _mixer_in(x, mod, g_pre, w_in, tables, *, tm, seq_tiles, decode):
    rows = x.shape[0]
    nt = rows // tm
    if decode:
        mod_spec = lambda c: pl.BlockSpec((tm // SUBLANES, D_MODEL), lambda i, j: (i, c))
        tab_spec = pl.BlockSpec((tm, LANES), lambda i, j: (i, 0))
    else:
        mod_spec = lambda c: pl.BlockSpec((None, 1, D_MODEL), lambda i, j: (i // seq_tiles, 0, c))
        tab_spec = pl.BlockSpec((tm, LANES), lambda i, j: (i % seq_tiles, 0))
    row_bf = pl.BlockSpec((tm, ATTN_WIDTH), lambda i, j: (i, 0))
    row4 = pl.BlockSpec((tm * N_HEADS, V_DIM), lambda i, j: (i, 0))
    sd_bf = jax.ShapeDtypeStruct((rows, ATTN_WIDTH), BF16)
    sd4 = jax.ShapeDtypeStruct((rows * N_HEADS, V_DIM), F32)
    sd_u = jax.ShapeDtypeStruct((rows, POOL_WIDTH), F32)
    n_sec = IN_WIDTH // ATTN_WIDTH
    scratch = [pltpu.VMEM((tm, D_MODEL), BF16)]
    if decode:
        w_out_spec = pl.BlockSpec((D_MODEL, ATTN_WIDTH),
                                  lambda i, j: (0, jnp.where(i == 0, j, n_sec - 1)))
        out_specs = [row4, row4, row4, row_bf, w_out_spec]
        out_shape = [sd4, sd4, sd4, sd_u, jax.ShapeDtypeStruct((D_MODEL, IN_WIDTH), BF16)]
        scratch.append(pltpu.VMEM((D_MODEL, ATTN_WIDTH), BF16))
        steps = n_sec
        w_spec = pl.BlockSpec((D_MODEL, ATTN_WIDTH), lambda i, j: (0, j))
    else:
        out_specs = [row_bf, row4, row_bf, row4, row_bf, row_bf]
        out_shape = [sd_bf, sd4, sd_bf, sd4, sd_bf, sd_u]
        steps = 1
        w_spec = pl.BlockSpec((D_MODEL, IN_WIDTH), lambda i, j: (0, 0), pipeline_mode=pl.Buffered(1))
    return pl.pallas_call(
        functools.partial(_mixer_in_kernel, tm=tm, decode=decode),
        grid=(nt, steps),
        in_specs=[
            pl.BlockSpec((tm, D_MODEL), lambda i, j: (i, 0)),
            mod_spec(0), mod_spec(1),
            pl.BlockSpec((1, D_MODEL), lambda i, j: (0, 0)),
            w_spec,
            tab_spec, tab_spec, tab_spec,
        ],
        out_specs=out_specs,
        out_shape=out_shape,
        scratch_shapes=scratch,
        compiler_params=_cparams(("arbitrary", "arbitrary")),
        name="mixer_in_decode" if decode else "mixer_in_prompt",
    )(x, mod, mod, g_pre.reshape(1, -1), w_in, *tables)


def _rope_tables(pos):
    half = ROT_DIM // 2
    inv = 1.0 / (ROPE_THETA ** (np.arange(0, ROT_DIM, 2, dtype=np.float64) / ROT_DIM))
    ang = np.asarray(pos, np.float64)[:, None] * inv[None, :]
    cos, sin = np.cos(ang), np.sin(ang)
    lane = np.arange(LANES) % HEAD_DIM
    first = (lane < half)[None, :]
    second = ((lane >= half) & (lane < ROT_DIM))[None, :]
    cos_l = cos[:, lane % half]
    sin_l = sin[:, lane % half]
    c = np.where(first | second, cos_l, 1.0)
    sa = np.where(first, -sin_l, 0.0)
    sb = np.where(second, sin_l, 0.0)
    return tuple(jnp.asarray(t, F32) for t in (c, sa, sb))


def _lam(lq1, lk1, lq2, lk2, lam_init):
    a = jnp.sum(lq1[...] * lk1[...], axis=-1, keepdims=True)
    b = jnp.sum(lq2[...] * lk2[...], axis=-1, keepdims=True)
    return jnp.exp(a) - jnp.exp(b) + lam_init


def _flash_kernel(q_ref, k_ref, v_ref, lq1, lk1, lq2, lk2, g_ref, o_ref,
                  vt_scr, s_scr, *, tq, nq, lam_init):
    seq = nq * tq
    vq = 2 * tq
    vt_scr[0:V_DIM, :] = v_ref[...].astype(F32).T.astype(BF16)
    vt_scr[V_DIM:, :] = jnp.ones((BF16_ROWS, seq), BF16)
    lam = _lam(lq1, lk1, lq2, lk2, lam_init)
    lane = lax.broadcasted_iota(jnp.int32, (tq, V_DIM), 1)
    kv = lax.broadcasted_iota(jnp.int32, (tq, vq), 0)
    qp = lax.broadcasted_iota(jnp.int32, (tq, vq), 1) & (tq - 1)
    causal = kv <= qp

    def stacked_queries(ii):
        q = q_ref[ii * tq:(ii + 1) * tq, :].astype(F32)
        return jnp.concatenate([jnp.where(lane < HEAD_DIM, q, 0.0),
                                jnp.where(lane >= HEAD_DIM, q, 0.0)], axis=0).astype(BF16)

    def kv_chunks(ii):
        out, r = [], 0
        while r < ii * tq:
            size = min(KV_CHUNK, ii * tq - r)
            out.append((r, size, False))
            r += size
        return out + [(ii * tq, tq, True)]

    def score_pass(ii):
        qq = stacked_queries(ii)
        mx = jnp.full((SUBLANES, vq), NEG_BIG, F32)
        for r, size, masked in kv_chunks(ii):
            s = lax.dot_general(k_ref[r:r + size, :], qq, (((1,), (1,)), ((), ())),
                                preferred_element_type=F32)
            if masked:
                s = jnp.where(causal, s, NEG_BIG)
            s_scr[ii % 2, r:r + size, :] = s
            mx = jnp.maximum(mx, jnp.max(s.reshape(size // SUBLANES, SUBLANES, vq), axis=0))
        return jnp.max(mx, axis=0, keepdims=True)

    def value_pass(ii, m):
        o_t = jnp.zeros((V_DIM + BF16_ROWS, vq), F32)
        for r, size, _ in kv_chunks(ii):
            p = jnp.exp2(s_scr[ii % 2, r:r + size, :] - m)
            o_t = o_t + jnp.dot(vt_scr[:, r:r + size], p.astype(BF16), preferred_element_type=F32)
        return o_t

    m = score_pass(0)
    for ii in range(nq):
        m_next = score_pass(ii + 1) if ii + 1 < nq else None
        o_t = value_pass(ii, m)
        m = m_next
        n_t = o_t[:V_DIM] / o_t[V_DIM:V_DIM + 1]
        out = (n_t[:, :tq] - lam * n_t[:, tq:]).T
        o_ref[ii * tq:(ii + 1) * tq, :] = (_rms(out, g_ref[...]) * (1.0 - lam_init)).astype(BF16)


def _flash(q_bf, k_bf, v_bf, lams, g_subln, *, batch, seq, lam_init, tq=256):
    nq = seq // tq
    lam_spec = pl.BlockSpec((1, HEAD_DIM), lambda b, h: (0, 0))
    head_spec = pl.BlockSpec((seq, V_DIM), lambda b, h: (b, h))
    return pl.pallas_call(
        functools.partial(_flash_kernel, tq=tq, nq=nq, lam_init=lam_init),
        grid=(batch, N_HEADS),
        in_specs=[head_spec, head_spec, head_spec, lam_spec, lam_spec, lam_spec, lam_spec,
                  pl.BlockSpec((1, V_DIM), lambda b, h: (0, 0))],
        out_specs=head_spec,
        out_shape=jax.ShapeDtypeStruct((batch * seq, ATTN_WIDTH), BF16),
        scratch_shapes=[pltpu.VMEM((V_DIM + BF16_ROWS, seq), BF16),
                        pltpu.VMEM((2, seq, 2 * tq), F32)],
        compiler_params=_cparams(("parallel", "parallel")),
        name="flash_prompt",
    )(q_bf, k_bf, v_bf, *lams, g_subln.reshape(1, -1))


def _decode_kernel(pt_ref, q_ref, kn_ref, vn_ref, ck_hbm, cv_hbm, lq1, lk1, lq2, lk2, g_ref, o_ref,
                   kbuf, vbuf, ksem, vsem, m_scr, l_scr, acc_scr, *, layer, n_pages, n_groups,
                   n_steps, dec_seq, lam_init):
    g = pl.program_id(1)
    t = pl.program_id(0) * n_groups + g
    qh = dec_seq * N_HEADS
    page_rows = PAGE_SIZE * N_HEADS

    def page_copies(step, slot):
        cps = []
        for p in range(n_pages):
            page = pt_ref[step * n_pages + p]
            cps.append(pltpu.make_async_copy(ck_hbm.at[layer, page], kbuf.at[slot, p], ksem.at[slot]))
            cps.append(pltpu.make_async_copy(cv_hbm.at[layer, page], vbuf.at[slot, p], vsem.at[slot]))
        return cps

    ahead = RING_SLOTS - 1

    @pl.when(t == 0)
    def _():
        for s in range(min(ahead, n_steps)):
            for cp in page_copies(s, s):
                cp.start()

    @pl.when(t + ahead < n_steps)
    def _():
        for cp in page_copies(t + ahead, lax.rem(t + ahead, RING_SLOTS)):
            cp.start()

    slot = lax.rem(t, RING_SLOTS)
    for cp in page_copies(t, slot):
        cp.wait()

    @pl.when(g == 0)
    def _():
        m_scr[...] = jnp.full(m_scr.shape, NEG_BIG, F32)
        l_scr[...] = jnp.zeros(l_scr.shape, F32)
        acc_scr[...] = jnp.zeros(acc_scr.shape, F32)

    sub = lax.broadcasted_iota(jnp.int32, (SUBLANES, LANES), 0)
    lane = lax.broadcasted_iota(jnp.int32, (SUBLANES, LANES), 1)
    diag = (lane & (N_HEADS - 1)) == sub
    lane_q = lax.broadcasted_iota(jnp.int32, (qh, V_DIM), 1)
    lam = _lam(lq1, lk1, lq2, lk2, lam_init)

    r = q_ref[...]
    qt = jnp.concatenate([jnp.where(lane_q < HEAD_DIM, r, 0.0),
                          jnp.where(lane_q >= HEAD_DIM, r, 0.0)], axis=0).astype(BF16)

    def scores(k_rows):
        s = lax.dot_general(k_rows.astype(BF16), qt, (((1,), (1,)), ((), ())),
                            preferred_element_type=F32)
        return s.reshape(k_rows.shape[0] // N_HEADS, N_HEADS, LANES)

    def update(s_list, v_list):
        m_prev = m_scr[...]
        m_new = m_prev
        for s in s_list:
            m_new = jnp.maximum(m_new, jnp.max(s, axis=0))
        alpha = jnp.exp(m_prev - m_new)
        m_eff = jnp.where(diag, m_new, -NEG_BIG)
        l_new = alpha * l_scr[...]
        pv = jnp.zeros((V_DIM, LANES), F32)
        for s, v in zip(s_list, v_list):
            p = jnp.exp(s - m_eff[None])
            l_new = l_new + jnp.sum(p, axis=0)
            pm = p.reshape(v.shape[0], LANES).astype(BF16)
            pv = pv + lax.dot_general(v.astype(BF16), pm, (((0,), (0,)), ((), ())),
                                      preferred_element_type=F32)
        alpha_row = jnp.sum(jnp.where(diag, alpha, 0.0), axis=0, keepdims=True)
        acc_scr[...] = acc_scr[...] * alpha_row + pv
        l_scr[...] = l_new
        m_scr[...] = m_new

    update([scores(kbuf[slot, p].reshape(page_rows, V_DIM)) for p in range(n_pages)],
           [vbuf[slot, p].reshape(page_rows, V_DIM) for p in range(n_pages)])

    @pl.when(g == n_groups - 1)
    def _():
        s_new = scores(kn_ref[...])
        pos = lax.broadcasted_iota(jnp.int32, s_new.shape, 0)
        qidx = (lax.broadcasted_iota(jnp.int32, s_new.shape, 2) & (qh - 1)) >> 3
        update([jnp.where(pos <= qidx, s_new, NEG_BIG)], [vn_ref[...]])
        l_row = jnp.sum(jnp.where(diag, l_scr[...], 0.0), axis=0, keepdims=True)
        nt = (acc_scr[...] / l_row).T
        out = nt[:qh] - lam * nt[qh:]
        o_ref[...] = _rms(out, g_ref[...]) * (1.0 - lam_init)


def _decode(q4, k4, v4, cache_k, cache_v, layer, page_table, lams, g_subln, *,
            dec_batch, dec_seq, lam_init, n_pages=8):
    pages_per_seq = page_table.shape[1]
    assert pages_per_seq % n_pages == 0
    n_groups = pages_per_seq // n_pages
    qh = dec_seq * N_HEADS
    tok_spec = pl.BlockSpec((qh, V_DIM), lambda b, g, pt: (b, 0))
    lam_spec = pl.BlockSpec((1, HEAD_DIM), lambda b, g, pt: (0, 0))
    hbm_spec = pl.BlockSpec(memory_space=pl.ANY)
    page_buf = pltpu.VMEM((RING_SLOTS, n_pages, PAGE_SIZE, N_HEADS, V_DIM), F32)
    grid_spec = pltpu.PrefetchScalarGridSpec(
        num_scalar_prefetch=1,
        grid=(dec_batch, n_groups),
        in_specs=[tok_spec, tok_spec, tok_spec, hbm_spec, hbm_spec]
        + [lam_spec] * 4 + [pl.BlockSpec((1, V_DIM), lambda b, g, pt: (0, 0))],
        out_specs=tok_spec,
        scratch_shapes=[page_buf, page_buf,
                        pltpu.SemaphoreType.DMA((RING_SLOTS,)), pltpu.SemaphoreType.DMA((RING_SLOTS,)),
                        pltpu.VMEM((SUBLANES, LANES), F32), pltpu.VMEM((SUBLANES, LANES), F32),
                        pltpu.VMEM((V_DIM, LANES), F32)],
    )
    return pl.pallas_call(
        functools.partial(_decode_kernel, layer=layer, n_pages=n_pages, n_groups=n_groups,
                          n_steps=dec_batch * n_groups, dec_seq=dec_seq, lam_init=lam_init),
        grid_spec=grid_spec,
        out_shape=jax.ShapeDtypeStruct((dec_batch * qh, V_DIM), F32),
        compiler_params=_cparams(("arbitrary", "arbitrary")),
        name="decode_attn",
    )(page_table.reshape(-1), q4, k4, v4, cache_k, cache_v, *lams, g_subln.reshape(1, -1))


def _mixer_out_kernel(x_ref, a_ref, u_ref, hist_ref, gate_ref, gpost_ref, wout_ref, wpool_ref,
                      ps_ref, *refs, nb, t, seq_tiles, decode, pos_base):
    if decode:
        o_ref, state_ref, ext_scr, m_scr, pool_scr, win_a, win_b = refs
    else:
        o_ref, ext_scr, m_scr, pool_scr, win_a, win_b = refs
    rows = nb * t
    ext_scr[:, HIST_PAD:HIST_PAD + t, :] = u_ref[...]
    if decode:
        ext_scr[:, HIST_PAD - POOL_HIST:HIST_PAD, :] = hist_ref[...]
        state_ref[...] = ext_scr[:, HIST_PAD + t - POOL_HIST:HIST_PAD + t, :]
        base = pos_base
    else:
        first = (pl.program_id(0) % seq_tiles) == 0
        ext_scr[:, 0:HIST_PAD, :] = jnp.where(first, 0.0, hist_ref[...])
        base = (pl.program_id(0) % seq_tiles) * t

    end = HIST_PAD + t
    pos = base + lax.broadcasted_iota(jnp.int32, (1, t, POOL_GROUP_WIDTH), 1)
    for gi, w in enumerate(POOL_WINDOWS):
        cs = slice(gi * POOL_GROUP_WIDTH, (gi + 1) * POOL_GROUP_WIDTH)
        levels = w.bit_length() - 1
        lo = [HIST_PAD] * (levels + 1)
        for lv in range(levels - 1, 0, -1):
            lo[lv] = lo[lv + 1] - (1 << lv)
        src, dst = None, win_a
        for lv in range(1, levels + 1):
            sh, r0 = 1 << (lv - 1), lo[lv]
            if src is None:
                s = ext_scr[:, r0:end, cs] + ext_scr[:, r0 - sh:end - sh, cs]
            else:
                s = src[:, r0:end, :] + src[:, r0 - sh:end - sh, :]
            if lv < levels:
                dst[:, r0:end, :] = s
                src, dst = dst, (win_b if dst is win_a else win_a)
        cnt = jnp.minimum(w, pos + 1).astype(F32)
        pooled = s / cnt - ext_scr[:, HIST_PAD:end, cs]
        pool_scr[:, cs] = pooled.reshape(rows, POOL_GROUP_WIDTH).astype(BF16)
    if decode:
        a = jnp.concatenate([a_ref[pl.ds(h, rows, stride=N_HEADS), :] for h in range(N_HEADS)],
                            axis=1).astype(BF16)
    else:
        a = a_ref[...]
    m_scr[...] = jnp.dot(a, wout_ref[0:ATTN_WIDTH, :], preferred_element_type=F32)

    ys = [jnp.dot(pool_scr[:, gi * POOL_GROUP_WIDTH:(gi + 1) * POOL_GROUP_WIDTH], wpool_ref[gi],
                  preferred_element_type=F32) for gi in range(len(POOL_WINDOWS))]
    y = jnp.concatenate(ys, axis=1) * ps_ref[...]

    m_scr[...] += jnp.dot(y.astype(BF16), wout_ref[ATTN_WIDTH:, :], preferred_element_type=F32)
    _gated_residual_into(o_ref, x_ref, m_scr, gpost_ref, gate_ref)


def _mixer_out(x, a, u3, hist, mod, g_post, w_out_bf, w_pool_bf, pool_scale, *,
               nb, t, seq_tiles, decode, pos_base=0, layer=0):
    rows_total = x.shape[0]
    tm = nb * t
    nt = rows_total // tm
    out_specs = pl.BlockSpec((tm, D_MODEL), lambda i: (i, 0))
    out_shape = jax.ShapeDtypeStruct((rows_total, D_MODEL), F32)
    if decode:
        a_spec = pl.BlockSpec((tm * N_HEADS, V_DIM), lambda i: (i, 0))
        u_spec = pl.BlockSpec((nb, t, POOL_WIDTH), lambda i: (i, 0, 0))
        hist_spec = pl.BlockSpec((None, nb, POOL_HIST, POOL_WIDTH), lambda i: (layer, i, 0, 0))
        gate_spec = pl.BlockSpec((nb, D_MODEL), lambda i: (i, 2))
        out_specs = [out_specs, pl.BlockSpec((nb, POOL_HIST, POOL_WIDTH), lambda i: (i, 0, 0))]
        out_shape = [out_shape, jax.ShapeDtypeStruct((nt * nb, POOL_HIST, POOL_WIDTH), F32)]
    else:
        a_spec = pl.BlockSpec((tm, ATTN_WIDTH), lambda i: (i, 0))
        u_spec = pl.BlockSpec((1, t, POOL_WIDTH), lambda i: (i // seq_tiles, i % seq_tiles, 0))
        hpt = t // HIST_PAD
        hist_spec = pl.BlockSpec(
            (1, HIST_PAD, POOL_WIDTH),
            lambda i: (i // seq_tiles, jnp.maximum((i % seq_tiles) * hpt - 1, 0), 0))
        gate_spec = pl.BlockSpec((None, 1, D_MODEL), lambda i: (i // seq_tiles, 0, 2))
    return pl.pallas_call(
        functools.partial(_mixer_out_kernel, nb=nb, t=t, seq_tiles=seq_tiles, decode=decode,
                          pos_base=pos_base),
        grid=(nt,),
        in_specs=[
            pl.BlockSpec((tm, D_MODEL), lambda i: (i, 0)),
            a_spec, u_spec, hist_spec, gate_spec,
            pl.BlockSpec((1, D_MODEL), lambda i: (0, 0)),
            pl.BlockSpec((D_MODEL, D_MODEL), lambda i: (0, 0)),
            pl.BlockSpec((len(POOL_WINDOWS), POOL_GROUP_WIDTH, POOL_GROUP_WIDTH), lambda i: (0, 0, 0)),
            pl.BlockSpec((1, POOL_WIDTH), lambda i: (0, 0)),
        ],
        out_specs=out_specs,
        out_shape=out_shape,
        scratch_shapes=[pltpu.VMEM((nb, HIST_PAD + t, POOL_WIDTH), F32),
                        pltpu.VMEM((tm, D_MODEL), F32), pltpu.VMEM((tm, POOL_WIDTH), BF16),
                        pltpu.VMEM((nb, HIST_PAD + t, POOL_GROUP_WIDTH), F32),
                        pltpu.VMEM((nb, HIST_PAD + t, POOL_GROUP_WIDTH), F32)],
        compiler_params=_cparams(("parallel",)),
        name="mixer_out_decode" if decode else "mixer_out_prompt",
    )(x, a, u3, hist, mod, g_post.reshape(1, -1), w_out_bf, w_pool_bf, pool_scale.reshape(1, -1))


def _ffn_kernel(x_ref, shift_ref, scale_ref, gate_ref, gpre_ref, gpost_ref, w1_ref, w2_ref,
                *refs, n_f, decode):
    i = pl.program_id(0)
    j = pl.program_id(1)
    if decode:
        o_ref, w1bf_ref, w2bf_ref, h_scr, acc_scr = refs
        w1bf_ref[...] = w1_ref[...].astype(BF16)
        w2bf_ref[...] = w2_ref[...].astype(BF16)
        w1_ref, w2_ref = w1bf_ref, w2bf_ref

        @pl.when(j == 0)
        def _():
            _modulate_into(h_scr, x_ref, gpre_ref, shift_ref, scale_ref)
    else:
        xn_ref, shiftn_ref, scalen_ref, o_ref, h_scr, hb_scr, acc_scr = refs

        @pl.when((i == 0) & (j == 0))
        def _():
            _modulate_into(h_scr, x_ref, gpre_ref, shift_ref, scale_ref)

    @pl.when(j == 0)
    def _():
        acc_scr[...] = jnp.zeros(acc_scr.shape, F32)

    def step(h_cur, h_next):
        t = jnp.dot(h_cur[...], w1_ref[...], preferred_element_type=F32)
        t = jnp.square(jnp.maximum(t, 0.0))
        acc_scr[...] += jnp.dot(t.astype(BF16), w2_ref[...], preferred_element_type=F32)
        if h_next is not None:
            per_step = x_ref.shape[0] // n_f
            for r in range(0, per_step, ROW_CHUNK):
                rows = pl.ds(pl.multiple_of(j * per_step + r, ROW_CHUNK), ROW_CHUNK)
                h = (_rms(xn_ref[rows, :], gpre_ref[...]) * (1.0 + scalen_ref[...]) + shiftn_ref[...])
                h_next[rows, :] = h.astype(BF16)

    if decode:
        step(h_scr, None)
    else:
        even = lax.rem(i, 2) == 0
        pl.when(even)(functools.partial(step, h_scr, hb_scr))
        pl.when(jnp.logical_not(even))(functools.partial(step, hb_scr, h_scr))

    @pl.when(j == n_f - 1)
    def _():
        _gated_residual_into(o_ref, x_ref, acc_scr, gpost_ref, gate_ref)


def _ffn(x, mod, g_pre, g_post, w1, w2, *, tm, tf, seq_tiles, decode):
    rows = x.shape[0]
    nt = rows // tm
    n_f = D_FF // tf
    row_mode = dict(pipeline_mode=pl.Buffered(1)) if nt == 1 else {}
    x_spec = pl.BlockSpec((tm, D_MODEL), lambda i, j: (i, 0), **row_mode)
    out_specs = pl.BlockSpec((tm, D_MODEL), lambda i, j: (i, 0), **row_mode)
    out_shape = jax.ShapeDtypeStruct((rows, D_MODEL), F32)
    h_buf = pltpu.VMEM((tm, D_MODEL), BF16)
    scratch = ([h_buf] if decode else [h_buf, h_buf]) + [pltpu.VMEM((tm, D_MODEL), F32)]
    extra_in, extra_args = [], []
    if decode:
        assert nt == 1
        mod_spec = lambda c: pl.BlockSpec((tm // SUBLANES, D_MODEL), lambda i, j: (i, c))
        out_specs = [out_specs, pl.BlockSpec((D_MODEL, tf), lambda i, j: (0, j)),
                     pl.BlockSpec((tf, D_MODEL), lambda i, j: (j, 0))]
        out_shape = [out_shape, jax.ShapeDtypeStruct((D_MODEL, D_FF), BF16),
                     jax.ShapeDtypeStruct((D_FF, D_MODEL), BF16)]
    else:
        assert tm % (n_f * ROW_CHUNK) == 0
        mod_spec = lambda c: pl.BlockSpec((None, 1, D_MODEL), lambda i, j: (i // seq_tiles, 0, c))
        nxt = lambda i: jnp.minimum(i + 1, nt - 1)
        mod_next = lambda c: pl.BlockSpec((None, 1, D_MODEL), lambda i, j: (nxt(i) // seq_tiles, 0, c))
        extra_in = [pl.BlockSpec((tm, D_MODEL), lambda i, j: (nxt(i), 0)), mod_next(3), mod_next(4)]
        extra_args = [x, mod, mod]
    vec_spec = pl.BlockSpec((1, D_MODEL), lambda i, j: (0, 0))
    return pl.pallas_call(
        functools.partial(_ffn_kernel, n_f=n_f, decode=decode),
        grid=(nt, n_f),
        in_specs=[
            x_spec,
            mod_spec(3), mod_spec(4), mod_spec(5),
            vec_spec, vec_spec,
            pl.BlockSpec((D_MODEL, tf), lambda i, j: (0, j)),
            pl.BlockSpec((tf, D_MODEL), lambda i, j: (j, 0)),
        ] + extra_in,
        out_specs=out_specs,
        out_shape=out_shape,
        scratch_shapes=scratch,
        compiler_params=_cparams(("arbitrary", "arbitrary")),
        name="ffn_decode" if decode else "ffn_prompt",
    )(x, mod, mod, mod, g_pre.reshape(1, -1), g_post.reshape(1, -1), w1, w2, *extra_args)


def kernel(x_prompt, x_sample, cache_k, cache_v, state_pool, page_table, c_prompt, c_sample,
           w_ada, b_ada, g_pre_mix, w_in, lambda_q1, lambda_k1, lambda_q2, lambda_k2, g_subln,
           w_pool, pool_scale, w_out, g_post_mix, g_pre_ffn, w_ff1, w_ff2, g_post_ffn):
    batch, seq, _ = x_prompt.shape
    dec_batch, dec_seq, _ = x_sample.shape
    depth = w_in.shape[0]
    past_len = page_table.shape[1] * PAGE_SIZE
    assert dec_seq == SUBLANES and state_pool.shape[2] == POOL_HIST

    tm_p = 512
    seq_tiles = seq // tm_p
    rows_s = dec_batch * dec_seq
    tables_p = _rope_tables(np.arange(seq))
    tables_s = _rope_tables(past_len + (np.arange(rows_s) % dec_seq))

    n_c = batch + dec_batch
    c_all = jnp.concatenate(
        [c_sample, c_prompt, jnp.zeros((-n_c % SUBLANES, D_MODEL), F32)], axis=0)

    xp = x_prompt.reshape(batch * seq, D_MODEL)
    xs = x_sample.reshape(rows_s, D_MODEL)
    outs = [[] for _ in range(6)]
    for l in range(depth):
        lam_init = 0.8 - 0.6 * math.exp(-0.3 * l)
        lams = [v[l].reshape(1, HEAD_DIM) for v in (lambda_q1, lambda_k1, lambda_q2, lambda_k2)]
        w_out_bf = w_out[l].astype(BF16)
        w_pool_bf = w_pool[l].astype(BF16)

        m_all = _ada(c_all, w_ada[l], b_ada[l])
        mod_p = m_all[dec_batch:n_c].reshape(batch, 1, N_MOD * D_MODEL)
        mod_s = m_all

        q4s, k4s, v4s, us, w_in_bf = _mixer_in(
            xs, mod_s, g_pre_mix[l], w_in[l], tables_s, tm=tm_p, seq_tiles=1, decode=True)
        a4s = _decode(q4s, k4s, v4s, cache_k, cache_v, l, page_table, lams, g_subln[l],
                      dec_batch=dec_batch, dec_seq=dec_seq, lam_init=lam_init)
        us3 = us.reshape(dec_batch, dec_seq, POOL_WIDTH)
        xs, pool_s = _mixer_out(xs, a4s, us3, state_pool, mod_s, g_post_mix[l], w_out_bf, w_pool_bf,
                                pool_scale[l], nb=tm_p // (2 * dec_seq), t=dec_seq, seq_tiles=1,
                                decode=True, pos_base=past_len, layer=l)
        xs, w1_bf, w2_bf = _ffn(xs, mod_s, g_pre_ffn[l], g_post_ffn[l], w_ff1[l], w_ff2[l],
                                tm=rows_s, tf=512, seq_tiles=1, decode=True)
        outs[3].append(k4s.reshape(dec_batch, dec_seq, N_HEADS, V_DIM))
        outs[4].append(v4s.reshape(dec_batch, dec_seq, N_HEADS, V_DIM))
        outs[5].append(pool_s)

        q_bf, k4, k_bf, v4, v_bf, u = _mixer_in(
            xp, mod_p, g_pre_mix[l], w_in_bf, tables_p, tm=tm_p, seq_tiles=seq_tiles, decode=False)
        a_bf = _flash(q_bf, k_bf, v_bf, lams, g_subln[l], batch=batch, seq=seq, lam_init=lam_init)
        u3 = u.reshape(batch, seq, POOL_WIDTH)
        xp = _mixer_out(xp, a_bf, u3, u3, mod_p, g_post_mix[l], w_out_bf, w_pool_bf, pool_scale[l],
                        nb=1, t=tm_p, seq_tiles=seq_tiles, decode=False)
        xp = _ffn(xp, mod_p, g_pre_ffn[l], g_post_ffn[l], w1_bf, w2_bf,
                  tm=tm_p, tf=1024, seq_tiles=seq_tiles, decode=False)
        outs[0].append(k4.reshape(batch, seq, N_HEADS, V_DIM))
        outs[1].append(v4.reshape(batch, seq, N_HEADS, V_DIM))
        outs[2].append(u3[:, seq - POOL_HIST:])

    kp, vp, sp, ks, vs, ss = (jnp.stack(o) for o in outs)
    return (xp.reshape(batch, seq, D_MODEL), xs.reshape(dec_batch, dec_seq, D_MODEL),
            kp, vp, sp, ks, vs, ss)
```

```python
import functools
import math

import jax
import jax.numpy as jnp
import numpy as np
from jax import lax
from jax.experimental import pallas as pl
from jax.experimental.pallas import tpu as pltpu

F32 = jnp.float32
BF16 = jnp.bfloat16

D_MODEL = 2048
N_HEADS = 8
HEAD_DIM = 64
V_DIM = 2 * HEAD_DIM
ATTN_WIDTH = N_HEADS * V_DIM
POOL_WIDTH = 1024
IN_WIDTH = 3 * ATTN_WIDTH + POOL_WIDTH
ROT_DIM = HEAD_DIM // 4
ROPE_THETA = 500000.0
POOL_WINDOWS = (2, 4, 8, 16)
POOL_GROUP_WIDTH = POOL_WIDTH // len(POOL_WINDOWS)
POOL_HIST = max(POOL_WINDOWS) - 1
HIST_PAD = 16
D_FF = 4 * D_MODEL
N_MOD = 6
PAGE_SIZE = 128
EPS = 1e-6
NEG_BIG = -1e30
LOG2_E = math.log2(math.e)

LANES = 128
SUBLANES = 8
BF16_ROWS = 16
VMEM_LIMIT = 60 * 1024 * 1024
RING_SLOTS = 3
ROW_CHUNK = 16
POOL_ROWS = 64
KV_CHUNK = 512
W1_BLOCK = 1024


def _cparams(sem, **kwargs):
    return pltpu.CompilerParams(dimension_semantics=sem, vmem_limit_bytes=VMEM_LIMIT, **kwargs)


def _rms(x, g):
    ms = jnp.mean(x * x, axis=-1, keepdims=True)
    return x * lax.rsqrt(ms + EPS) * g


def _mod_rows(ref, r0, nrows, tile_rows):
    n, d = ref.shape
    if n == 1:
        return ref[...]
    per = tile_rows // n
    m = ref[r0 // per:(r0 + nrows) // per, :]
    return jnp.broadcast_to(m[:, None, :], (nrows // per, per, d)).reshape(nrows, d)


def _modulate_into(h_ref, x_ref, g_ref, shift_ref, scale_ref):
    tm = x_ref.shape[0]
    for r in range(0, tm, ROW_CHUNK):
        h = (_rms(x_ref[r:r + ROW_CHUNK, :], g_ref[...])
             * (1.0 + _mod_rows(scale_ref, r, ROW_CHUNK, tm)) + _mod_rows(shift_ref, r, ROW_CHUNK, tm))
        h_ref[r:r + ROW_CHUNK, :] = h.astype(BF16)


def _gated_residual_into(o_ref, x_ref, m_ref, g_ref, gate_ref):
    tm = x_ref.shape[0]
    for r in range(0, tm, ROW_CHUNK):
        rows = slice(r, r + ROW_CHUNK)
        o_ref[rows, :] = x_ref[rows, :] + _mod_rows(gate_ref, r, ROW_CHUNK, tm) * _rms(m_ref[rows, :], g_ref[...])


def _ada_kernel(c_ref, w_ref, b_ref, o_ref):
    c = c_ref[...]
    s = c * (1.0 / (1.0 + jnp.exp(-c)))
    o_ref[...] = jnp.dot(s.astype(BF16), w_ref[...].astype(BF16),
                         preferred_element_type=F32) + b_ref[...]


def _ada(c_all, w_ada, b_ada):
    rows = c_all.shape[0]
    tn = 1024
    return pl.pallas_call(
        _ada_kernel,
        grid=(N_MOD * D_MODEL // tn,),
        in_specs=[
            pl.BlockSpec((rows, D_MODEL), lambda j: (0, 0)),
            pl.BlockSpec((D_MODEL, tn), lambda j: (0, j)),
            pl.BlockSpec((1, tn), lambda j: (0, j)),
        ],
        out_specs=pl.BlockSpec((rows, tn), lambda j: (0, j)),
        out_shape=jax.ShapeDtypeStruct((rows, N_MOD * D_MODEL), F32),
        compiler_params=_cparams(("arbitrary",)),
        name="ada",
    )(c_all, w_ada, b_ada.reshape(1, -1))


def _rope(z, cos_ref, sa_ref, sb_ref):
    return (z * cos_ref[...]
            + pltpu.roll(z, LANES - ROT_DIM // 2, 1) * sa_ref[...]
            + pltpu.roll(z, ROT_DIM // 2, 1) * sb_ref[...])


def _mixer_in_kernel(x_ref, shift_ref, scale_ref, g_ref, w_ref, cos_ref, sa_ref, sb_ref,
                     *refs, tm, decode):
    j = pl.program_id(1)
    if decode:
        q4_ref, k4_ref, v4_ref, u_ref, wbf_ref, h_scr, wc_scr = refs
        wc_scr[...] = w_ref[...].astype(BF16)

        @pl.when(pl.program_id(0) == 0)
        def _():
            wbf_ref[...] = wc_scr[...]
    else:
        qb_ref, k4_ref, kb_ref, v4_ref, vb_ref, u_ref, h_scr = refs

    def weights(sec, lo, hi):
        if decode:
            return wc_scr[:, lo:hi]
        return w_ref[:, sec * ATTN_WIDTH + lo:sec * ATTN_WIDTH + hi]

    def heads(sec):
        for hp in range(N_HEADS // 2):
            z2 = jnp.dot(h_scr[...], weights(sec, 2 * hp * V_DIM, 2 * (hp + 1) * V_DIM),
                         preferred_element_type=F32)
            for h in (2 * hp, 2 * hp + 1):
                yield h, slice(h * V_DIM, (h + 1) * V_DIM), z2[:, (h % 2) * V_DIM:(h % 2 + 1) * V_DIM]

    def section(sec):
        if sec == 0:
            for h, cols, zh in heads(sec):
                q = _rope(zh, cos_ref, sa_ref, sb_ref) * (HEAD_DIM ** -0.5)
                if decode:
                    q4_ref[pl.ds(h, tm, stride=N_HEADS), :] = q
                else:
                    qb_ref[:, cols] = (q * LOG2_E).astype(BF16)
        elif sec == 1:
            for h, cols, zh in heads(sec):
                k = _rope(zh, cos_ref, sa_ref, sb_ref)
                k4_ref[pl.ds(h, tm, stride=N_HEADS), :] = k
                if not decode:
                    kb_ref[:, cols] = k.astype(BF16)
        elif sec == 2:
            for h, cols, zh in heads(sec):
                v4_ref[pl.ds(h, tm, stride=N_HEADS), :] = zh
                if not decode:
                    vb_ref[:, cols] = zh.astype(BF16)
        else:
            u_ref[...] = jnp.dot(h_scr[...], weights(sec, 0, POOL_WIDTH), preferred_element_type=F32)

    n_sec = IN_WIDTH // ATTN_WIDTH
    if decode:
        pl.when(j == 0)(functools.partial(_modulate_into, h_scr, x_ref, g_ref, shift_ref, scale_ref))
        for sec in range(n_sec):
            pl.when(j == sec)(functools.partial(section, sec))
    else:
        _modulate_into(h_scr, x_ref, g_ref, shift_ref, scale_ref)
        for sec in range(n_sec):
            section(sec)


def _mixer_in(x, mod, g_pre, w_in, tables, *, tm, seq_tiles, decode):
    rows = x.shape[0]
    nt = rows // tm
    if decode:
        mod_spec = lambda c: pl.BlockSpec((tm // SUBLANES, D_MODEL), lambda i, j: (i, c))
        tab_spec = pl.BlockSpec((tm, LANES), lambda i, j: (i, 0))
    else:
        mod_spec = lambda c: pl.BlockSpec((None, 1, D_MODEL), lambda i, j: (i // seq_tiles, 0, c))
        tab_spec = pl.BlockSpec((tm, LANES), lambda i, j: (i % seq_tiles, 0))
    row_bf = pl.BlockSpec((tm, ATTN_WIDTH), lambda i, j: (i, 0))
    row4 = pl.BlockSpec((tm * N_HEADS, V_DIM), lambda i, j: (i, 0))
    sd_bf = jax.ShapeDtypeStruct((rows, ATTN_WIDTH), BF16)
    sd4 = jax.ShapeDtypeStruct((rows * N_HEADS, V_DIM), F32)
    sd_u = jax.ShapeDtypeStruct((rows, POOL_WIDTH), F32)
    n_sec = IN_WIDTH // ATTN_WIDTH
    scratch = [pltpu.VMEM((tm, D_MODEL), BF16)]
    if decode:
        w_out_spec = pl.BlockSpec((D_MODEL, ATTN_WIDTH),
                                  lambda i, j: (0, jnp.where(i == 0, j, n_sec - 1)))
        out_specs = [row4, row4, row4, row_bf, w_out_spec]
        out_shape = [sd4, sd4, sd4, sd_u, jax.ShapeDtypeStruct((D_MODEL, IN_WIDTH), BF16)]
        scratch.append(pltpu.VMEM((D_MODEL, ATTN_WIDTH), BF16))
        steps = n_sec
        w_spec = pl.BlockSpec((D_MODEL, ATTN_WIDTH), lambda i, j: (0, j))
    else:
        out_specs = [row_bf, row4, row_bf, row4, row_bf, row_bf]
        out_shape = [sd_bf, sd4, sd_bf, sd4, sd_bf, sd_u]
        steps = 1
        w_spec = pl.BlockSpec((D_MODEL, IN_WIDTH), lambda i, j: (0, 0), pipeline_mode=pl.Buffered(1))
    return pl.pallas_call(
        functools.partial(_mixer_in_kernel, tm=tm, decode=decode),
        grid=(nt, steps),
        in_specs=[
            pl.BlockSpec((tm, D_MODEL), lambda i, j: (i, 0)),
            mod_spec(0), mod_spec(1),
            pl.BlockSpec((1, D_MODEL), lambda i, j: (0, 0)),
            w_spec,
            tab_spec, tab_spec, tab_spec,
        ],
        out_specs=out_specs,
        out_shape=out_shape,
        scratch_shapes=scratch,
        compiler_params=_cparams(("arbitrary", "arbitrary")),
        name="mixer_in_decode" if decode else "mixer_in_prompt",
    )(x, mod, mod, g_pre.reshape(1, -1), w_in, *tables)


def _rope_tables(pos):
    half = ROT_DIM // 2
    inv = 1.0 / (ROPE_THETA ** (np.arange(0, ROT_DIM, 2, dtype=np.float64) / ROT_DIM))
    ang = np.asarray(pos, np.float64)[:, None] * inv[None, :]
    cos, sin = np.cos(ang), np.sin(ang)
    lane = np.arange(LANES) % HEAD_DIM
    first = (lane < half)[None, :]
    second = ((lane >= half) & (lane < ROT_DIM))[None, :]
    cos_l = cos[:, lane % half]
    sin_l = sin[:, lane % half]
    c = np.where(first | second, cos_l, 1.0)
    sa = np.where(first, -sin_l, 0.0)
    sb = np.where(second, sin_l, 0.0)
    return tuple(jnp.asarray(t, F32) for t in (c, sa, sb))


def _lam(lq1, lk1, lq2, lk2, lam_init):
    a = jnp.sum(lq1[...] * lk1[...], axis=-1, keepdims=True)
    b = jnp.sum(lq2[...] * lk2[...], axis=-1, keepdims=True)
    return jnp.exp(a) - jnp.exp(b) + lam_init


def _flash_kernel(q_ref, k_ref, v_ref, lq1, lk1, lq2, lk2, g_ref, o_ref,
                  vt_scr, s_scr, *, tq, nq, lam_init):
    seq = nq * tq
    vq = 2 * tq
    vt_scr[0:V_DIM, :] = v_ref[...].astype(F32).T.astype(BF16)
    vt_scr[V_DIM:, :] = jnp.ones((BF16_ROWS, seq), BF16)
    lam = _lam(lq1, lk1, lq2, lk2, lam_init)
    lane = lax.broadcasted_iota(jnp.int32, (tq, V_DIM), 1)
    kv = lax.broadcasted_iota(jnp.int32, (tq, vq), 0)
    qp = lax.broadcasted_iota(jnp.int32, (tq, vq), 1) & (tq - 1)
    causal = kv <= qp

    def stacked_queries(ii):
        q = q_ref[ii * tq:(ii + 1) * tq, :].astype(F32)
        return jnp.concatenate([jnp.where(lane < HEAD_DIM, q, 0.0),
                                jnp.where(lane >= HEAD_DIM, q, 0.0)], axis=0).astype(BF16)

    def kv_chunks(ii):
        out, r = [], 0
        while r < ii * tq:
            size = min(KV_CHUNK, ii * tq - r)
            out.append((r, size, False))
            r += size
        return out + [(ii * tq, tq, True)]

    def score_pass(ii):
        qq = stacked_queries(ii)
        mx = jnp.full((SUBLANES, vq), NEG_BIG, F32)
        for r, size, masked in kv_chunks(ii):
            s = lax.dot_general(k_ref[r:r + size, :], qq, (((1,), (1,)), ((), ())),
                                preferred_element_type=F32)
            if masked:
                s = jnp.where(causal, s, NEG_BIG)
            s_scr[ii % 2, r:r + size, :] = s
            mx = jnp.maximum(mx, jnp.max(s.reshape(size // SUBLANES, SUBLANES, vq), axis=0))
        return jnp.max(mx, axis=0, keepdims=True)

    def value_pass(ii, m):
        o_t = jnp.zeros((V_DIM + BF16_ROWS, vq), F32)
        for r, size, _ in kv_chunks(ii):
            p = jnp.exp2(s_scr[ii % 2, r:r + size, :] - m)
            o_t = o_t + jnp.dot(vt_scr[:, r:r + size], p.astype(BF16), preferred_element_type=F32)
        return o_t

    m = score_pass(0)
    for ii in range(nq):
        m_next = score_pass(ii + 1) if ii + 1 < nq else None
        o_t = value_pass(ii, m)
        m = m_next
        n_t = o_t[:V_DIM] / o_t[V_DIM:V_DIM + 1]
        out = (n_t[:, :tq] - lam * n_t[:, tq:]).T
        o_ref[ii * tq:(ii + 1) * tq, :] = (_rms(out, g_ref[...]) * (1.0 - lam_init)).astype(BF16)


def _flash(q_bf, k_bf, v_bf, lams, g_subln, *, batch, seq, lam_init, tq=256):
    nq = seq // tq
    lam_spec = pl.BlockSpec((1, HEAD_DIM), lambda b, h: (0, 0))
    head_spec = pl.BlockSpec((seq, V_DIM), lambda b, h: (b, h))
    return pl.pallas_call(
        functools.partial(_flash_kernel, tq=tq, nq=nq, lam_init=lam_init),
        grid=(batch, N_HEADS),
        in_specs=[head_spec, head_spec, head_spec, lam_spec, lam_spec, lam_spec, lam_spec,
                  pl.BlockSpec((1, V_DIM), lambda b, h: (0, 0))],
        out_specs=head_spec,
        out_shape=jax.ShapeDtypeStruct((batch * seq, ATTN_WIDTH), BF16),
        scratch_shapes=[pltpu.VMEM((V_DIM + BF16_ROWS, seq), BF16),
                        pltpu.VMEM((2, seq, 2 * tq), F32)],
        compiler_params=_cparams(("parallel", "parallel")),
        name="flash_prompt",
    )(q_bf, k_bf, v_bf, *lams, g_subln.reshape(1, -1))


def _decode_kernel(pt_ref, q_ref, kn_ref, vn_ref, ck_hbm, cv_hbm, lq1, lk1, lq2, lk2, g_ref, o_ref,
                   kbuf, vbuf, ksem, vsem, m_scr, l_scr, acc_scr, *, layer, n_pages, n_groups,
                   n_steps, dec_seq, lam_init):
    g = pl.program_id(1)
    t = pl.program_id(0) * n_groups + g
    qh = dec_seq * N_HEADS
    page_rows = PAGE_SIZE * N_HEADS

    def page_copies(step, slot):
        cps = []
        for p in range(n_pages):
            page = pt_ref[step * n_pages + p]
            cps.append(pltpu.make_async_copy(ck_hbm.at[layer, page], kbuf.at[slot, p], ksem.at[slot]))
            cps.append(pltpu.make_async_copy(cv_hbm.at[layer, page], vbuf.at[slot, p], vsem.at[slot]))
        return cps

    ahead = RING_SLOTS - 1

    @pl.when(t == 0)
    def _():
        for s in range(min(ahead, n_steps)):
            for cp in page_copies(s, s):
                cp.start()

    @pl.when(t + ahead < n_steps)
    def _():
        for cp in page_copies(t + ahead, lax.rem(t + ahead, RING_SLOTS)):
            cp.start()

    slot = lax.rem(t, RING_SLOTS)
    for cp in page_copies(t, slot):
        cp.wait()

    @pl.when(g == 0)
    def _():
        m_scr[...] = jnp.full(m_scr.shape, NEG_BIG, F32)
        l_scr[...] = jnp.zeros(l_scr.shape, F32)
        acc_scr[...] = jnp.zeros(acc_scr.shape, F32)

    sub = lax.broadcasted_iota(jnp.int32, (SUBLANES, LANES), 0)
    lane = lax.broadcasted_iota(jnp.int32, (SUBLANES, LANES), 1)
    diag = (lane & (N_HEADS - 1)) == sub
    lane_q = lax.broadcasted_iota(jnp.int32, (qh, V_DIM), 1)
    lam = _lam(lq1, lk1, lq2, lk2, lam_init)

    r = q_ref[...]
    qt = jnp.concatenate([jnp.where(lane_q < HEAD_DIM, r, 0.0),
                          jnp.where(lane_q >= HEAD_DIM, r, 0.0)], axis=0).astype(BF16)

    def scores(k_rows):
        s = lax.dot_general(k_rows.astype(BF16), qt, (((1,), (1,)), ((), ())),
                            preferred_element_type=F32)
        return s.reshape(k_rows.shape[0] // N_HEADS, N_HEADS, LANES)

    def update(s_list, v_list):
        m_prev = m_scr[...]
        m_new = m_prev
        for s in s_list:
            m_new = jnp.maximum(m_new, jnp.max(s, axis=0))
        alpha = jnp.exp(m_prev - m_new)
        m_eff = jnp.where(diag, m_new, -NEG_BIG)
        l_new = alpha * l_scr[...]
        pv = jnp.zeros((V_DIM, LANES), F32)
        for s, v in zip(s_list, v_list):
            p = jnp.exp(s - m_eff[None])
            l_new = l_new + jnp.sum(p, axis=0)
            pm = p.reshape(v.shape[0], LANES).astype(BF16)
            pv = pv + lax.dot_general(v.astype(BF16), pm, (((0,), (0,)), ((), ())),
                                      preferred_element_type=F32)
        alpha_row = jnp.sum(jnp.where(diag, alpha, 0.0), axis=0, keepdims=True)
        acc_scr[...] = acc_scr[...] * alpha_row + pv
        l_scr[...] = l_new
        m_scr[...] = m_new

    update([scores(kbuf[slot, p].reshape(page_rows, V_DIM)) for p in range(n_pages)],
           [vbuf[slot, p].reshape(page_rows, V_DIM) for p in range(n_pages)])

    @pl.when(g == n_groups - 1)
    def _():
        s_new = scores(kn_ref[...])
        pos = lax.broadcasted_iota(jnp.int32, s_new.shape, 0)
        qidx = (lax.broadcasted_iota(jnp.int32, s_new.shape, 2) & (qh - 1)) >> 3
        update([jnp.where(pos <= qidx, s_new, NEG_BIG)], [vn_ref[...]])
        l_row = jnp.sum(jnp.where(diag, l_scr[...], 0.0), axis=0, keepdims=True)
        nt = (acc_scr[...] / l_row).T
        out = nt[:qh] - lam * nt[qh:]
        o_ref[...] = _rms(out, g_ref[...]) * (1.0 - lam_init)


def _decode(q4, k4, v4, cache_k, cache_v, layer, page_table, lams, g_subln, *,
            dec_batch, dec_seq, lam_init, n_pages=8):
    pages_per_seq = page_table.shape[1]
    assert pages_per_seq % n_pages == 0
    n_groups = pages_per_seq // n_pages
    qh = dec_seq * N_HEADS
    tok_spec = pl.BlockSpec((qh, V_DIM), lambda b, g, pt: (b, 0))
    lam_spec = pl.BlockSpec((1, HEAD_DIM), lambda b, g, pt: (0, 0))
    hbm_spec = pl.BlockSpec(memory_space=pl.ANY)
    page_buf = pltpu.VMEM((RING_SLOTS, n_pages, PAGE_SIZE, N_HEADS, V_DIM), F32)
    grid_spec = pltpu.PrefetchScalarGridSpec(
        num_scalar_prefetch=1,
        grid=(dec_batch, n_groups),
        in_specs=[tok_spec, tok_spec, tok_spec, hbm_spec, hbm_spec]
        + [lam_spec] * 4 + [pl.BlockSpec((1, V_DIM), lambda b, g, pt: (0, 0))],
        out_specs=tok_spec,
        scratch_shapes=[page_buf, page_buf,
                        pltpu.SemaphoreType.DMA((RING_SLOTS,)), pltpu.SemaphoreType.DMA((RING_SLOTS,)),
                        pltpu.VMEM((SUBLANES, LANES), F32), pltpu.VMEM((SUBLANES, LANES), F32),
                        pltpu.VMEM((V_DIM, LANES), F32)],
    )
    return pl.pallas_call(
        functools.partial(_decode_kernel, layer=layer, n_pages=n_pages, n_groups=n_groups,
                          n_steps=dec_batch * n_groups, dec_seq=dec_seq, lam_init=lam_init),
        grid_spec=grid_spec,
        out_shape=jax.ShapeDtypeStruct((dec_batch * qh, V_DIM), F32),
        compiler_params=_cparams(("arbitrary", "arbitrary")),
        name="decode_attn",
    )(page_table.reshape(-1), q4, k4, v4, cache_k, cache_v, *lams, g_subln.reshape(1, -1))


def _mixer_out_kernel(x_ref, a_ref, u_ref, hist_ref, gate_ref, gpost_ref, wout_ref, wpool_ref,
                      ps_ref, *refs, nb, t, seq_tiles, decode, pos_base):
    if decode:
        o_ref, state_ref, ext_scr, m_scr, pool_scr, win_a, win_b = refs
    else:
        o_ref, ext_scr, m_scr, pool_scr, win_a, win_b = refs
    rows = nb * t
    ext_scr[:, HIST_PAD:HIST_PAD + t, :] = u_ref[...]
    if decode:
        ext_scr[:, HIST_PAD - POOL_HIST:HIST_PAD, :] = hist_ref[...]
        state_ref[...] = ext_scr[:, HIST_PAD + t - POOL_HIST:HIST_PAD + t, :]
        base = pos_base
    else:
        first = (pl.program_id(0) % seq_tiles) == 0
        ext_scr[:, 0:HIST_PAD, :] = jnp.where(first, 0.0, hist_ref[...])
        base = (pl.program_id(0) % seq_tiles) * t

    end = HIST_PAD + t
    pos = base + lax.broadcasted_iota(jnp.int32, (1, t, POOL_GROUP_WIDTH), 1)
    for gi, w in enumerate(POOL_WINDOWS):
        cs = slice(gi * POOL_GROUP_WIDTH, (gi + 1) * POOL_GROUP_WIDTH)
        levels = w.bit_length() - 1
        lo = [HIST_PAD] * (levels + 1)
        for lv in range(levels - 1, 0, -1):
            lo[lv] = lo[lv + 1] - (1 << lv)
        src, dst = None, win_a
        for lv in range(1, levels + 1):
            sh, r0 = 1 << (lv - 1), lo[lv]
            if src is None:
                s = ext_scr[:, r0:end, cs] + ext_scr[:, r0 - sh:end - sh, cs]
            else:
                s = src[:, r0:end, :] + src[:, r0 - sh:end - sh, :]
            if lv < levels:
                dst[:, r0:end, :] = s
                src, dst = dst, (win_b if dst is win_a else win_a)
        cnt = jnp.minimum(w, pos + 1).astype(F32)
        pooled = s / cnt - ext_scr[:, HIST_PAD:end, cs]
        pool_scr[:, cs] = pooled.reshape(rows, POOL_GROUP_WIDTH).astype(BF16)
    if decode:
        a = jnp.concatenate([a_ref[pl.ds(h, rows, stride=N_HEADS), :] for h in range(N_HEADS)],
                            axis=1).astype(BF16)
    else:
        a = a_ref[...]
    m_scr[...] = jnp.dot(a, wout_ref[0:ATTN_WIDTH, :], preferred_element_type=F32)

    ys = [jnp.dot(pool_scr[:, gi * POOL_GROUP_WIDTH:(gi + 1) * POOL_GROUP_WIDTH], wpool_ref[gi],
                  preferred_element_type=F32) for gi in range(len(POOL_WINDOWS))]
    y = jnp.concatenate(ys, axis=1) * ps_ref[...]

    m_scr[...] += jnp.dot(y.astype(BF16), wout_ref[ATTN_WIDTH:, :], preferred_element_type=F32)
    _gated_residual_into(o_ref, x_ref, m_scr, gpost_ref, gate_ref)


def _mixer_out(x, a, u3, hist, mod, g_post, w_out_bf, w_pool_bf, pool_scale, *,
               nb, t, seq_tiles, decode, pos_base=0, layer=0):
    rows_total = x.shape[0]
    tm = nb * t
    nt = rows_total // tm
    out_specs = pl.BlockSpec((tm, D_MODEL), lambda i: (i, 0))
    out_shape = jax.ShapeDtypeStruct((rows_total, D_MODEL), F32)
    if decode:
        a_spec = pl.BlockSpec((tm * N_HEADS, V_DIM), lambda i: (i, 0))
        u_spec = pl.BlockSpec((nb, t, POOL_WIDTH), lambda i: (i, 0, 0))
        hist_spec = pl.BlockSpec((None, nb, POOL_HIST, POOL_WIDTH), lambda i: (layer, i, 0, 0))
        gate_spec = pl.BlockSpec((nb, D_MODEL), lambda i: (i, 2))
        out_specs = [out_specs, pl.BlockSpec((nb, POOL_HIST, POOL_WIDTH), lambda i: (i, 0, 0))]
        out_shape = [out_shape, jax.ShapeDtypeStruct((nt * nb, POOL_HIST, POOL_WIDTH), F32)]
    else:
        a_spec = pl.BlockSpec((tm, ATTN_WIDTH), lambda i: (i, 0))
        u_spec = pl.BlockSpec((1, t, POOL_WIDTH), lambda i: (i // seq_tiles, i % seq_tiles, 0))
        hpt = t // HIST_PAD
        hist_spec = pl.BlockSpec(
            (1, HIST_PAD, POOL_WIDTH),
            lambda i: (i // seq_tiles, jnp.maximum((i % seq_tiles) * hpt - 1, 0), 0))
        gate_spec = pl.BlockSpec((None, 1, D_MODEL), lambda i: (i // seq_tiles, 0, 2))
    return pl.pallas_call(
        functools.partial(_mixer_out_kernel, nb=nb, t=t, seq_tiles=seq_tiles, decode=decode,
                          pos_base=pos_base),
        grid=(nt,),
        in_specs=[
            pl.BlockSpec((tm, D_MODEL), lambda i: (i, 0)),
            a_spec, u_spec, hist_spec, gate_spec,
            pl.BlockSpec((1, D_MODEL), lambda i: (0, 0)),
            pl.BlockSpec((D_MODEL, D_MODEL), lambda i: (0, 0)),
            pl.BlockSpec((len(POOL_WINDOWS), POOL_GROUP_WIDTH, POOL_GROUP_WIDTH), lambda i: (0, 0, 0)),
            pl.BlockSpec((1, POOL_WIDTH), lambda i: (0, 0)),
        ],
        out_specs=out_specs,
        out_shape=out_shape,
        scratch_shapes=[pltpu.VMEM((nb, HIST_PAD + t, POOL_WIDTH), F32),
                        pltpu.VMEM((tm, D_MODEL), F32), pltpu.VMEM((tm, POOL_WIDTH), BF16),
                        pltpu.VMEM((nb, HIST_PAD + t, POOL_GROUP_WIDTH), F32),
                        pltpu.VMEM((nb, HIST_PAD + t, POOL_GROUP_WIDTH), F32)],
        compiler_params=_cparams(("parallel",)),
        name="mixer_out_decode" if decode else "mixer_out_prompt",
    )(x, a, u3, hist, mod, g_post.reshape(1, -1), w_out_bf, w_pool_bf, pool_scale.reshape(1, -1))


def _ffn_kernel(x_ref, shift_ref, scale_ref, gate_ref, gpre_ref, gpost_ref, w1_ref, w2_ref,
                *refs, n_f, decode):
    i = pl.program_id(0)
    j = pl.program_id(1)
    if decode:
        o_ref, w1bf_ref, w2bf_ref, h_scr, acc_scr = refs
        w1bf_ref[...] = w1_ref[...].astype(BF16)
        w2bf_ref[...] = w2_ref[...].astype(BF16)
        w1_ref, w2_ref = w1bf_ref, w2bf_ref

        @pl.when(j == 0)
        def _():
            _modulate_into(h_scr, x_ref, gpre_ref, shift_ref, scale_ref)
    else:
        xn_ref, shiftn_ref, scalen_ref, o_ref, h_scr, hb_scr, acc_scr = refs

        @pl.when((i == 0) & (j == 0))
        def _():
            _modulate_into(h_scr, x_ref, gpre_ref, shift_ref, scale_ref)

    @pl.when(j == 0)
    def _():
        acc_scr[...] = jnp.zeros(acc_scr.shape, F32)

    def step(h_cur, h_next):
        t = jnp.dot(h_cur[...], w1_ref[...], preferred_element_type=F32)
        t = jnp.square(jnp.maximum(t, 0.0))
        acc_scr[...] += jnp.dot(t.astype(BF16), w2_ref[...], preferred_element_type=F32)
        if h_next is not None:
            per_step = x_ref.shape[0] // n_f
            for r in range(0, per_step, ROW_CHUNK):
                rows = pl.ds(pl.multiple_of(j * per_step + r, ROW_CHUNK), ROW_CHUNK)
                h = (_rms(xn_ref[rows, :], gpre_ref[...]) * (1.0 + scalen_ref[...]) + shiftn_ref[...])
                h_next[rows, :] = h.astype(BF16)

    if decode:
        step(h_scr, None)
    else:
        even = lax.rem(i, 2) == 0
        pl.when(even)(functools.partial(step, h_scr, hb_scr))
        pl.when(jnp.logical_not(even))(functools.partial(step, hb_scr, h_scr))

    @pl.when(j == n_f - 1)
    def _():
        _gated_residual_into(o_ref, x_ref, acc_scr, gpost_ref, gate_ref)


def _ffn(x, mod, g_pre, g_post, w1, w2, *, tm, tf, seq_tiles, decode):
    rows = x.shape[0]
    nt = rows // tm
    n_f = D_FF // tf
    row_mode = dict(pipeline_mode=pl.Buffered(1)) if nt == 1 else {}
    x_spec = pl.BlockSpec((tm, D_MODEL), lambda i, j: (i, 0), **row_mode)
    out_specs = pl.BlockSpec((tm, D_MODEL), lambda i, j: (i, 0), **row_mode)
    out_shape = jax.ShapeDtypeStruct((rows, D_MODEL), F32)
    h_buf = pltpu.VMEM((tm, D_MODEL), BF16)
    scratch = ([h_buf] if decode else [h_buf, h_buf]) + [pltpu.VMEM((tm, D_MODEL), F32)]
    extra_in, extra_args = [], []
    if decode:
        assert nt == 1 and W1_BLOCK % tf == 0
        mod_spec = lambda c: pl.BlockSpec((tm // SUBLANES, D_MODEL), lambda i, j: (i, c))
        w1_spec = pl.BlockSpec((D_MODEL, tf), lambda i, j: (0, j))
        per = W1_BLOCK // tf
        out_specs = [out_specs,
                     pl.BlockSpec((None, D_MODEL, tf), lambda i, j: (j // per, 0, j % per)),
                     pl.BlockSpec((tf, D_MODEL), lambda i, j: (j, 0))]
        out_shape = [out_shape, jax.ShapeDtypeStruct((D_FF // W1_BLOCK, D_MODEL, W1_BLOCK), BF16),
                     jax.ShapeDtypeStruct((D_FF, D_MODEL), BF16)]
    else:
        assert tm % (n_f * ROW_CHUNK) == 0 and tf == W1_BLOCK
        w1_spec = pl.BlockSpec((None, D_MODEL, tf), lambda i, j: (j, 0, 0))
        mod_spec = lambda c: pl.BlockSpec((None, 1, D_MODEL), lambda i, j: (i // seq_tiles, 0, c))
        nxt = lambda i: jnp.minimum(i + 1, nt - 1)
        mod_next = lambda c: pl.BlockSpec((None, 1, D_MODEL), lambda i, j: (nxt(i) // seq_tiles, 0, c))
        extra_in = [pl.BlockSpec((tm, D_MODEL), lambda i, j: (nxt(i), 0)), mod_next(3), mod_next(4)]
        extra_args = [x, mod, mod]
    vec_spec = pl.BlockSpec((1, D_MODEL), lambda i, j: (0, 0))
    return pl.pallas_call(
        functools.partial(_ffn_kernel, n_f=n_f, decode=decode),
        grid=(nt, n_f),
        in_specs=[
            x_spec,
            mod_spec(3), mod_spec(4), mod_spec(5),
            vec_spec, vec_spec,
            w1_spec,
            pl.BlockSpec((tf, D_MODEL), lambda i, j: (j, 0)),
        ] + extra_in,
        out_specs=out_specs,
        out_shape=out_shape,
        scratch_shapes=scratch,
        compiler_params=_cparams(("arbitrary", "arbitrary")),
        name="ffn_decode" if decode else "ffn_prompt",
    )(x, mod, mod, mod, g_pre.reshape(1, -1), g_post.reshape(1, -1), w1, w2, *extra_args)


def kernel(x_prompt, x_sample, cache_k, cache_v, state_pool, page_table, c_prompt, c_sample,
           w_ada, b_ada, g_pre_mix, w_in, lambda_q1, lambda_k1, lambda_q2, lambda_k2, g_subln,
           w_pool, pool_scale, w_out, g_post_mix, g_pre_ffn, w_ff1, w_ff2, g_post_ffn):
    batch, seq, _ = x_prompt.shape
    dec_batch, dec_seq, _ = x_sample.shape
    depth = w_in.shape[0]
    past_len = page_table.shape[1] * PAGE_SIZE
    assert dec_seq == SUBLANES and state_pool.shape[2] == POOL_HIST

    tm_p = 512
    seq_tiles = seq // tm_p
    rows_s = dec_batch * dec_seq
    tables_p = _rope_tables(np.arange(seq))
    tables_s = _rope_tables(past_len + (np.arange(rows_s) % dec_seq))

    n_c = batch + dec_batch
    c_all = jnp.concatenate(
        [c_sample, c_prompt, jnp.zeros((-n_c % SUBLANES, D_MODEL), F32)], axis=0)

    xp = x_prompt.reshape(batch * seq, D_MODEL)
    xs = x_sample.reshape(rows_s, D_MODEL)
    outs = [[] for _ in range(6)]
    for l in range(depth):
        lam_init = 0.8 - 0.6 * math.exp(-0.3 * l)
        lams = [v[l].reshape(1, HEAD_DIM) for v in (lambda_q1, lambda_k1, lambda_q2, lambda_k2)]
        w_out_bf = w_out[l].astype(BF16)
        w_pool_bf = w_pool[l].astype(BF16)

        m_all = _ada(c_all, w_ada[l], b_ada[l])
        mod_p = m_all[dec_batch:n_c].reshape(batch, 1, N_MOD * D_MODEL)
        mod_s = m_all

        q4s, k4s, v4s, us, w_in_bf = _mixer_in(
            xs, mod_s, g_pre_mix[l], w_in[l], tables_s, tm=tm_p, seq_tiles=1, decode=True)
        a4s = _decode(q4s, k4s, v4s, cache_k, cache_v, l, page_table, lams, g_subln[l],
                      dec_batch=dec_batch, dec_seq=dec_seq, lam_init=lam_init)
        us3 = us.reshape(dec_batch, dec_seq, POOL_WIDTH)
        xs, pool_s = _mixer_out(xs, a4s, us3, state_pool, mod_s, g_post_mix[l], w_out_bf, w_pool_bf,
                                pool_scale[l], nb=tm_p // (2 * dec_seq), t=dec_seq, seq_tiles=1,
                                decode=True, pos_base=past_len, layer=l)
        xs, w1_bf, w2_bf = _ffn(xs, mod_s, g_pre_ffn[l], g_post_ffn[l], w_ff1[l], w_ff2[l],
                                tm=rows_s, tf=512, seq_tiles=1, decode=True)
        outs[3].append(k4s.reshape(dec_batch, dec_seq, N_HEADS, V_DIM))
        outs[4].append(v4s.reshape(dec_batch, dec_seq, N_HEADS, V_DIM))
        outs[5].append(pool_s)

        q_bf, k4, k_bf, v4, v_bf, u = _mixer_in(
            xp, mod_p, g_pre_mix[l], w_in_bf, tables_p, tm=tm_p, seq_tiles=seq_tiles, decode=False)
        a_bf = _flash(q_bf, k_bf, v_bf, lams, g_subln[l], batch=batch, seq=seq, lam_init=lam_init)
        u3 = u.reshape(batch, seq, POOL_WIDTH)
        xp = _mixer_out(xp, a_bf, u3, u3, mod_p, g_post_mix[l], w_out_bf, w_pool_bf, pool_scale[l],
                        nb=1, t=tm_p, seq_tiles=seq_tiles, decode=False)
        xp = _ffn(xp, mod_p, g_pre_ffn[l], g_post_ffn[l], w1_bf, w2_bf,
                  tm=tm_p, tf=W1_BLOCK, seq_tiles=seq_tiles, decode=False)
        outs[0].append(k4.reshape(batch, seq, N_HEADS, V_DIM))
        outs[1].append(v4.reshape(batch, seq, N_HEADS, V_DIM))
        outs[2].append(u3[:, seq - POOL_HIST:])

    kp, vp, sp, ks, vs, ss = (jnp.stack(o) for o in outs)
    return (xp.reshape(batch, seq, D_MODEL), xs.reshape(dec_batch, dec_seq, D_MODEL),
            kp, vp, sp, ks, vs, ss)
```

```python
import functools
import math

import jax
import jax.numpy as jnp
import numpy as np
from jax import lax
from jax.experimental import pallas as pl
from jax.experimental.pallas import tpu as pltpu

F32 = jnp.float32
BF16 = jnp.bfloat16

D_MODEL = 2048
N_HEADS = 8
HEAD_DIM = 64
V_DIM = 2 * HEAD_DIM
ATTN_WIDTH = N_HEADS * V_DIM
POOL_WIDTH = 1024
IN_WIDTH = 3 * ATTN_WIDTH + POOL_WIDTH
ROT_DIM = HEAD_DIM // 4
ROPE_THETA = 500000.0
POOL_WINDOWS = (2, 4, 8, 16)
POOL_GROUP_WIDTH = POOL_WIDTH // len(POOL_WINDOWS)
POOL_HIST = max(POOL_WINDOWS) - 1
HIST_PAD = 16
D_FF = 4 * D_MODEL
N_MOD = 6
PAGE_SIZE = 128
EPS = 1e-6
NEG_BIG = -1e30
LOG2_E = math.log2(math.e)

LANES = 128
SUBLANES = 8
BF16_ROWS = 16
VMEM_LIMIT = 60 * 1024 * 1024

ROW_TILE = 512
ROW_CHUNK = 16
ADA_COLS = 1024
FLASH_Q = 256
KV_CHUNK = 512
DECODE_PAGES = 8
RING_SLOTS = 3
W1_BLOCK = 1024
FFN_DECODE_COLS = 512


def _cparams(sem, **kwargs):
    return pltpu.CompilerParams(dimension_semantics=sem, vmem_limit_bytes=VMEM_LIMIT, **kwargs)


def _rms(x, g):
    ms = jnp.mean(x * x, axis=-1, keepdims=True)
    return x * lax.rsqrt(ms + EPS) * g


def _mod_rows(ref, r0, nrows, tile_rows):
    n, d = ref.shape
    if n == 1:
        return ref[...]
    per = tile_rows // n
    m = ref[r0 // per:(r0 + nrows) // per, :]
    return jnp.broadcast_to(m[:, None, :], (nrows // per, per, d)).reshape(nrows, d)


def _modulate_into(h_ref, x_ref, g_ref, shift_ref, scale_ref):
    tm = x_ref.shape[0]
    for r in range(0, tm, ROW_CHUNK):
        h = (_rms(x_ref[r:r + ROW_CHUNK, :], g_ref[...])
             * (1.0 + _mod_rows(scale_ref, r, ROW_CHUNK, tm)) + _mod_rows(shift_ref, r, ROW_CHUNK, tm))
        h_ref[r:r + ROW_CHUNK, :] = h.astype(BF16)


def _gated_residual_into(o_ref, x_ref, m_ref, g_ref, gate_ref):
    tm = x_ref.shape[0]
    for r in range(0, tm, ROW_CHUNK):
        rows = slice(r, r + ROW_CHUNK)
        o_ref[rows, :] = x_ref[rows, :] + _mod_rows(gate_ref, r, ROW_CHUNK, tm) * _rms(m_ref[rows, :], g_ref[...])


def _ada_kernel(c_ref, w_ref, b_ref, o_ref):
    c = c_ref[...]
    s = c * (1.0 / (1.0 + jnp.exp(-c)))
    o_ref[...] = jnp.dot(s.astype(BF16), w_ref[...].astype(BF16),
                         preferred_element_type=F32) + b_ref[...]


def _ada(c_all, w_ada, b_ada):
    rows = c_all.shape[0]
    tn = ADA_COLS
    return pl.pallas_call(
        _ada_kernel,
        grid=(N_MOD * D_MODEL // tn,),
        in_specs=[
            pl.BlockSpec((rows, D_MODEL), lambda j: (0, 0)),
            pl.BlockSpec((D_MODEL, tn), lambda j: (0, j)),
            pl.BlockSpec((1, tn), lambda j: (0, j)),
        ],
        out_specs=pl.BlockSpec((rows, tn), lambda j: (0, j)),
        out_shape=jax.ShapeDtypeStruct((rows, N_MOD * D_MODEL), F32),
        compiler_params=_cparams(("arbitrary",)),
        name="ada",
    )(c_all, w_ada, b_ada.reshape(1, -1))


def _rope(z, cos_ref, sa_ref, sb_ref):
    return (z * cos_ref[...]
            + pltpu.roll(z, LANES - ROT_DIM // 2, 1) * sa_ref[...]
            + pltpu.roll(z, ROT_DIM // 2, 1) * sb_ref[...])


def _mixer_in_kernel(x_ref, shift_ref, scale_ref, g_ref, w_ref, cos_ref, sa_ref, sb_ref,
                     *refs, tm, decode):
    j = pl.program_id(1)
    if decode:
        q4_ref, k4_ref, v4_ref, u_ref, wbf_ref, h_scr, wc_scr = refs
        wc_scr[...] = w_ref[...].astype(BF16)

        @pl.when(pl.program_id(0) == 0)
        def _():
            wbf_ref[...] = wc_scr[...]
    else:
        qb_ref, k4_ref, kb_ref, v4_ref, vb_ref, u_ref, h_scr = refs

    def weights(sec, lo, hi):
        if decode:
            return wc_scr[:, lo:hi]
        return w_ref[:, sec * ATTN_WIDTH + lo:sec * ATTN_WIDTH + hi]

    def heads(sec):
        for hp in range(N_HEADS // 2):
            z2 = jnp.dot(h_scr[...], weights(sec, 2 * hp * V_DIM, 2 * (hp + 1) * V_DIM),
                         preferred_element_type=F32)
            for h in (2 * hp, 2 * hp + 1):
                yield h, slice(h * V_DIM, (h + 1) * V_DIM), z2[:, (h % 2) * V_DIM:(h % 2 + 1) * V_DIM]

    def section(sec):
        if sec == 0:
            for h, cols, zh in heads(sec):
                q = _rope(zh, cos_ref, sa_ref, sb_ref) * (HEAD_DIM ** -0.5)
                if decode:
                    q4_ref[pl.ds(h, tm, stride=N_HEADS), :] = q
                else:
                    qb_ref[:, cols] = (q * LOG2_E).astype(BF16)
        elif sec == 1:
            for h, cols, zh in heads(sec):
                k = _rope(zh, cos_ref, sa_ref, sb_ref)
                k4_ref[pl.ds(h, tm, stride=N_HEADS), :] = k
                if not decode:
                    kb_ref[:, cols] = k.astype(BF16)
        elif sec == 2:
            for h, cols, zh in heads(sec):
                v4_ref[pl.ds(h, tm, stride=N_HEADS), :] = zh
                if not decode:
                    vb_ref[:, cols] = zh.astype(BF16)
        else:
            u_ref[...] = jnp.dot(h_scr[...], weights(sec, 0, POOL_WIDTH), preferred_element_type=F32)

    n_sec = IN_WIDTH // ATTN_WIDTH
    if decode:
        pl.when(j == 0)(functools.partial(_modulate_into, h_scr, x_ref, g_ref, shift_ref, scale_ref))
        for sec in range(n_sec):
            pl.when(j == sec)(functools.partial(section, sec))
    else:
        _modulate_into(h_scr, x_ref, g_ref, shift_ref, scale_ref)
        for sec in range(n_sec):
            section(sec)


def _mixer_in(x, mod, g_pre, w_in, tables, *, tm, seq_tiles, decode):
    rows = x.shape[0]
    nt = rows // tm
    if decode:
        mod_spec = lambda c: pl.BlockSpec((tm // SUBLANES, D_MODEL), lambda i, j: (i, c))
        tab_spec = pl.BlockSpec((tm, LANES), lambda i, j: (i, 0))
    else:
        mod_spec = lambda c: pl.BlockSpec((None, 1, D_MODEL), lambda i, j: (i // seq_tiles, 0, c))
        tab_spec = pl.BlockSpec((tm, LANES), lambda i, j: (i % seq_tiles, 0))
    row_bf = pl.BlockSpec((tm, ATTN_WIDTH), lambda i, j: (i, 0))
    row4 = pl.BlockSpec((tm * N_HEADS, V_DIM), lambda i, j: (i, 0))
    sd_bf = jax.ShapeDtypeStruct((rows, ATTN_WIDTH), BF16)
    sd4 = jax.ShapeDtypeStruct((rows * N_HEADS, V_DIM), F32)
    sd_u = jax.ShapeDtypeStruct((rows, POOL_WIDTH), F32)
    n_sec = IN_WIDTH // ATTN_WIDTH
    scratch = [pltpu.VMEM((tm, D_MODEL), BF16)]
    if decode:
        w_out_spec = pl.BlockSpec((D_MODEL, ATTN_WIDTH),
                                  lambda i, j: (0, jnp.where(i == 0, j, n_sec - 1)))
        out_specs = [row4, row4, row4, row_bf, w_out_spec]
        out_shape = [sd4, sd4, sd4, sd_u, jax.ShapeDtypeStruct((D_MODEL, IN_WIDTH), BF16)]
        scratch.append(pltpu.VMEM((D_MODEL, ATTN_WIDTH), BF16))
        steps = n_sec
        w_spec = pl.BlockSpec((D_MODEL, ATTN_WIDTH), lambda i, j: (0, j))
    else:
        out_specs = [row_bf, row4, row_bf, row4, row_bf, row_bf]
        out_shape = [sd_bf, sd4, sd_bf, sd4, sd_bf, sd_u]
        steps = 1
        w_spec = pl.BlockSpec((D_MODEL, IN_WIDTH), lambda i, j: (0, 0), pipeline_mode=pl.Buffered(1))
    return pl.pallas_call(
        functools.partial(_mixer_in_kernel, tm=tm, decode=decode),
        grid=(nt, steps),
        in_specs=[
            pl.BlockSpec((tm, D_MODEL), lambda i, j: (i, 0)),
            mod_spec(0), mod_spec(1),
            pl.BlockSpec((1, D_MODEL), lambda i, j: (0, 0)),
            w_spec,
            tab_spec, tab_spec, tab_spec,
        ],
        out_specs=out_specs,
        out_shape=out_shape,
        scratch_shapes=scratch,
        compiler_params=_cparams(("arbitrary", "arbitrary")),
        name="mixer_in_decode" if decode else "mixer_in_prompt",
    )(x, mod, mod, g_pre.reshape(1, -1), w_in, *tables)


def _rope_tables(pos):
    half = ROT_DIM // 2
    inv = 1.0 / (ROPE_THETA ** (np.arange(0, ROT_DIM, 2, dtype=np.float64) / ROT_DIM))
    ang = np.asarray(pos, np.float64)[:, None] * inv[None, :]
    cos, sin = np.cos(ang), np.sin(ang)
    lane = np.arange(LANES) % HEAD_DIM
    first = (lane < half)[None, :]
    second = ((lane >= half) & (lane < ROT_DIM))[None, :]
    cos_l = cos[:, lane % half]
    sin_l = sin[:, lane % half]
    c = np.where(first | second, cos_l, 1.0)
    sa = np.where(first, -sin_l, 0.0)
    sb = np.where(second, sin_l, 0.0)
    return tuple(jnp.asarray(t, F32) for t in (c, sa, sb))


def _lam(lq1, lk1, lq2, lk2, lam_init):
    a = jnp.sum(lq1[...] * lk1[...], axis=-1, keepdims=True)
    b = jnp.sum(lq2[...] * lk2[...], axis=-1, keepdims=True)
    return jnp.exp(a) - jnp.exp(b) + lam_init


def _flash_kernel(q_ref, k_ref, v_ref, lq1, lk1, lq2, lk2, g_ref, o_ref,
                  vt_scr, s_scr, *, tq, nq, lam_init):
    seq = nq * tq
    vq = 2 * tq
    vt_scr[0:V_DIM, :] = v_ref[...].astype(F32).T.astype(BF16)
    vt_scr[V_DIM:, :] = jnp.ones((BF16_ROWS, seq), BF16)
    lam = _lam(lq1, lk1, lq2, lk2, lam_init)
    lane = lax.broadcasted_iota(jnp.int32, (tq, V_DIM), 1)
    kv = lax.broadcasted_iota(jnp.int32, (tq, vq), 0)
    qp = lax.broadcasted_iota(jnp.int32, (tq, vq), 1) & (tq - 1)
    causal = kv <= qp

    def stacked_queries(ii):
        q = q_ref[ii * tq:(ii + 1) * tq, :].astype(F32)
        return jnp.concatenate([jnp.where(lane < HEAD_DIM, q, 0.0),
                                jnp.where(lane >= HEAD_DIM, q, 0.0)], axis=0).astype(BF16)

    def kv_chunks(ii):
        out, r = [], 0
        while r < ii * tq:
            size = min(KV_CHUNK, ii * tq - r)
            out.append((r, size, False))
            r += size
        return out + [(ii * tq, tq, True)]

    def score_pass(ii):
        qq = stacked_queries(ii)
        mx = jnp.full((SUBLANES, vq), NEG_BIG, F32)
        for r, size, masked in kv_chunks(ii):
            s = lax.dot_general(k_ref[r:r + size, :], qq, (((1,), (1,)), ((), ())),
                                preferred_element_type=F32)
            if masked:
                s = jnp.where(causal, s, NEG_BIG)
            s_scr[ii % 2, r:r + size, :] = s
            mx = jnp.maximum(mx, jnp.max(s.reshape(size // SUBLANES, SUBLANES, vq), axis=0))
        return jnp.max(mx, axis=0, keepdims=True)

    def value_pass(ii, m):
        o_t = jnp.zeros((V_DIM + BF16_ROWS, vq), F32)
        for r, size, _ in kv_chunks(ii):
            p = jnp.exp2(s_scr[ii % 2, r:r + size, :] - m)
            o_t = o_t + jnp.dot(vt_scr[:, r:r + size], p.astype(BF16), preferred_element_type=F32)
        return o_t

    m = score_pass(0)
    for ii in range(nq):
        m_next = score_pass(ii + 1) if ii + 1 < nq else None
        o_t = value_pass(ii, m)
        m = m_next
        n_t = o_t[:V_DIM] / o_t[V_DIM:V_DIM + 1]
        out = (n_t[:, :tq] - lam * n_t[:, tq:]).T
        o_ref[ii * tq:(ii + 1) * tq, :] = (_rms(out, g_ref[...]) * (1.0 - lam_init)).astype(BF16)


def _flash(q_bf, k_bf, v_bf, lams, g_subln, *, batch, seq, lam_init, tq=FLASH_Q):
    nq = seq // tq
    lam_spec = pl.BlockSpec((1, HEAD_DIM), lambda b, h: (0, 0))
    head_spec = pl.BlockSpec((seq, V_DIM), lambda b, h: (b, h))
    return pl.pallas_call(
        functools.partial(_flash_kernel, tq=tq, nq=nq, lam_init=lam_init),
        grid=(batch, N_HEADS),
        in_specs=[head_spec, head_spec, head_spec, lam_spec, lam_spec, lam_spec, lam_spec,
                  pl.BlockSpec((1, V_DIM), lambda b, h: (0, 0))],
        out_specs=head_spec,
        out_shape=jax.ShapeDtypeStruct((batch * seq, ATTN_WIDTH), BF16),
        scratch_shapes=[pltpu.VMEM((V_DIM + BF16_ROWS, seq), BF16),
                        pltpu.VMEM((2, seq, 2 * tq), F32)],
        compiler_params=_cparams(("parallel", "parallel")),
        name="flash_prompt",
    )(q_bf, k_bf, v_bf, *lams, g_subln.reshape(1, -1))


def _decode_kernel(pt_ref, q_ref, kn_ref, vn_ref, ck_hbm, cv_hbm, lq1, lk1, lq2, lk2, g_ref, o_ref,
                   kbuf, vbuf, ksem, vsem, m_scr, l_scr, acc_scr, *, layer, n_pages, n_groups,
                   n_steps, dec_seq, lam_init):
    g = pl.program_id(1)
    t = pl.program_id(0) * n_groups + g
    qh = dec_seq * N_HEADS
    page_rows = PAGE_SIZE * N_HEADS

    def page_copies(step, slot):
        cps = []
        for p in range(n_pages):
            page = pt_ref[step * n_pages + p]
            cps.append(pltpu.make_async_copy(ck_hbm.at[layer, page], kbuf.at[slot, p], ksem.at[slot]))
            cps.append(pltpu.make_async_copy(cv_hbm.at[layer, page], vbuf.at[slot, p], vsem.at[slot]))
        return cps

    ahead = RING_SLOTS - 1

    @pl.when(t == 0)
    def _():
        for s in range(min(ahead, n_steps)):
            for cp in page_copies(s, s):
                cp.start()

    @pl.when(t + ahead < n_steps)
    def _():
        for cp in page_copies(t + ahead, lax.rem(t + ahead, RING_SLOTS)):
            cp.start()

    slot = lax.rem(t, RING_SLOTS)
    for cp in page_copies(t, slot):
        cp.wait()

    @pl.when(g == 0)
    def _():
        m_scr[...] = jnp.full(m_scr.shape, NEG_BIG, F32)
        l_scr[...] = jnp.zeros(l_scr.shape, F32)
        acc_scr[...] = jnp.zeros(acc_scr.shape, F32)

    sub = lax.broadcasted_iota(jnp.int32, (SUBLANES, LANES), 0)
    lane = lax.broadcasted_iota(jnp.int32, (SUBLANES, LANES), 1)
    diag = (lane & (N_HEADS - 1)) == sub
    lane_q = lax.broadcasted_iota(jnp.int32, (qh, V_DIM), 1)
    lam = _lam(lq1, lk1, lq2, lk2, lam_init)

    r = q_ref[...]
    qt = jnp.concatenate([jnp.where(lane_q < HEAD_DIM, r, 0.0),
                          jnp.where(lane_q >= HEAD_DIM, r, 0.0)], axis=0).astype(BF16)

    def scores(k_rows):
        s = lax.dot_general(k_rows.astype(BF16), qt, (((1,), (1,)), ((), ())),
                            preferred_element_type=F32)
        return s.reshape(k_rows.shape[0] // N_HEADS, N_HEADS, LANES)

    def update(s_list, v_list):
        m_prev = m_scr[...]
        m_new = m_prev
        for s in s_list:
            m_new = jnp.maximum(m_new, jnp.max(s, axis=0))
        alpha = jnp.exp(m_prev - m_new)
        m_eff = jnp.where(diag, m_new, -NEG_BIG)
        l_new = alpha * l_scr[...]
        pv = jnp.zeros((V_DIM, LANES), F32)
        for s, v in zip(s_list, v_list):
            p = jnp.exp(s - m_eff[None])
            l_new = l_new + jnp.sum(p, axis=0)
            pm = p.reshape(v.shape[0], LANES).astype(BF16)
            pv = pv + lax.dot_general(v.astype(BF16), pm, (((0,), (0,)), ((), ())),
                                      preferred_element_type=F32)
        alpha_row = jnp.sum(jnp.where(diag, alpha, 0.0), axis=0, keepdims=True)
        acc_scr[...] = acc_scr[...] * alpha_row + pv
        l_scr[...] = l_new
        m_scr[...] = m_new

    update([scores(kbuf[slot, p].reshape(page_rows, V_DIM)) for p in range(n_pages)],
           [vbuf[slot, p].reshape(page_rows, V_DIM) for p in range(n_pages)])

    @pl.when(g == n_groups - 1)
    def _():
        s_new = scores(kn_ref[...])
        pos = lax.broadcasted_iota(jnp.int32, s_new.shape, 0)
        qidx = (lax.broadcasted_iota(jnp.int32, s_new.shape, 2) & (qh - 1)) >> 3
        update([jnp.where(pos <= qidx, s_new, NEG_BIG)], [vn_ref[...]])
        l_row = jnp.sum(jnp.where(diag, l_scr[...], 0.0), axis=0, keepdims=True)
        nt = (acc_scr[...] / l_row).T
        out = nt[:qh] - lam * nt[qh:]
        o_ref[...] = _rms(out, g_ref[...]) * (1.0 - lam_init)


def _decode(q4, k4, v4, cache_k, cache_v, layer, page_table, lams, g_subln, *,
            dec_batch, dec_seq, lam_init, n_pages=DECODE_PAGES):
    pages_per_seq = page_table.shape[1]
    assert pages_per_seq % n_pages == 0
    n_groups = pages_per_seq // n_pages
    qh = dec_seq * N_HEADS
    tok_spec = pl.BlockSpec((qh, V_DIM), lambda b, g, pt: (b, 0))
    lam_spec = pl.BlockSpec((1, HEAD_DIM), lambda b, g, pt: (0, 0))
    hbm_spec = pl.BlockSpec(memory_space=pl.ANY)
    page_buf = pltpu.VMEM((RING_SLOTS, n_pages, PAGE_SIZE, N_HEADS, V_DIM), F32)
    grid_spec = pltpu.PrefetchScalarGridSpec(
        num_scalar_prefetch=1,
        grid=(dec_batch, n_groups),
        in_specs=[tok_spec, tok_spec, tok_spec, hbm_spec, hbm_spec]
        + [lam_spec] * 4 + [pl.BlockSpec((1, V_DIM), lambda b, g, pt: (0, 0))],
        out_specs=tok_spec,
        scratch_shapes=[page_buf, page_buf,
                        pltpu.SemaphoreType.DMA((RING_SLOTS,)), pltpu.SemaphoreType.DMA((RING_SLOTS,)),
                        pltpu.VMEM((SUBLANES, LANES), F32), pltpu.VMEM((SUBLANES, LANES), F32),
                        pltpu.VMEM((V_DIM, LANES), F32)],
    )
    return pl.pallas_call(
        functools.partial(_decode_kernel, layer=layer, n_pages=n_pages, n_groups=n_groups,
                          n_steps=dec_batch * n_groups, dec_seq=dec_seq, lam_init=lam_init),
        grid_spec=grid_spec,
        out_shape=jax.ShapeDtypeStruct((dec_batch * qh, V_DIM), F32),
        compiler_params=_cparams(("arbitrary", "arbitrary")),
        name="decode_attn",
    )(page_table.reshape(-1), q4, k4, v4, cache_k, cache_v, *lams, g_subln.reshape(1, -1))


def _mixer_out_kernel(x_ref, a_ref, u_ref, hist_ref, gate_ref, gpost_ref, wout_ref, wpool_ref,
                      ps_ref, *refs, nb, t, seq_tiles, decode, pos_base):
    if decode:
        o_ref, state_ref, ext_scr, m_scr, pool_scr, win_a, win_b = refs
    else:
        o_ref, ext_scr, m_scr, pool_scr, win_a, win_b = refs
    rows = nb * t
    ext_scr[:, HIST_PAD:HIST_PAD + t, :] = u_ref[...]
    if decode:
        ext_scr[:, HIST_PAD - POOL_HIST:HIST_PAD, :] = hist_ref[...]
        state_ref[...] = ext_scr[:, HIST_PAD + t - POOL_HIST:HIST_PAD + t, :]
        base = pos_base
    else:
        first = (pl.program_id(0) % seq_tiles) == 0
        ext_scr[:, 0:HIST_PAD, :] = jnp.where(first, 0.0, hist_ref[...])
        base = (pl.program_id(0) % seq_tiles) * t

    end = HIST_PAD + t
    pos = base + lax.broadcasted_iota(jnp.int32, (1, t, POOL_GROUP_WIDTH), 1)
    for gi, w in enumerate(POOL_WINDOWS):
        cs = slice(gi * POOL_GROUP_WIDTH, (gi + 1) * POOL_GROUP_WIDTH)
        levels = w.bit_length() - 1
        lo = [HIST_PAD] * (levels + 1)
        for lv in range(levels - 1, 0, -1):
            lo[lv] = lo[lv + 1] - (1 << lv)
        src, dst = None, win_a
        for lv in range(1, levels + 1):
            sh, r0 = 1 << (lv - 1), lo[lv]
            if src is None:
                s = ext_scr[:, r0:end, cs] + ext_scr[:, r0 - sh:end - sh, cs]
            else:
                s = src[:, r0:end, :] + src[:, r0 - sh:end - sh, :]
            if lv < levels:
                dst[:, r0:end, :] = s
                src, dst = dst, (win_b if dst is win_a else win_a)
        cnt = jnp.minimum(w, pos + 1).astype(F32)
        pooled = s / cnt - ext_scr[:, HIST_PAD:end, cs]
        pool_scr[:, cs] = pooled.reshape(rows, POOL_GROUP_WIDTH).astype(BF16)
    if decode:
        a = jnp.concatenate([a_ref[pl.ds(h, rows, stride=N_HEADS), :] for h in range(N_HEADS)],
                            axis=1).astype(BF16)
    else:
        a = a_ref[...]
    m_scr[...] = jnp.dot(a, wout_ref[0:ATTN_WIDTH, :], preferred_element_type=F32)

    ys = [jnp.dot(pool_scr[:, gi * POOL_GROUP_WIDTH:(gi + 1) * POOL_GROUP_WIDTH], wpool_ref[gi],
                  preferred_element_type=F32) for gi in range(len(POOL_WINDOWS))]
    y = jnp.concatenate(ys, axis=1) * ps_ref[...]

    m_scr[...] += jnp.dot(y.astype(BF16), wout_ref[ATTN_WIDTH:, :], preferred_element_type=F32)
    _gated_residual_into(o_ref, x_ref, m_scr, gpost_ref, gate_ref)


def _mixer_out(x, a, u3, hist, mod, g_post, w_out_bf, w_pool_bf, pool_scale, *,
               nb, t, seq_tiles, decode, pos_base=0, layer=0):
    rows_total = x.shape[0]
    tm = nb * t
    nt = rows_total // tm
    out_specs = pl.BlockSpec((tm, D_MODEL), lambda i: (i, 0))
    out_shape = jax.ShapeDtypeStruct((rows_total, D_MODEL), F32)
    if decode:
        a_spec = pl.BlockSpec((tm * N_HEADS, V_DIM), lambda i: (i, 0))
        u_spec = pl.BlockSpec((nb, t, POOL_WIDTH), lambda i: (i, 0, 0))
        hist_spec = pl.BlockSpec((None, nb, POOL_HIST, POOL_WIDTH), lambda i: (layer, i, 0, 0))
        gate_spec = pl.BlockSpec((nb, D_MODEL), lambda i: (i, 2))
        out_specs = [out_specs, pl.BlockSpec((nb, POOL_HIST, POOL_WIDTH), lambda i: (i, 0, 0))]
        out_shape = [out_shape, jax.ShapeDtypeStruct((nt * nb, POOL_HIST, POOL_WIDTH), F32)]
    else:
        a_spec = pl.BlockSpec((tm, ATTN_WIDTH), lambda i: (i, 0))
        u_spec = pl.BlockSpec((1, t, POOL_WIDTH), lambda i: (i // seq_tiles, i % seq_tiles, 0))
        hpt = t // HIST_PAD
        hist_spec = pl.BlockSpec(
            (1, HIST_PAD, POOL_WIDTH),
            lambda i: (i // seq_tiles, jnp.maximum((i % seq_tiles) * hpt - 1, 0), 0))
        gate_spec = pl.BlockSpec((None, 1, D_MODEL), lambda i: (i // seq_tiles, 0, 2))
    return pl.pallas_call(
        functools.partial(_mixer_out_kernel, nb=nb, t=t, seq_tiles=seq_tiles, decode=decode,
                          pos_base=pos_base),
        grid=(nt,),
        in_specs=[
            pl.BlockSpec((tm, D_MODEL), lambda i: (i, 0)),
            a_spec, u_spec, hist_spec, gate_spec,
            pl.BlockSpec((1, D_MODEL), lambda i: (0, 0)),
            pl.BlockSpec((D_MODEL, D_MODEL), lambda i: (0, 0)),
            pl.BlockSpec((len(POOL_WINDOWS), POOL_GROUP_WIDTH, POOL_GROUP_WIDTH), lambda i: (0, 0, 0)),
            pl.BlockSpec((1, POOL_WIDTH), lambda i: (0, 0)),
        ],
        out_specs=out_specs,
        out_shape=out_shape,
        scratch_shapes=[pltpu.VMEM((nb, HIST_PAD + t, POOL_WIDTH), F32),
                        pltpu.VMEM((tm, D_MODEL), F32), pltpu.VMEM((tm, POOL_WIDTH), BF16),
                        pltpu.VMEM((nb, HIST_PAD + t, POOL_GROUP_WIDTH), F32),
                        pltpu.VMEM((nb, HIST_PAD + t, POOL_GROUP_WIDTH), F32)],
        compiler_params=_cparams(("parallel",)),
        name="mixer_out_decode" if decode else "mixer_out_prompt",
    )(x, a, u3, hist, mod, g_post.reshape(1, -1), w_out_bf, w_pool_bf, pool_scale.reshape(1, -1))


def _ffn_kernel(x_ref, shift_ref, scale_ref, gate_ref, gpre_ref, gpost_ref, w1_ref, w2_ref,
                *refs, n_f, decode):
    i = pl.program_id(0)
    j = pl.program_id(1)
    if decode:
        o_ref, w1bf_ref, w2bf_ref, h_scr, acc_scr = refs
        w1bf_ref[...] = w1_ref[...].astype(BF16)
        w2bf_ref[...] = w2_ref[...].astype(BF16)
        w1_ref, w2_ref = w1bf_ref, w2bf_ref

        @pl.when(j == 0)
        def _():
            _modulate_into(h_scr, x_ref, gpre_ref, shift_ref, scale_ref)
    else:
        xn_ref, shiftn_ref, scalen_ref, o_ref, h_scr, hb_scr, acc_scr = refs

        @pl.when((i == 0) & (j == 0))
        def _():
            _modulate_into(h_scr, x_ref, gpre_ref, shift_ref, scale_ref)

    @pl.when(j == 0)
    def _():
        acc_scr[...] = jnp.zeros(acc_scr.shape, F32)

    def step(h_cur, h_next):
        t = jnp.dot(h_cur[...], w1_ref[...], preferred_element_type=F32)
        t = jnp.square(jnp.maximum(t, 0.0))
        acc_scr[...] += jnp.dot(t.astype(BF16), w2_ref[...], preferred_element_type=F32)
        if h_next is not None:
            per_step = x_ref.shape[0] // n_f
            for r in range(0, per_step, ROW_CHUNK):
                rows = pl.ds(pl.multiple_of(j * per_step + r, ROW_CHUNK), ROW_CHUNK)
                h = (_rms(xn_ref[rows, :], gpre_ref[...]) * (1.0 + scalen_ref[...]) + shiftn_ref[...])
                h_next[rows, :] = h.astype(BF16)

    if decode:
        step(h_scr, None)
    else:
        even = lax.rem(i, 2) == 0
        pl.when(even)(functools.partial(step, h_scr, hb_scr))
        pl.when(jnp.logical_not(even))(functools.partial(step, hb_scr, h_scr))

    @pl.when(j == n_f - 1)
    def _():
        _gated_residual_into(o_ref, x_ref, acc_scr, gpost_ref, gate_ref)


def _ffn(x, mod, g_pre, g_post, w1, w2, *, tm, tf, seq_tiles, decode):
    rows = x.shape[0]
    nt = rows // tm
    n_f = D_FF // tf
    row_mode = dict(pipeline_mode=pl.Buffered(1)) if nt == 1 else {}
    x_spec = pl.BlockSpec((tm, D_MODEL), lambda i, j: (i, 0), **row_mode)
    out_specs = pl.BlockSpec((tm, D_MODEL), lambda i, j: (i, 0), **row_mode)
    out_shape = jax.ShapeDtypeStruct((rows, D_MODEL), F32)
    h_buf = pltpu.VMEM((tm, D_MODEL), BF16)
    scratch = ([h_buf] if decode else [h_buf, h_buf]) + [pltpu.VMEM((tm, D_MODEL), F32)]
    extra_in, extra_args = [], []
    if decode:
        assert nt == 1 and W1_BLOCK % tf == 0
        mod_spec = lambda c: pl.BlockSpec((tm // SUBLANES, D_MODEL), lambda i, j: (i, c))
        w1_spec = pl.BlockSpec((D_MODEL, tf), lambda i, j: (0, j))
        per = W1_BLOCK // tf
        out_specs = [out_specs,
                     pl.BlockSpec((None, D_MODEL, tf), lambda i, j: (j // per, 0, j % per)),
                     pl.BlockSpec((tf, D_MODEL), lambda i, j: (j, 0))]
        out_shape = [out_shape, jax.ShapeDtypeStruct((D_FF // W1_BLOCK, D_MODEL, W1_BLOCK), BF16),
                     jax.ShapeDtypeStruct((D_FF, D_MODEL), BF16)]
    else:
        assert tm % (n_f * ROW_CHUNK) == 0 and tf == W1_BLOCK
        w1_spec = pl.BlockSpec((None, D_MODEL, tf), lambda i, j: (j, 0, 0))
        mod_spec = lambda c: pl.BlockSpec((None, 1, D_MODEL), lambda i, j: (i // seq_tiles, 0, c))
        nxt = lambda i: jnp.minimum(i + 1, nt - 1)
        mod_next = lambda c: pl.BlockSpec((None, 1, D_MODEL), lambda i, j: (nxt(i) // seq_tiles, 0, c))
        extra_in = [pl.BlockSpec((tm, D_MODEL), lambda i, j: (nxt(i), 0)), mod_next(3), mod_next(4)]
        extra_args = [x, mod, mod]
    vec_spec = pl.BlockSpec((1, D_MODEL), lambda i, j: (0, 0))
    return pl.pallas_call(
        functools.partial(_ffn_kernel, n_f=n_f, decode=decode),
        grid=(nt, n_f),
        in_specs=[
            x_spec,
            mod_spec(3), mod_spec(4), mod_spec(5),
            vec_spec, vec_spec,
            w1_spec,
            pl.BlockSpec((tf, D_MODEL), lambda i, j: (j, 0)),
        ] + extra_in,
        out_specs=out_specs,
        out_shape=out_shape,
        scratch_shapes=scratch,
        compiler_params=_cparams(("arbitrary", "arbitrary")),
        name="ffn_decode" if decode else "ffn_prompt",
    )(x, mod, mod, mod, g_pre.reshape(1, -1), g_post.reshape(1, -1), w1, w2, *extra_args)


def kernel(x_prompt, x_sample, cache_k, cache_v, state_pool, page_table, c_prompt, c_sample,
           w_ada, b_ada, g_pre_mix, w_in, lambda_q1, lambda_k1, lambda_q2, lambda_k2, g_subln,
           w_pool, pool_scale, w_out, g_post_mix, g_pre_ffn, w_ff1, w_ff2, g_post_ffn):
    batch, seq, _ = x_prompt.shape
    dec_batch, dec_seq, _ = x_sample.shape
    depth = w_in.shape[0]
    past_len = page_table.shape[1] * PAGE_SIZE
    assert dec_seq == SUBLANES and state_pool.shape[2] == POOL_HIST

    tm_p = ROW_TILE
    seq_tiles = seq // tm_p
    rows_s = dec_batch * dec_seq
    tables_p = _rope_tables(np.arange(seq))
    tables_s = _rope_tables(past_len + (np.arange(rows_s) % dec_seq))

    n_c = batch + dec_batch
    c_all = jnp.concatenate(
        [c_sample, c_prompt, jnp.zeros((-n_c % SUBLANES, D_MODEL), F32)], axis=0)

    xp = x_prompt.reshape(batch * seq, D_MODEL)
    xs = x_sample.reshape(rows_s, D_MODEL)
    outs = [[] for _ in range(6)]
    for l in range(depth):
        lam_init = 0.8 - 0.6 * math.exp(-0.3 * l)
        lams = [v[l].reshape(1, HEAD_DIM) for v in (lambda_q1, lambda_k1, lambda_q2, lambda_k2)]
        w_out_bf = w_out[l].astype(BF16)
        w_pool_bf = w_pool[l].astype(BF16)

        m_all = _ada(c_all, w_ada[l], b_ada[l])
        mod_p = m_all[dec_batch:n_c].reshape(batch, 1, N_MOD * D_MODEL)
        mod_s = m_all

        q4s, k4s, v4s, us, w_in_bf = _mixer_in(
            xs, mod_s, g_pre_mix[l], w_in[l], tables_s, tm=tm_p, seq_tiles=1, decode=True)
        a4s = _decode(q4s, k4s, v4s, cache_k, cache_v, l, page_table, lams, g_subln[l],
                      dec_batch=dec_batch, dec_seq=dec_seq, lam_init=lam_init)
        us3 = us.reshape(dec_batch, dec_seq, POOL_WIDTH)
        xs, pool_s = _mixer_out(xs, a4s, us3, state_pool, mod_s, g_post_mix[l], w_out_bf, w_pool_bf,
                                pool_scale[l], nb=tm_p // (2 * dec_seq), t=dec_seq, seq_tiles=1,
                                decode=True, pos_base=past_len, layer=l)
        xs, w1_bf, w2_bf = _ffn(xs, mod_s, g_pre_ffn[l], g_post_ffn[l], w_ff1[l], w_ff2[l],
                                tm=rows_s, tf=FFN_DECODE_COLS, seq_tiles=1, decode=True)
        outs[3].append(k4s.reshape(dec_batch, dec_seq, N_HEADS, V_DIM))
        outs[4].append(v4s.reshape(dec_batch, dec_seq, N_HEADS, V_DIM))
        outs[5].append(pool_s)

        q_bf, k4, k_bf, v4, v_bf, u = _mixer_in(
            xp, mod_p, g_pre_mix[l], w_in_bf, tables_p, tm=tm_p, seq_tiles=seq_tiles, decode=False)
        a_bf = _flash(q_bf, k_bf, v_bf, lams, g_subln[l], batch=batch, seq=seq, lam_init=lam_init)
        u3 = u.reshape(batch, seq, POOL_WIDTH)
        xp = _mixer_out(xp, a_bf, u3, u3, mod_p, g_post_mix[l], w_out_bf, w_pool_bf, pool_scale[l],
                        nb=1, t=tm_p, seq_tiles=seq_tiles, decode=False)
        xp = _ffn(xp, mod_p, g_pre_ffn[l], g_post_ffn[l], w1_bf, w2_bf,
                  tm=tm_p, tf=W1_BLOCK, seq_tiles=seq_tiles, decode=False)
        outs[0].append(k4.reshape(batch, seq, N_HEADS, V_DIM))
        outs[1].append(v4.reshape(batch, seq, N_HEADS, V_DIM))
        outs[2].append(u3[:, seq - POOL_HIST:])

    kp, vp, sp, ks, vs, ss = (jnp.stack(o) for o in outs)
    return (xp.reshape(batch, seq, D_MODEL), xs.reshape(dec_batch, dec_seq, D_MODEL),
            kp, vp, sp, ks, vs, ss)
```

```python
import functools
import math

import jax
import jax.numpy as jnp
import numpy as np
from jax import lax
from jax.experimental import pallas as pl
from jax.experimental.pallas import tpu as pltpu

F32 = jnp.float32
BF16 = jnp.bfloat16

D_MODEL = 2048
N_HEADS = 8
HEAD_DIM = 64
V_DIM = 2 * HEAD_DIM
ATTN_WIDTH = N_HEADS * V_DIM
POOL_WIDTH = 1024
IN_WIDTH = 3 * ATTN_WIDTH + POOL_WIDTH
ROT_DIM = HEAD_DIM // 4
ROPE_THETA = 500000.0
POOL_WINDOWS = (2, 4, 8, 16)
POOL_GROUP_WIDTH = POOL_WIDTH // len(POOL_WINDOWS)
POOL_HIST = max(POOL_WINDOWS) - 1
HIST_PAD = 16
D_FF = 4 * D_MODEL
N_MOD = 6
PAGE_SIZE = 128
EPS = 1e-6
NEG_BIG = -1e30
LOG2_E = math.log2(math.e)

LANES = 128
SUBLANES = 8
BF16_ROWS = 16
VMEM_LIMIT = 60 * 1024 * 1024

ROW_TILE = 512
ROW_CHUNK = 16
ADA_COLS = 1024
FLASH_Q = 256
KV_CHUNK = 512
DECODE_PAGES = 8
RING_SLOTS = 4
W1_BLOCK = 1024
FFN_DECODE_COLS = 512


def _cparams(sem, **kwargs):
    return pltpu.CompilerParams(dimension_semantics=sem, vmem_limit_bytes=VMEM_LIMIT, **kwargs)


def _rms(x, g):
    ms = jnp.mean(x * x, axis=-1, keepdims=True)
    return x * lax.rsqrt(ms + EPS) * g


def _mod_rows(ref, r0, nrows, tile_rows):
    n, d = ref.shape
    if n == 1:
        return ref[...]
    per = tile_rows // n
    m = ref[r0 // per:(r0 + nrows) // per, :]
    return jnp.broadcast_to(m[:, None, :], (nrows // per, per, d)).reshape(nrows, d)


def _modulate_into(h_ref, x_ref, g_ref, shift_ref, scale_ref):
    tm = x_ref.shape[0]
    for r in range(0, tm, ROW_CHUNK):
        h = (_rms(x_ref[r:r + ROW_CHUNK, :], g_ref[...])
             * (1.0 + _mod_rows(scale_ref, r, ROW_CHUNK, tm)) + _mod_rows(shift_ref, r, ROW_CHUNK, tm))
        h_ref[r:r + ROW_CHUNK, :] = h.astype(BF16)


def _gated_residual_into(o_ref, x_ref, m_ref, g_ref, gate_ref):
    tm = x_ref.shape[0]
    for r in range(0, tm, ROW_CHUNK):
        rows = slice(r, r + ROW_CHUNK)
        o_ref[rows, :] = x_ref[rows, :] + _mod_rows(gate_ref, r, ROW_CHUNK, tm) * _rms(m_ref[rows, :], g_ref[...])


def _ada_kernel(c_ref, w_ref, b_ref, o_ref):
    c = c_ref[...]
    s = c * (1.0 / (1.0 + jnp.exp(-c)))
    o_ref[...] = jnp.dot(s.astype(BF16), w_ref[...].astype(BF16),
                         preferred_element_type=F32) + b_ref[...]


def _ada(c_all, w_ada, b_ada):
    rows = c_all.shape[0]
    tn = ADA_COLS
    return pl.pallas_call(
        _ada_kernel,
        grid=(N_MOD * D_MODEL // tn,),
        in_specs=[
            pl.BlockSpec((rows, D_MODEL), lambda j: (0, 0)),
            pl.BlockSpec((D_MODEL, tn), lambda j: (0, j)),
            pl.BlockSpec((1, tn), lambda j: (0, j)),
        ],
        out_specs=pl.BlockSpec((rows, tn), lambda j: (0, j)),
        out_shape=jax.ShapeDtypeStruct((rows, N_MOD * D_MODEL), F32),
        compiler_params=_cparams(("arbitrary",)),
        name="ada",
    )(c_all, w_ada, b_ada.reshape(1, -1))


def _rope(z, cos_ref, sa_ref, sb_ref):
    return (z * cos_ref[...]
            + pltpu.roll(z, LANES - ROT_DIM // 2, 1) * sa_ref[...]
            + pltpu.roll(z, ROT_DIM // 2, 1) * sb_ref[...])


def _mixer_in_kernel(x_ref, shift_ref, scale_ref, g_ref, w_ref, cos_ref, sa_ref, sb_ref,
                     *refs, tm, decode):
    j = pl.program_id(1)
    if decode:
        q4_ref, k4_ref, v4_ref, u_ref, wbf_ref, h_scr, wc_scr = refs
        wc_scr[...] = w_ref[...].astype(BF16)

        @pl.when(pl.program_id(0) == 0)
        def _():
            wbf_ref[...] = wc_scr[...]
    else:
        qb_ref, k4_ref, kb_ref, v4_ref, vb_ref, u_ref, h_scr = refs

    def weights(sec, lo, hi):
        if decode:
            return wc_scr[:, lo:hi]
        return w_ref[:, sec * ATTN_WIDTH + lo:sec * ATTN_WIDTH + hi]

    def heads(sec):
        for hp in range(N_HEADS // 2):
            z2 = jnp.dot(h_scr[...], weights(sec, 2 * hp * V_DIM, 2 * (hp + 1) * V_DIM),
                         preferred_element_type=F32)
            for h in (2 * hp, 2 * hp + 1):
                yield h, slice(h * V_DIM, (h + 1) * V_DIM), z2[:, (h % 2) * V_DIM:(h % 2 + 1) * V_DIM]

    def section(sec):
        if sec == 0:
            for h, cols, zh in heads(sec):
                q = _rope(zh, cos_ref, sa_ref, sb_ref) * (HEAD_DIM ** -0.5)
                if decode:
                    q4_ref[pl.ds(h, tm, stride=N_HEADS), :] = q
                else:
                    qb_ref[:, cols] = (q * LOG2_E).astype(BF16)
        elif sec == 1:
            for h, cols, zh in heads(sec):
                k = _rope(zh, cos_ref, sa_ref, sb_ref)
                k4_ref[pl.ds(h, tm, stride=N_HEADS), :] = k
                if not decode:
                    kb_ref[:, cols] = k.astype(BF16)
        elif sec == 2:
            for h, cols, zh in heads(sec):
                v4_ref[pl.ds(h, tm, stride=N_HEADS), :] = zh
                if not decode:
                    vb_ref[:, cols] = zh.astype(BF16)
        else:
            u_ref[...] = jnp.dot(h_scr[...], weights(sec, 0, POOL_WIDTH), preferred_element_type=F32)

    n_sec = IN_WIDTH // ATTN_WIDTH
    if decode:
        pl.when(j == 0)(functools.partial(_modulate_into, h_scr, x_ref, g_ref, shift_ref, scale_ref))
        for sec in range(n_sec):
            pl.when(j == sec)(functools.partial(section, sec))
    else:
        _modulate_into(h_scr, x_ref, g_ref, shift_ref, scale_ref)
        for sec in range(n_sec):
            section(sec)


def _mixer_in(x, mod, g_pre, w_in, tables, *, tm, seq_tiles, decode):
    rows = x.shape[0]
    nt = rows // tm
    if decode:
        mod_spec = lambda c: pl.BlockSpec((tm // SUBLANES, D_MODEL), lambda i, j: (i, c))
        tab_spec = pl.BlockSpec((tm, LANES), lambda i, j: (i, 0))
    else:
        mod_spec = lambda c: pl.BlockSpec((None, 1, D_MODEL), lambda i, j: (i // seq_tiles, 0, c))
        tab_spec = pl.BlockSpec((tm, LANES), lambda i, j: (i % seq_tiles, 0))
    row_bf = pl.BlockSpec((tm, ATTN_WIDTH), lambda i, j: (i, 0))
    row4 = pl.BlockSpec((tm * N_HEADS, V_DIM), lambda i, j: (i, 0))
    sd_bf = jax.ShapeDtypeStruct((rows, ATTN_WIDTH), BF16)
    sd4 = jax.ShapeDtypeStruct((rows * N_HEADS, V_DIM), F32)
    sd_u = jax.ShapeDtypeStruct((rows, POOL_WIDTH), F32)
    n_sec = IN_WIDTH // ATTN_WIDTH
    scratch = [pltpu.VMEM((tm, D_MODEL), BF16)]
    if decode:
        w_out_spec = pl.BlockSpec((D_MODEL, ATTN_WIDTH),
                                  lambda i, j: (0, jnp.where(i == 0, j, n_sec - 1)))
        out_specs = [row4, row4, row4, row_bf, w_out_spec]
        out_shape = [sd4, sd4, sd4, sd_u, jax.ShapeDtypeStruct((D_MODEL, IN_WIDTH), BF16)]
        scratch.append(pltpu.VMEM((D_MODEL, ATTN_WIDTH), BF16))
        steps = n_sec
        w_spec = pl.BlockSpec((D_MODEL, ATTN_WIDTH), lambda i, j: (0, j))
    else:
        out_specs = [row_bf, row4, row_bf, row4, row_bf, row_bf]
        out_shape = [sd_bf, sd4, sd_bf, sd4, sd_bf, sd_u]
        steps = 1
        w_spec = pl.BlockSpec((D_MODEL, IN_WIDTH), lambda i, j: (0, 0), pipeline_mode=pl.Buffered(1))
    return pl.pallas_call(
        functools.partial(_mixer_in_kernel, tm=tm, decode=decode),
        grid=(nt, steps),
        in_specs=[
            pl.BlockSpec((tm, D_MODEL), lambda i, j: (i, 0)),
            mod_spec(0), mod_spec(1),
            pl.BlockSpec((1, D_MODEL), lambda i, j: (0, 0)),
            w_spec,
            tab_spec, tab_spec, tab_spec,
        ],
        out_specs=out_specs,
        out_shape=out_shape,
        scratch_shapes=scratch,
        compiler_params=_cparams(("arbitrary", "arbitrary")),
        name="mixer_in_decode" if decode else "mixer_in_prompt",
    )(x, mod, mod, g_pre.reshape(1, -1), w_in, *tables)


def _rope_tables(pos):
    half = ROT_DIM // 2
    inv = 1.0 / (ROPE_THETA ** (np.arange(0, ROT_DIM, 2, dtype=np.float64) / ROT_DIM))
    ang = np.asarray(pos, np.float64)[:, None] * inv[None, :]
    cos, sin = np.cos(ang), np.sin(ang)
    lane = np.arange(LANES) % HEAD_DIM
    first = (lane < half)[None, :]
    second = ((lane >= half) & (lane < ROT_DIM))[None, :]
    cos_l = cos[:, lane % half]
    sin_l = sin[:, lane % half]
    c = np.where(first | second, cos_l, 1.0)
    sa = np.where(first, -sin_l, 0.0)
    sb = np.where(second, sin_l, 0.0)
    return tuple(jnp.asarray(t, F32) for t in (c, sa, sb))


def _lam(lq1, lk1, lq2, lk2, lam_init):
    a = jnp.sum(lq1[...] * lk1[...], axis=-1, keepdims=True)
    b = jnp.sum(lq2[...] * lk2[...], axis=-1, keepdims=True)
    return jnp.exp(a) - jnp.exp(b) + lam_init


def _flash_kernel(q_ref, k_ref, v_ref, lq1, lk1, lq2, lk2, g_ref, o_ref,
                  vt_scr, s_scr, *, tq, nq, lam_init):
    seq = nq * tq
    vq = 2 * tq
    vt_scr[0:V_DIM, :] = v_ref[...].astype(F32).T.astype(BF16)
    vt_scr[V_DIM:, :] = jnp.ones((BF16_ROWS, seq), BF16)
    lam = _lam(lq1, lk1, lq2, lk2, lam_init)
    lane = lax.broadcasted_iota(jnp.int32, (tq, V_DIM), 1)
    kv = lax.broadcasted_iota(jnp.int32, (tq, vq), 0)
    qp = lax.broadcasted_iota(jnp.int32, (tq, vq), 1) & (tq - 1)
    causal = kv <= qp

    def stacked_queries(ii):
        q = q_ref[ii * tq:(ii + 1) * tq, :].astype(F32)
        return jnp.concatenate([jnp.where(lane < HEAD_DIM, q, 0.0),
                                jnp.where(lane >= HEAD_DIM, q, 0.0)], axis=0).astype(BF16)

    def kv_chunks(ii):
        out, r = [], 0
        while r < ii * tq:
            size = min(KV_CHUNK, ii * tq - r)
            out.append((r, size, False))
            r += size
        return out + [(ii * tq, tq, True)]

    def score_pass(ii):
        qq = stacked_queries(ii)
        mx = jnp.full((SUBLANES, vq), NEG_BIG, F32)
        for r, size, masked in kv_chunks(ii):
            s = lax.dot_general(k_ref[r:r + size, :], qq, (((1,), (1,)), ((), ())),
                                preferred_element_type=F32)
            if masked:
                s = jnp.where(causal, s, NEG_BIG)
            s_scr[ii % 2, r:r + size, :] = s
            mx = jnp.maximum(mx, jnp.max(s.reshape(size // SUBLANES, SUBLANES, vq), axis=0))
        return jnp.max(mx, axis=0, keepdims=True)

    def value_pass(ii, m):
        o_t = jnp.zeros((V_DIM + BF16_ROWS, vq), F32)
        for r, size, _ in kv_chunks(ii):
            p = jnp.exp2(s_scr[ii % 2, r:r + size, :] - m)
            o_t = o_t + jnp.dot(vt_scr[:, r:r + size], p.astype(BF16), preferred_element_type=F32)
        return o_t

    m = score_pass(0)
    for ii in range(nq):
        m_next = score_pass(ii + 1) if ii + 1 < nq else None
        o_t = value_pass(ii, m)
        m = m_next
        n_t = o_t[:V_DIM] / o_t[V_DIM:V_DIM + 1]
        out = (n_t[:, :tq] - lam * n_t[:, tq:]).T
        o_ref[ii * tq:(ii + 1) * tq, :] = (_rms(out, g_ref[...]) * (1.0 - lam_init)).astype(BF16)


def _flash(q_bf, k_bf, v_bf, lams, g_subln, *, batch, seq, lam_init, tq=FLASH_Q):
    nq = seq // tq
    lam_spec = pl.BlockSpec((1, HEAD_DIM), lambda b, h: (0, 0))
    head_spec = pl.BlockSpec((seq, V_DIM), lambda b, h: (b, h))
    return pl.pallas_call(
        functools.partial(_flash_kernel, tq=tq, nq=nq, lam_init=lam_init),
        grid=(batch, N_HEADS),
        in_specs=[head_spec, head_spec, head_spec, lam_spec, lam_spec, lam_spec, lam_spec,
                  pl.BlockSpec((1, V_DIM), lambda b, h: (0, 0))],
        out_specs=head_spec,
        out_shape=jax.ShapeDtypeStruct((batch * seq, ATTN_WIDTH), BF16),
        scratch_shapes=[pltpu.VMEM((V_DIM + BF16_ROWS, seq), BF16),
                        pltpu.VMEM((2, seq, 2 * tq), F32)],
        compiler_params=_cparams(("parallel", "parallel")),
        name="flash_prompt",
    )(q_bf, k_bf, v_bf, *lams, g_subln.reshape(1, -1))


def _decode_kernel(pt_ref, q_ref, kn_ref, vn_ref, ck_hbm, cv_hbm, lq1, lk1, lq2, lk2, g_ref, o_ref,
                   kbuf, vbuf, ksem, vsem, m_scr, l_scr, acc_scr, *, layer, n_pages, n_groups,
                   n_steps, dec_seq, lam_init):
    g = pl.program_id(1)
    t = pl.program_id(0) * n_groups + g
    qh = dec_seq * N_HEADS
    page_rows = PAGE_SIZE * N_HEADS

    def page_copies(step, slot):
        cps = []
        for p in range(n_pages):
            page = pt_ref[step * n_pages + p]
            cps.append(pltpu.make_async_copy(ck_hbm.at[layer, page], kbuf.at[slot, p], ksem.at[slot]))
            cps.append(pltpu.make_async_copy(cv_hbm.at[layer, page], vbuf.at[slot, p], vsem.at[slot]))
        return cps

    ahead = RING_SLOTS - 1

    @pl.when(t == 0)
    def _():
        for s in range(min(ahead, n_steps)):
            for cp in page_copies(s, s):
                cp.start()

    @pl.when(t + ahead < n_steps)
    def _():
        for cp in page_copies(t + ahead, lax.rem(t + ahead, RING_SLOTS)):
            cp.start()

    slot = lax.rem(t, RING_SLOTS)
    for cp in page_copies(t, slot):
        cp.wait()

    @pl.when(g == 0)
    def _():
        m_scr[...] = jnp.full(m_scr.shape, NEG_BIG, F32)
        l_scr[...] = jnp.zeros(l_scr.shape, F32)
        acc_scr[...] = jnp.zeros(acc_scr.shape, F32)

    sub = lax.broadcasted_iota(jnp.int32, (SUBLANES, LANES), 0)
    lane = lax.broadcasted_iota(jnp.int32, (SUBLANES, LANES), 1)
    diag = (lane & (N_HEADS - 1)) == sub
    lane_q = lax.broadcasted_iota(jnp.int32, (qh, V_DIM), 1)
    lam = _lam(lq1, lk1, lq2, lk2, lam_init)

    r = q_ref[...]
    qt = jnp.concatenate([jnp.where(lane_q < HEAD_DIM, r, 0.0),
                          jnp.where(lane_q >= HEAD_DIM, r, 0.0)], axis=0).astype(BF16)

    def scores(k_rows):
        s = lax.dot_general(k_rows.astype(BF16), qt, (((1,), (1,)), ((), ())),
                            preferred_element_type=F32)
        return s.reshape(k_rows.shape[0] // N_HEADS, N_HEADS, LANES)

    def update(s_list, v_list):
        m_prev = m_scr[...]
        m_new = m_prev
        for s in s_list:
            m_new = jnp.maximum(m_new, jnp.max(s, axis=0))
        alpha = jnp.exp(m_prev - m_new)
        m_eff = jnp.where(diag, m_new, -NEG_BIG)
        l_new = alpha * l_scr[...]
        pv = jnp.zeros((V_DIM, LANES), F32)
        for s, v in zip(s_list, v_list):
            p = jnp.exp(s - m_eff[None])
            l_new = l_new + jnp.sum(p, axis=0)
            pm = p.reshape(v.shape[0], LANES).astype(BF16)
            pv = pv + lax.dot_general(v.astype(BF16), pm, (((0,), (0,)), ((), ())),
                                      preferred_element_type=F32)
        alpha_row = jnp.sum(jnp.where(diag, alpha, 0.0), axis=0, keepdims=True)
        acc_scr[...] = acc_scr[...] * alpha_row + pv
        l_scr[...] = l_new
        m_scr[...] = m_new

    update([scores(kbuf[slot, p].reshape(page_rows, V_DIM)) for p in range(n_pages)],
           [vbuf[slot, p].reshape(page_rows, V_DIM) for p in range(n_pages)])

    @pl.when(g == n_groups - 1)
    def _():
        s_new = scores(kn_ref[...])
        pos = lax.broadcasted_iota(jnp.int32, s_new.shape, 0)
        qidx = (lax.broadcasted_iota(jnp.int32, s_new.shape, 2) & (qh - 1)) >> 3
        update([jnp.where(pos <= qidx, s_new, NEG_BIG)], [vn_ref[...]])
        l_row = jnp.sum(jnp.where(diag, l_scr[...], 0.0), axis=0, keepdims=True)
        nt = (acc_scr[...] / l_row).T
        out = nt[:qh] - lam * nt[qh:]
        o_ref[...] = _rms(out, g_ref[...]) * (1.0 - lam_init)


def _decode(q4, k4, v4, cache_k, cache_v, layer, page_table, lams, g_subln, *,
            dec_batch, dec_seq, lam_init, n_pages=DECODE_PAGES):
    pages_per_seq = page_table.shape[1]
    assert pages_per_seq % n_pages == 0
    n_groups = pages_per_seq // n_pages
    qh = dec_seq * N_HEADS
    tok_spec = pl.BlockSpec((qh, V_DIM), lambda b, g, pt: (b, 0))
    lam_spec = pl.BlockSpec((1, HEAD_DIM), lambda b, g, pt: (0, 0))
    hbm_spec = pl.BlockSpec(memory_space=pl.ANY)
    page_buf = pltpu.VMEM((RING_SLOTS, n_pages, PAGE_SIZE, N_HEADS, V_DIM), F32)
    grid_spec = pltpu.PrefetchScalarGridSpec(
        num_scalar_prefetch=1,
        grid=(dec_batch, n_groups),
        in_specs=[tok_spec, tok_spec, tok_spec, hbm_spec, hbm_spec]
        + [lam_spec] * 4 + [pl.BlockSpec((1, V_DIM), lambda b, g, pt: (0, 0))],
        out_specs=tok_spec,
        scratch_shapes=[page_buf, page_buf,
                        pltpu.SemaphoreType.DMA((RING_SLOTS,)), pltpu.SemaphoreType.DMA((RING_SLOTS,)),
                        pltpu.VMEM((SUBLANES, LANES), F32), pltpu.VMEM((SUBLANES, LANES), F32),
                        pltpu.VMEM((V_DIM, LANES), F32)],
    )
    return pl.pallas_call(
        functools.partial(_decode_kernel, layer=layer, n_pages=n_pages, n_groups=n_groups,
                          n_steps=dec_batch * n_groups, dec_seq=dec_seq, lam_init=lam_init),
        grid_spec=grid_spec,
        out_shape=jax.ShapeDtypeStruct((dec_batch * qh, V_DIM), F32),
        compiler_params=_cparams(("arbitrary", "arbitrary")),
        name="decode_attn",
    )(page_table.reshape(-1), q4, k4, v4, cache_k, cache_v, *lams, g_subln.reshape(1, -1))


def _mixer_out_kernel(x_ref, a_ref, u_ref, hist_ref, gate_ref, gpost_ref, wout_ref, wpool_ref,
                      ps_ref, *refs, nb, t, seq_tiles, decode, pos_base):
    if decode:
        o_ref, state_ref, ext_scr, m_scr, pool_scr, win_a, win_b = refs
    else:
        o_ref, ext_scr, m_scr, pool_scr, win_a, win_b = refs
    rows = nb * t
    ext_scr[:, HIST_PAD:HIST_PAD + t, :] = u_ref[...]
    if decode:
        ext_scr[:, HIST_PAD - POOL_HIST:HIST_PAD, :] = hist_ref[...]
        state_ref[...] = ext_scr[:, HIST_PAD + t - POOL_HIST:HIST_PAD + t, :]
        base = pos_base
    else:
        first = (pl.program_id(0) % seq_tiles) == 0
        ext_scr[:, 0:HIST_PAD, :] = jnp.where(first, 0.0, hist_ref[...])
        base = (pl.program_id(0) % seq_tiles) * t

    end = HIST_PAD + t
    pos = base + lax.broadcasted_iota(jnp.int32, (1, t, POOL_GROUP_WIDTH), 1)
    for gi, w in enumerate(POOL_WINDOWS):
        cs = slice(gi * POOL_GROUP_WIDTH, (gi + 1) * POOL_GROUP_WIDTH)
        levels = w.bit_length() - 1
        lo = [HIST_PAD] * (levels + 1)
        for lv in range(levels - 1, 0, -1):
            lo[lv] = lo[lv + 1] - (1 << lv)
        src, dst = None, win_a
        for lv in range(1, levels + 1):
            sh, r0 = 1 << (lv - 1), lo[lv]
            if src is None:
                s = ext_scr[:, r0:end, cs] + ext_scr[:, r0 - sh:end - sh, cs]
            else:
                s = src[:, r0:end, :] + src[:, r0 - sh:end - sh, :]
            if lv < levels:
                dst[:, r0:end, :] = s
                src, dst = dst, (win_b if dst is win_a else win_a)
        cnt = jnp.minimum(w, pos + 1).astype(F32)
        pooled = s / cnt - ext_scr[:, HIST_PAD:end, cs]
        pool_scr[:, cs] = pooled.reshape(rows, POOL_GROUP_WIDTH).astype(BF16)
    if decode:
        a = jnp.concatenate([a_ref[pl.ds(h, rows, stride=N_HEADS), :] for h in range(N_HEADS)],
                            axis=1).astype(BF16)
    else:
        a = a_ref[...]
    m_scr[...] = jnp.dot(a, wout_ref[0:ATTN_WIDTH, :], preferred_element_type=F32)

    ys = [jnp.dot(pool_scr[:, gi * POOL_GROUP_WIDTH:(gi + 1) * POOL_GROUP_WIDTH], wpool_ref[gi],
                  preferred_element_type=F32) for gi in range(len(POOL_WINDOWS))]
    y = jnp.concatenate(ys, axis=1) * ps_ref[...]

    m_scr[...] += jnp.dot(y.astype(BF16), wout_ref[ATTN_WIDTH:, :], preferred_element_type=F32)
    _gated_residual_into(o_ref, x_ref, m_scr, gpost_ref, gate_ref)


def _mixer_out(x, a, u3, hist, mod, g_post, w_out_bf, w_pool_bf, pool_scale, *,
               nb, t, seq_tiles, decode, pos_base=0, layer=0):
    rows_total = x.shape[0]
    tm = nb * t
    nt = rows_total // tm
    out_specs = pl.BlockSpec((tm, D_MODEL), lambda i: (i, 0))
    out_shape = jax.ShapeDtypeStruct((rows_total, D_MODEL), F32)
    if decode:
        a_spec = pl.BlockSpec((tm * N_HEADS, V_DIM), lambda i: (i, 0))
        u_spec = pl.BlockSpec((nb, t, POOL_WIDTH), lambda i: (i, 0, 0))
        hist_spec = pl.BlockSpec((None, nb, POOL_HIST, POOL_WIDTH), lambda i: (layer, i, 0, 0))
        gate_spec = pl.BlockSpec((nb, D_MODEL), lambda i: (i, 2))
        out_specs = [out_specs, pl.BlockSpec((nb, POOL_HIST, POOL_WIDTH), lambda i: (i, 0, 0))]
        out_shape = [out_shape, jax.ShapeDtypeStruct((nt * nb, POOL_HIST, POOL_WIDTH), F32)]
    else:
        a_spec = pl.BlockSpec((tm, ATTN_WIDTH), lambda i: (i, 0))
        u_spec = pl.BlockSpec((1, t, POOL_WIDTH), lambda i: (i // seq_tiles, i % seq_tiles, 0))
        hpt = t // HIST_PAD
        hist_spec = pl.BlockSpec(
            (1, HIST_PAD, POOL_WIDTH),
            lambda i: (i // seq_tiles, jnp.maximum((i % seq_tiles) * hpt - 1, 0), 0))
        gate_spec = pl.BlockSpec((None, 1, D_MODEL), lambda i: (i // seq_tiles, 0, 2))
    return pl.pallas_call(
        functools.partial(_mixer_out_kernel, nb=nb, t=t, seq_tiles=seq_tiles, decode=decode,
                          pos_base=pos_base),
        grid=(nt,),
        in_specs=[
            pl.BlockSpec((tm, D_MODEL), lambda i: (i, 0)),
            a_spec, u_spec, hist_spec, gate_spec,
            pl.BlockSpec((1, D_MODEL), lambda i: (0, 0)),
            pl.BlockSpec((D_MODEL, D_MODEL), lambda i: (0, 0)),
            pl.BlockSpec((len(POOL_WINDOWS), POOL_GROUP_WIDTH, POOL_GROUP_WIDTH), lambda i: (0, 0, 0)),
            pl.BlockSpec((1, POOL_WIDTH), lambda i: (0, 0)),
        ],
        out_specs=out_specs,
        out_shape=out_shape,
        scratch_shapes=[pltpu.VMEM((nb, HIST_PAD + t, POOL_WIDTH), F32),
                        pltpu.VMEM((tm, D_MODEL), F32), pltpu.VMEM((tm, POOL_WIDTH), BF16),
                        pltpu.VMEM((nb, HIST_PAD + t, POOL_GROUP_WIDTH), F32),
                        pltpu.VMEM((nb, HIST_PAD + t, POOL_GROUP_WIDTH), F32)],
        compiler_params=_cparams(("parallel",)),
        name="mixer_out_decode" if decode else "mixer_out_prompt",
    )(x, a, u3, hist, mod, g_post.reshape(1, -1), w_out_bf, w_pool_bf, pool_scale.reshape(1, -1))


def _ffn_kernel(x_ref, shift_ref, scale_ref, gate_ref, gpre_ref, gpost_ref, w1_ref, w2_ref,
                *refs, n_f, decode):
    i = pl.program_id(0)
    j = pl.program_id(1)
    if decode:
        o_ref, w1bf_ref, w2bf_ref, h_scr, acc_scr = refs
        w1bf_ref[...] = w1_ref[...].astype(BF16)
        w2bf_ref[...] = w2_ref[...].astype(BF16)
        w1_ref, w2_ref = w1bf_ref, w2bf_ref

        @pl.when(j == 0)
        def _():
            _modulate_into(h_scr, x_ref, gpre_ref, shift_ref, scale_ref)
    else:
        xn_ref, shiftn_ref, scalen_ref, o_ref, h_scr, hb_scr, acc_scr = refs

        @pl.when((i == 0) & (j == 0))
        def _():
            _modulate_into(h_scr, x_ref, gpre_ref, shift_ref, scale_ref)

    @pl.when(j == 0)
    def _():
        acc_scr[...] = jnp.zeros(acc_scr.shape, F32)

    def step(h_cur, h_next):
        t = jnp.dot(h_cur[...], w1_ref[...], preferred_element_type=F32)
        t = jnp.square(jnp.maximum(t, 0.0))
        acc_scr[...] += jnp.dot(t.astype(BF16), w2_ref[...], preferred_element_type=F32)
        if h_next is not None:
            per_step = x_ref.shape[0] // n_f
            for r in range(0, per_step, ROW_CHUNK):
                rows = pl.ds(pl.multiple_of(j * per_step + r, ROW_CHUNK), ROW_CHUNK)
                h = (_rms(xn_ref[rows, :], gpre_ref[...]) * (1.0 + scalen_ref[...]) + shiftn_ref[...])
                h_next[rows, :] = h.astype(BF16)

    if decode:
        step(h_scr, None)
    else:
        even = lax.rem(i, 2) == 0
        pl.when(even)(functools.partial(step, h_scr, hb_scr))
        pl.when(jnp.logical_not(even))(functools.partial(step, hb_scr, h_scr))

    @pl.when(j == n_f - 1)
    def _():
        _gated_residual_into(o_ref, x_ref, acc_scr, gpost_ref, gate_ref)


def _ffn(x, mod, g_pre, g_post, w1, w2, *, tm, tf, seq_tiles, decode):
    rows = x.shape[0]
    nt = rows // tm
    n_f = D_FF // tf
    row_mode = dict(pipeline_mode=pl.Buffered(1)) if nt == 1 else {}
    x_spec = pl.BlockSpec((tm, D_MODEL), lambda i, j: (i, 0), **row_mode)
    out_specs = pl.BlockSpec((tm, D_MODEL), lambda i, j: (i, 0), **row_mode)
    out_shape = jax.ShapeDtypeStruct((rows, D_MODEL), F32)
    h_buf = pltpu.VMEM((tm, D_MODEL), BF16)
    scratch = ([h_buf] if decode else [h_buf, h_buf]) + [pltpu.VMEM((tm, D_MODEL), F32)]
    extra_in, extra_args = [], []
    if decode:
        assert nt == 1 and W1_BLOCK % tf == 0
        mod_spec = lambda c: pl.BlockSpec((tm // SUBLANES, D_MODEL), lambda i, j: (i, c))
        w1_spec = pl.BlockSpec((D_MODEL, tf), lambda i, j: (0, j))
        per = W1_BLOCK // tf
        out_specs = [out_specs,
                     pl.BlockSpec((None, D_MODEL, tf), lambda i, j: (j // per, 0, j % per)),
                     pl.BlockSpec((tf, D_MODEL), lambda i, j: (j, 0))]
        out_shape = [out_shape, jax.ShapeDtypeStruct((D_FF // W1_BLOCK, D_MODEL, W1_BLOCK), BF16),
                     jax.ShapeDtypeStruct((D_FF, D_MODEL), BF16)]
    else:
        assert tm % (n_f * ROW_CHUNK) == 0 and tf == W1_BLOCK
        w1_spec = pl.BlockSpec((None, D_MODEL, tf), lambda i, j: (j, 0, 0))
        mod_spec = lambda c: pl.BlockSpec((None, 1, D_MODEL), lambda i, j: (i // seq_tiles, 0, c))
        nxt = lambda i: jnp.minimum(i + 1, nt - 1)
        mod_next = lambda c: pl.BlockSpec((None, 1, D_MODEL), lambda i, j: (nxt(i) // seq_tiles, 0, c))
        extra_in = [pl.BlockSpec((tm, D_MODEL), lambda i, j: (nxt(i), 0)), mod_next(3), mod_next(4)]
        extra_args = [x, mod, mod]
    vec_spec = pl.BlockSpec((1, D_MODEL), lambda i, j: (0, 0))
    return pl.pallas_call(
        functools.partial(_ffn_kernel, n_f=n_f, decode=decode),
        grid=(nt, n_f),
        in_specs=[
            x_spec,
            mod_spec(3), mod_spec(4), mod_spec(5),
            vec_spec, vec_spec,
            w1_spec,
            pl.BlockSpec((tf, D_MODEL), lambda i, j: (j, 0)),
        ] + extra_in,
        out_specs=out_specs,
        out_shape=out_shape,
        scratch_shapes=scratch,
        compiler_params=_cparams(("arbitrary", "arbitrary")),
        name="ffn_decode" if decode else "ffn_prompt",
    )(x, mod, mod, mod, g_pre.reshape(1, -1), g_post.reshape(1, -1), w1, w2, *extra_args)


def kernel(x_prompt, x_sample, cache_k, cache_v, state_pool, page_table, c_prompt, c_sample,
           w_ada, b_ada, g_pre_mix, w_in, lambda_q1, lambda_k1, lambda_q2, lambda_k2, g_subln,
           w_pool, pool_scale, w_out, g_post_mix, g_pre_ffn, w_ff1, w_ff2, g_post_ffn):
    batch, seq, _ = x_prompt.shape
    dec_batch, dec_seq, _ = x_sample.shape
    depth = w_in.shape[0]
    past_len = page_table.shape[1] * PAGE_SIZE
    assert dec_seq == SUBLANES and state_pool.shape[2] == POOL_HIST

    tm_p = ROW_TILE
    seq_tiles = seq // tm_p
    rows_s = dec_batch * dec_seq
    tables_p = _rope_tables(np.arange(seq))
    tables_s = _rope_tables(past_len + (np.arange(rows_s) % dec_seq))

    n_c = batch + dec_batch
    c_all = jnp.concatenate(
        [c_sample, c_prompt, jnp.zeros((-n_c % SUBLANES, D_MODEL), F32)], axis=0)

    xp = x_prompt.reshape(batch * seq, D_MODEL)
    xs = x_sample.reshape(rows_s, D_MODEL)
    outs = [[] for _ in range(6)]
    for l in range(depth):
        lam_init = 0.8 - 0.6 * math.exp(-0.3 * l)
        lams = [v[l].reshape(1, HEAD_DIM) for v in (lambda_q1, lambda_k1, lambda_q2, lambda_k2)]
        w_out_bf = w_out[l].astype(BF16)
        w_pool_bf = w_pool[l].astype(BF16)

        m_all = _ada(c_all, w_ada[l], b_ada[l])
        mod_p = m_all[dec_batch:n_c].reshape(batch, 1, N_MOD * D_MODEL)
        mod_s = m_all

        q4s, k4s, v4s, us, w_in_bf = _mixer_in(
            xs, mod_s, g_pre_mix[l], w_in[l], tables_s, tm=tm_p, seq_tiles=1, decode=True)
        a4s = _decode(q4s, k4s, v4s, cache_k, cache_v, l, page_table, lams, g_subln[l],
                      dec_batch=dec_batch, dec_seq=dec_seq, lam_init=lam_init)
        us3 = us.reshape(dec_batch, dec_seq, POOL_WIDTH)
        xs, pool_s = _mixer_out(xs, a4s, us3, state_pool, mod_s, g_post_mix[l], w_out_bf, w_pool_bf,
                                pool_scale[l], nb=tm_p // (2 * dec_seq), t=dec_seq, seq_tiles=1,
                                decode=True, pos_base=past_len, layer=l)
        xs, w1_bf, w2_bf = _ffn(xs, mod_s, g_pre_ffn[l], g_post_ffn[l], w_ff1[l], w_ff2[l],
                                tm=rows_s, tf=FFN_DECODE_COLS, seq_tiles=1, decode=True)
        outs[3].append(k4s.reshape(dec_batch, dec_seq, N_HEADS, V_DIM))
        outs[4].append(v4s.reshape(dec_batch, dec_seq, N_HEADS, V_DIM))
        outs[5].append(pool_s)

        q_bf, k4, k_bf, v4, v_bf, u = _mixer_in(
            xp, mod_p, g_pre_mix[l], w_in_bf, tables_p, tm=tm_p, seq_tiles=seq_tiles, decode=False)
        a_bf = _flash(q_bf, k_bf, v_bf, lams, g_subln[l], batch=batch, seq=seq, lam_init=lam_init)
        u3 = u.reshape(batch, seq, POOL_WIDTH)
        xp = _mixer_out(xp, a_bf, u3, u3, mod_p, g_post_mix[l], w_out_bf, w_pool_bf, pool_scale[l],
                        nb=1, t=tm_p, seq_tiles=seq_tiles, decode=False)
        xp = _ffn(xp, mod_p, g_pre_ffn[l], g_post_ffn[l], w1_bf, w2_bf,
                  tm=tm_p, tf=W1_BLOCK, seq_tiles=seq_tiles, decode=False)
        outs[0].append(k4.reshape(batch, seq, N_HEADS, V_DIM))
        outs[1].append(v4.reshape(batch, seq, N_HEADS, V_DIM))
        outs[2].append(u3[:, seq - POOL_HIST:])

    kp, vp, sp, ks, vs, ss = (jnp.stack(o) for o in outs)
    return (xp.reshape(batch, seq, D_MODEL), xs.reshape(dec_batch, dec_seq, D_MODEL),
            kp, vp, sp, ks, vs, ss)
```

```python
import functools
import math

import jax
import jax.numpy as jnp
import numpy as np
from jax import lax
from jax.experimental import pallas as pl
from jax.experimental.pallas import tpu as pltpu

F32 = jnp.float32
BF16 = jnp.bfloat16

D_MODEL = 2048
N_HEADS = 8
HEAD_DIM = 64
V_DIM = 2 * HEAD_DIM
ATTN_WIDTH = N_HEADS * V_DIM
POOL_WIDTH = 1024
IN_WIDTH = 3 * ATTN_WIDTH + POOL_WIDTH
ROT_DIM = HEAD_DIM // 4
ROPE_THETA = 500000.0
POOL_WINDOWS = (2, 4, 8, 16)
POOL_GROUP_WIDTH = POOL_WIDTH // len(POOL_WINDOWS)
POOL_HIST = max(POOL_WINDOWS) - 1
HIST_PAD = 16
D_FF = 4 * D_MODEL
N_MOD = 6
PAGE_SIZE = 128
EPS = 1e-6
NEG_BIG = -1e30
LOG2_E = math.log2(math.e)

LANES = 128
SUBLANES = 8
BF16_ROWS = 16
VMEM_LIMIT = 60 * 1024 * 1024

ROW_TILE = 512
ROW_CHUNK = 16
ADA_COLS = 1024
FLASH_Q = 256
KV_CHUNK = 512
DECODE_PAGES = 8
RING_SLOTS = 3
W1_BLOCK = 1024
FFN_DECODE_COLS = 512


def _cparams(sem, **kwargs):
    return pltpu.CompilerParams(dimension_semantics=sem, vmem_limit_bytes=VMEM_LIMIT, **kwargs)


def _rms(x, g):
    ms = jnp.mean(x * x, axis=-1, keepdims=True)
    return x * lax.rsqrt(ms + EPS) * g


def _mod_rows(ref, r0, nrows, tile_rows):
    n, d = ref.shape
    if n == 1:
        return ref[...]
    per = tile_rows // n
    m = ref[r0 // per:(r0 + nrows) // per, :]
    return jnp.broadcast_to(m[:, None, :], (nrows // per, per, d)).reshape(nrows, d)


def _modulate_into(h_ref, x_ref, g_ref, shift_ref, scale_ref):
    tm = x_ref.shape[0]
    for r in range(0, tm, ROW_CHUNK):
        h = (_rms(x_ref[r:r + ROW_CHUNK, :], g_ref[...])
             * (1.0 + _mod_rows(scale_ref, r, ROW_CHUNK, tm)) + _mod_rows(shift_ref, r, ROW_CHUNK, tm))
        h_ref[r:r + ROW_CHUNK, :] = h.astype(BF16)


def _gated_residual_into(o_ref, x_ref, m_ref, g_ref, gate_ref):
    tm = x_ref.shape[0]
    for r in range(0, tm, ROW_CHUNK):
        rows = slice(r, r + ROW_CHUNK)
        o_ref[rows, :] = x_ref[rows, :] + _mod_rows(gate_ref, r, ROW_CHUNK, tm) * _rms(m_ref[rows, :], g_ref[...])


def _ada_kernel(c_ref, w_ref, b_ref, o_ref):
    c = c_ref[...]
    s = c * (1.0 / (1.0 + jnp.exp(-c)))
    o_ref[...] = jnp.dot(s.astype(BF16), w_ref[...].astype(BF16),
                         preferred_element_type=F32) + b_ref[...]


def _ada(c_all, w_ada, b_ada):
    rows = c_all.shape[0]
    tn = ADA_COLS
    return pl.pallas_call(
        _ada_kernel,
        grid=(N_MOD * D_MODEL // tn,),
        in_specs=[
            pl.BlockSpec((rows, D_MODEL), lambda j: (0, 0)),
            pl.BlockSpec((D_MODEL, tn), lambda j: (0, j)),
            pl.BlockSpec((1, tn), lambda j: (0, j)),
        ],
        out_specs=pl.BlockSpec((rows, tn), lambda j: (0, j)),
        out_shape=jax.ShapeDtypeStruct((rows, N_MOD * D_MODEL), F32),
        compiler_params=_cparams(("arbitrary",)),
        name="ada",
    )(c_all, w_ada, b_ada.reshape(1, -1))


def _rope(z, cos_ref, sa_ref, sb_ref):
    return (z * cos_ref[...]
            + pltpu.roll(z, LANES - ROT_DIM // 2, 1) * sa_ref[...]
            + pltpu.roll(z, ROT_DIM // 2, 1) * sb_ref[...])


def _mixer_in_kernel(x_ref, shift_ref, scale_ref, g_ref, w_ref, cos_ref, sa_ref, sb_ref,
                     *refs, tm, decode):
    j = pl.program_id(1)
    if decode:
        q4_ref, k4_ref, v4_ref, u_ref, wbf_ref, h_scr, wc_scr = refs
        wc_scr[...] = w_ref[...].astype(BF16)

        @pl.when(pl.program_id(0) == 0)
        def _():
            wbf_ref[...] = wc_scr[...]
    else:
        qb_ref, k4_ref, kb_ref, v4_ref, vb_ref, u_ref, h_scr = refs

    def weights(sec, lo, hi):
        if decode:
            return wc_scr[:, lo:hi]
        return w_ref[:, sec * ATTN_WIDTH + lo:sec * ATTN_WIDTH + hi]

    def heads(sec):
        for hp in range(N_HEADS // 2):
            z2 = jnp.dot(h_scr[...], weights(sec, 2 * hp * V_DIM, 2 * (hp + 1) * V_DIM),
                         preferred_element_type=F32)
            for h in (2 * hp, 2 * hp + 1):
                yield h, slice(h * V_DIM, (h + 1) * V_DIM), z2[:, (h % 2) * V_DIM:(h % 2 + 1) * V_DIM]

    def section(sec):
        if sec == 0:
            for h, cols, zh in heads(sec):
                q = _rope(zh, cos_ref, sa_ref, sb_ref) * (HEAD_DIM ** -0.5)
                if decode:
                    q4_ref[pl.ds(h, tm, stride=N_HEADS), :] = q
                else:
                    qb_ref[:, cols] = (q * LOG2_E).astype(BF16)
        elif sec == 1:
            for h, cols, zh in heads(sec):
                k = _rope(zh, cos_ref, sa_ref, sb_ref)
                k4_ref[pl.ds(h, tm, stride=N_HEADS), :] = k
                if not decode:
                    kb_ref[:, cols] = k.astype(BF16)
        elif sec == 2:
            for h, cols, zh in heads(sec):
                v4_ref[pl.ds(h, tm, stride=N_HEADS), :] = zh
                if not decode:
                    vb_ref[:, cols] = zh.astype(BF16)
        else:
            u_ref[...] = jnp.dot(h_scr[...], weights(sec, 0, POOL_WIDTH), preferred_element_type=F32)

    n_sec = IN_WIDTH // ATTN_WIDTH
    if decode:
        pl.when(j == 0)(functools.partial(_modulate_into, h_scr, x_ref, g_ref, shift_ref, scale_ref))
        for sec in range(n_sec):
            pl.when(j == sec)(functools.partial(section, sec))
    else:
        _modulate_into(h_scr, x_ref, g_ref, shift_ref, scale_ref)
        for sec in range(n_sec):
            section(sec)


def _mixer_in(x, mod, g_pre, w_in, tables, *, tm, seq_tiles, decode):
    rows = x.shape[0]
    nt = rows // tm
    if decode:
        mod_spec = lambda c: pl.BlockSpec((tm // SUBLANES, D_MODEL), lambda i, j: (i, c))
        tab_spec = pl.BlockSpec((tm, LANES), lambda i, j: (i, 0))
    else:
        mod_spec = lambda c: pl.BlockSpec((None, 1, D_MODEL), lambda i, j: (i // seq_tiles, 0, c))
        tab_spec = pl.BlockSpec((tm, LANES), lambda i, j: (i % seq_tiles, 0))
    row_bf = pl.BlockSpec((tm, ATTN_WIDTH), lambda i, j: (i, 0))
    row4 = pl.BlockSpec((tm * N_HEADS, V_DIM), lambda i, j: (i, 0))
    sd_bf = jax.ShapeDtypeStruct((rows, ATTN_WIDTH), BF16)
    sd4 = jax.ShapeDtypeStruct((rows * N_HEADS, V_DIM), F32)
    sd_u = jax.ShapeDtypeStruct((rows, POOL_WIDTH), F32)
    n_sec = IN_WIDTH // ATTN_WIDTH
    scratch = [pltpu.VMEM((tm, D_MODEL), BF16)]
    if decode:
        w_out_spec = pl.BlockSpec((D_MODEL, ATTN_WIDTH),
                                  lambda i, j: (0, jnp.where(i == 0, j, n_sec - 1)))
        out_specs = [row4, row4, row4, row_bf, w_out_spec]
        out_shape = [sd4, sd4, sd4, sd_u, jax.ShapeDtypeStruct((D_MODEL, IN_WIDTH), BF16)]
        scratch.append(pltpu.VMEM((D_MODEL, ATTN_WIDTH), BF16))
        steps = n_sec
        w_spec = pl.BlockSpec((D_MODEL, ATTN_WIDTH), lambda i, j: (0, j))
    else:
        out_specs = [row_bf, row4, row_bf, row4, row_bf, row_bf]
        out_shape = [sd_bf, sd4, sd_bf, sd4, sd_bf, sd_u]
        steps = 1
        w_spec = pl.BlockSpec((D_MODEL, IN_WIDTH), lambda i, j: (0, 0), pipeline_mode=pl.Buffered(1))
    return pl.pallas_call(
        functools.partial(_mixer_in_kernel, tm=tm, decode=decode),
        grid=(nt, steps),
        in_specs=[
            pl.BlockSpec((tm, D_MODEL), lambda i, j: (i, 0)),
            mod_spec(0), mod_spec(1),
            pl.BlockSpec((1, D_MODEL), lambda i, j: (0, 0)),
            w_spec,
            tab_spec, tab_spec, tab_spec,
        ],
        out_specs=out_specs,
        out_shape=out_shape,
        scratch_shapes=scratch,
        compiler_params=_cparams(("arbitrary", "arbitrary")),
        name="mixer_in_decode" if decode else "mixer_in_prompt",
    )(x, mod, mod, g_pre.reshape(1, -1), w_in, *tables)


def _rope_tables(pos):
    half = ROT_DIM // 2
    inv = 1.0 / (ROPE_THETA ** (np.arange(0, ROT_DIM, 2, dtype=np.float64) / ROT_DIM))
    ang = np.asarray(pos, np.float64)[:, None] * inv[None, :]
    cos, sin = np.cos(ang), np.sin(ang)
    lane = np.arange(LANES) % HEAD_DIM
    first = (lane < half)[None, :]
    second = ((lane >= half) & (lane < ROT_DIM))[None, :]
    cos_l = cos[:, lane % half]
    sin_l = sin[:, lane % half]
    c = np.where(first | second, cos_l, 1.0)
    sa = np.where(first, -sin_l, 0.0)
    sb = np.where(second, sin_l, 0.0)
    return tuple(jnp.asarray(t, F32) for t in (c, sa, sb))


def _lam(lq1, lk1, lq2, lk2, lam_init):
    a = jnp.sum(lq1[...] * lk1[...], axis=-1, keepdims=True)
    b = jnp.sum(lq2[...] * lk2[...], axis=-1, keepdims=True)
    return jnp.exp(a) - jnp.exp(b) + lam_init


def _flash_kernel(q_ref, k_ref, v_ref, lq1, lk1, lq2, lk2, g_ref, o_ref,
                  vt_scr, s_scr, *, tq, nq, lam_init):
    seq = nq * tq
    vq = 2 * tq
    vt_scr[0:V_DIM, :] = v_ref[...].astype(F32).T.astype(BF16)
    vt_scr[V_DIM:, :] = jnp.ones((BF16_ROWS, seq), BF16)
    lam = _lam(lq1, lk1, lq2, lk2, lam_init)
    lane = lax.broadcasted_iota(jnp.int32, (tq, V_DIM), 1)
    kv = lax.broadcasted_iota(jnp.int32, (tq, vq), 0)
    qp = lax.broadcasted_iota(jnp.int32, (tq, vq), 1) & (tq - 1)
    causal = kv <= qp

    def stacked_queries(ii):
        q = q_ref[ii * tq:(ii + 1) * tq, :].astype(F32)
        return jnp.concatenate([jnp.where(lane < HEAD_DIM, q, 0.0),
                                jnp.where(lane >= HEAD_DIM, q, 0.0)], axis=0).astype(BF16)

    def kv_chunks(ii):
        out, r = [], 0
        while r < ii * tq:
            size = min(KV_CHUNK, ii * tq - r)
            out.append((r, size, False))
            r += size
        return out + [(ii * tq, tq, True)]

    def score_pass(ii):
        qq = stacked_queries(ii)
        mx = jnp.full((SUBLANES, vq), NEG_BIG, F32)
        for r, size, masked in kv_chunks(ii):
            s = lax.dot_general(k_ref[r:r + size, :], qq, (((1,), (1,)), ((), ())),
                                preferred_element_type=F32)
            if masked:
                s = jnp.where(causal, s, NEG_BIG)
            s_scr[ii % 2, r:r + size, :] = s
            mx = jnp.maximum(mx, jnp.max(s.reshape(size // SUBLANES, SUBLANES, vq), axis=0))
        return jnp.max(mx, axis=0, keepdims=True)

    def value_pass(ii, m):
        o_t = jnp.zeros((V_DIM + BF16_ROWS, vq), F32)
        for r, size, _ in kv_chunks(ii):
            p = jnp.exp2(s_scr[ii % 2, r:r + size, :] - m)
            o_t = o_t + jnp.dot(vt_scr[:, r:r + size], p.astype(BF16), preferred_element_type=F32)
        return o_t

    m = score_pass(0)
    for ii in range(nq):
        m_next = score_pass(ii + 1) if ii + 1 < nq else None
        o_t = value_pass(ii, m)
        m = m_next
        n_t = o_t[:V_DIM] / o_t[V_DIM:V_DIM + 1]
        out = (n_t[:, :tq] - lam * n_t[:, tq:]).T
        o_ref[ii * tq:(ii + 1) * tq, :] = (_rms(out, g_ref[...]) * (1.0 - lam_init)).astype(BF16)


def _flash(q_bf, k_bf, v_bf, lams, g_subln, *, batch, seq, lam_init, tq=FLASH_Q):
    nq = seq // tq
    lam_spec = pl.BlockSpec((1, HEAD_DIM), lambda b, h: (0, 0))
    head_spec = pl.BlockSpec((seq, V_DIM), lambda b, h: (b, h))
    return pl.pallas_call(
        functools.partial(_flash_kernel, tq=tq, nq=nq, lam_init=lam_init),
        grid=(batch, N_HEADS),
        in_specs=[head_spec, head_spec, head_spec, lam_spec, lam_spec, lam_spec, lam_spec,
                  pl.BlockSpec((1, V_DIM), lambda b, h: (0, 0))],
        out_specs=head_spec,
        out_shape=jax.ShapeDtypeStruct((batch * seq, ATTN_WIDTH), BF16),
        scratch_shapes=[pltpu.VMEM((V_DIM + BF16_ROWS, seq), BF16),
                        pltpu.VMEM((2, seq, 2 * tq), F32)],
        compiler_params=_cparams(("parallel", "parallel")),
        name="flash_prompt",
    )(q_bf, k_bf, v_bf, *lams, g_subln.reshape(1, -1))


def _decode_kernel(pt_ref, q_ref, kn_ref, vn_ref, ck_hbm, cv_hbm, lq1, lk1, lq2, lk2, g_ref, o_ref,
                   kbuf, vbuf, ksem, vsem, m_scr, l_scr, acc_scr, *, layer, n_pages, n_groups,
                   n_steps, dec_seq, lam_init):
    g = pl.program_id(1)
    t = pl.program_id(0) * n_groups + g
    qh = dec_seq * N_HEADS
    page_rows = PAGE_SIZE * N_HEADS

    def page_copies(step, slot):
        cps = []
        for p in range(n_pages):
            page = pt_ref[step * n_pages + p]
            cps.append(pltpu.make_async_copy(ck_hbm.at[layer, page], kbuf.at[slot, p], ksem.at[slot]))
            cps.append(pltpu.make_async_copy(cv_hbm.at[layer, page], vbuf.at[slot, p], vsem.at[slot]))
        return cps

    ahead = RING_SLOTS - 1

    @pl.when(t == 0)
    def _():
        for s in range(min(ahead, n_steps)):
            for cp in page_copies(s, s):
                cp.start()

    @pl.when(t + ahead < n_steps)
    def _():
        for cp in page_copies(t + ahead, lax.rem(t + ahead, RING_SLOTS)):
            cp.start()

    slot = lax.rem(t, RING_SLOTS)
    for cp in page_copies(t, slot):
        cp.wait()

    @pl.when(g == 0)
    def _():
        m_scr[...] = jnp.full(m_scr.shape, NEG_BIG, F32)
        l_scr[...] = jnp.zeros(l_scr.shape, F32)
        acc_scr[...] = jnp.zeros(acc_scr.shape, F32)

    sub = lax.broadcasted_iota(jnp.int32, (SUBLANES, LANES), 0)
    lane = lax.broadcasted_iota(jnp.int32, (SUBLANES, LANES), 1)
    diag = (lane & (N_HEADS - 1)) == sub
    lane_q = lax.broadcasted_iota(jnp.int32, (qh, V_DIM), 1)
    lam = _lam(lq1, lk1, lq2, lk2, lam_init)

    r = q_ref[...]
    qt = jnp.concatenate([jnp.where(lane_q < HEAD_DIM, r, 0.0),
                          jnp.where(lane_q >= HEAD_DIM, r, 0.0)], axis=0).astype(BF16)

    def scores(k_rows):
        s = lax.dot_general(k_rows.astype(BF16), qt, (((1,), (1,)), ((), ())),
                            preferred_element_type=F32)
        return s.reshape(k_rows.shape[0] // N_HEADS, N_HEADS, LANES)

    def update(s_list, v_list):
        m_prev = m_scr[...]
        m_new = m_prev
        for s in s_list:
            m_new = jnp.maximum(m_new, jnp.max(s, axis=0))
        alpha = jnp.exp(m_prev - m_new)
        m_eff = jnp.where(diag, m_new, -NEG_BIG)
        l_new = alpha * l_scr[...]
        pv = jnp.zeros((V_DIM, LANES), F32)
        for s, v in zip(s_list, v_list):
            p = jnp.exp(s - m_eff[None])
            l_new = l_new + jnp.sum(p, axis=0)
            pm = p.reshape(v.shape[0], LANES).astype(BF16)
            pv = pv + lax.dot_general(v.astype(BF16), pm, (((0,), (0,)), ((), ())),
                                      preferred_element_type=F32)
        alpha_row = jnp.sum(jnp.where(diag, alpha, 0.0), axis=0, keepdims=True)
        acc_scr[...] = acc_scr[...] * alpha_row + pv
        l_scr[...] = l_new
        m_scr[...] = m_new

    update([scores(kbuf[slot, p].reshape(page_rows, V_DIM)) for p in range(n_pages)],
           [vbuf[slot, p].reshape(page_rows, V_DIM) for p in range(n_pages)])

    @pl.when(g == n_groups - 1)
    def _():
        s_new = scores(kn_ref[...])
        pos = lax.broadcasted_iota(jnp.int32, s_new.shape, 0)
        qidx = (lax.broadcasted_iota(jnp.int32, s_new.shape, 2) & (qh - 1)) >> 3
        update([jnp.where(pos <= qidx, s_new, NEG_BIG)], [vn_ref[...]])
        l_row = jnp.sum(jnp.where(diag, l_scr[...], 0.0), axis=0, keepdims=True)
        nt = (acc_scr[...] / l_row).T
        out = nt[:qh] - lam * nt[qh:]
        o_ref[...] = _rms(out, g_ref[...]) * (1.0 - lam_init)


def _decode(q4, k4, v4, cache_k, cache_v, layer, page_table, lams, g_subln, *,
            dec_batch, dec_seq, lam_init, n_pages=DECODE_PAGES):
    pages_per_seq = page_table.shape[1]
    assert pages_per_seq % n_pages == 0
    n_groups = pages_per_seq // n_pages
    qh = dec_seq * N_HEADS
    tok_spec = pl.BlockSpec((qh, V_DIM), lambda b, g, pt: (b, 0))
    lam_spec = pl.BlockSpec((1, HEAD_DIM), lambda b, g, pt: (0, 0))
    hbm_spec = pl.BlockSpec(memory_space=pl.ANY)
    page_buf = pltpu.VMEM((RING_SLOTS, n_pages, PAGE_SIZE, N_HEADS, V_DIM), F32)
    grid_spec = pltpu.PrefetchScalarGridSpec(
        num_scalar_prefetch=1,
        grid=(dec_batch, n_groups),
        in_specs=[tok_spec, tok_spec, tok_spec, hbm_spec, hbm_spec]
        + [lam_spec] * 4 + [pl.BlockSpec((1, V_DIM), lambda b, g, pt: (0, 0))],
        out_specs=tok_spec,
        scratch_shapes=[page_buf, page_buf,
                        pltpu.SemaphoreType.DMA((RING_SLOTS,)), pltpu.SemaphoreType.DMA((RING_SLOTS,)),
                        pltpu.VMEM((SUBLANES, LANES), F32), pltpu.VMEM((SUBLANES, LANES), F32),
                        pltpu.VMEM((V_DIM, LANES), F32)],
    )
    return pl.pallas_call(
        functools.partial(_decode_kernel, layer=layer, n_pages=n_pages, n_groups=n_groups,
                          n_steps=dec_batch * n_groups, dec_seq=dec_seq, lam_init=lam_init),
        grid_spec=grid_spec,
        out_shape=jax.ShapeDtypeStruct((dec_batch * qh, V_DIM), F32),
        compiler_params=_cparams(("arbitrary", "arbitrary")),
        name="decode_attn",
    )(page_table.reshape(-1), q4, k4, v4, cache_k, cache_v, *lams, g_subln.reshape(1, -1))


def _mixer_out_kernel(x_ref, a_ref, u_ref, hist_ref, gate_ref, gpost_ref, wout_ref, wpool_ref,
                      ps_ref, *refs, nb, t, seq_tiles, decode, pos_base):
    if decode:
        o_ref, state_ref, ext_scr, m_scr, pool_scr, win_a, win_b = refs
    else:
        o_ref, ext_scr, m_scr, pool_scr, win_a, win_b = refs
    rows = nb * t
    ext_scr[:, HIST_PAD:HIST_PAD + t, :] = u_ref[...]
    if decode:
        ext_scr[:, HIST_PAD - POOL_HIST:HIST_PAD, :] = hist_ref[...]
        state_ref[...] = ext_scr[:, HIST_PAD + t - POOL_HIST:HIST_PAD + t, :]
        base = pos_base
    else:
        first = (pl.program_id(0) % seq_tiles) == 0
        ext_scr[:, 0:HIST_PAD, :] = jnp.where(first, 0.0, hist_ref[...])
        base = (pl.program_id(0) % seq_tiles) * t

    end = HIST_PAD + t
    pos = base + lax.broadcasted_iota(jnp.int32, (1, t, POOL_GROUP_WIDTH), 1)
    if decode:
        a = jnp.concatenate([a_ref[pl.ds(h, rows, stride=N_HEADS), :] for h in range(N_HEADS)],
                            axis=1).astype(BF16)
    else:
        a = a_ref
    ys = []
    for gi, w in enumerate(POOL_WINDOWS):
        cs = slice(gi * POOL_GROUP_WIDTH, (gi + 1) * POOL_GROUP_WIDTH)
        part = jnp.dot(a[:, cs], wout_ref[cs, :], preferred_element_type=F32)
        if gi == 0:
            m_scr[...] = part
        else:
            m_scr[...] += part
        levels = w.bit_length() - 1
        lo = [HIST_PAD] * (levels + 1)
        for lv in range(levels - 1, 0, -1):
            lo[lv] = lo[lv + 1] - (1 << lv)
        src, dst = None, win_a
        for lv in range(1, levels + 1):
            sh, r0 = 1 << (lv - 1), lo[lv]
            if src is None:
                s = ext_scr[:, r0:end, cs] + ext_scr[:, r0 - sh:end - sh, cs]
            else:
                s = src[:, r0:end, :] + src[:, r0 - sh:end - sh, :]
            if lv < levels:
                dst[:, r0:end, :] = s
                src, dst = dst, (win_b if dst is win_a else win_a)
        cnt = jnp.minimum(w, pos + 1).astype(F32)
        pooled = s / cnt - ext_scr[:, HIST_PAD:end, cs]
        pool_scr[:, cs] = pooled.reshape(rows, POOL_GROUP_WIDTH).astype(BF16)
        ys.append(jnp.dot(pool_scr[:, cs], wpool_ref[gi], preferred_element_type=F32))
    y = jnp.concatenate(ys, axis=1) * ps_ref[...]

    m_scr[...] += jnp.dot(y.astype(BF16), wout_ref[ATTN_WIDTH:, :], preferred_element_type=F32)
    _gated_residual_into(o_ref, x_ref, m_scr, gpost_ref, gate_ref)


def _mixer_out(x, a, u3, hist, mod, g_post, w_out_bf, w_pool_bf, pool_scale, *,
               nb, t, seq_tiles, decode, pos_base=0, layer=0):
    rows_total = x.shape[0]
    tm = nb * t
    nt = rows_total // tm
    out_specs = pl.BlockSpec((tm, D_MODEL), lambda i: (i, 0))
    out_shape = jax.ShapeDtypeStruct((rows_total, D_MODEL), F32)
    if decode:
        a_spec = pl.BlockSpec((tm * N_HEADS, V_DIM), lambda i: (i, 0))
        u_spec = pl.BlockSpec((nb, t, POOL_WIDTH), lambda i: (i, 0, 0))
        hist_spec = pl.BlockSpec((None, nb, POOL_HIST, POOL_WIDTH), lambda i: (layer, i, 0, 0))
        gate_spec = pl.BlockSpec((nb, D_MODEL), lambda i: (i, 2))
        out_specs = [out_specs, pl.BlockSpec((nb, POOL_HIST, POOL_WIDTH), lambda i: (i, 0, 0))]
        out_shape = [out_shape, jax.ShapeDtypeStruct((nt * nb, POOL_HIST, POOL_WIDTH), F32)]
    else:
        a_spec = pl.BlockSpec((tm, ATTN_WIDTH), lambda i: (i, 0))
        u_spec = pl.BlockSpec((1, t, POOL_WIDTH), lambda i: (i // seq_tiles, i % seq_tiles, 0))
        hpt = t // HIST_PAD
        hist_spec = pl.BlockSpec(
            (1, HIST_PAD, POOL_WIDTH),
            lambda i: (i // seq_tiles, jnp.maximum((i % seq_tiles) * hpt - 1, 0), 0))
        gate_spec = pl.BlockSpec((None, 1, D_MODEL), lambda i: (i // seq_tiles, 0, 2))
    return pl.pallas_call(
        functools.partial(_mixer_out_kernel, nb=nb, t=t, seq_tiles=seq_tiles, decode=decode,
                          pos_base=pos_base),
        grid=(nt,),
        in_specs=[
            pl.BlockSpec((tm, D_MODEL), lambda i: (i, 0)),
            a_spec, u_spec, hist_spec, gate_spec,
            pl.BlockSpec((1, D_MODEL), lambda i: (0, 0)),
            pl.BlockSpec((D_MODEL, D_MODEL), lambda i: (0, 0)),
            pl.BlockSpec((len(POOL_WINDOWS), POOL_GROUP_WIDTH, POOL_GROUP_WIDTH), lambda i: (0, 0, 0)),
            pl.BlockSpec((1, POOL_WIDTH), lambda i: (0, 0)),
        ],
        out_specs=out_specs,
        out_shape=out_shape,
        scratch_shapes=[pltpu.VMEM((nb, HIST_PAD + t, POOL_WIDTH), F32),
                        pltpu.VMEM((tm, D_MODEL), F32), pltpu.VMEM((tm, POOL_WIDTH), BF16),
                        pltpu.VMEM((nb, HIST_PAD + t, POOL_GROUP_WIDTH), F32),
                        pltpu.VMEM((nb, HIST_PAD + t, POOL_GROUP_WIDTH), F32)],
        compiler_params=_cparams(("parallel",)),
        name="mixer_out_decode" if decode else "mixer_out_prompt",
    )(x, a, u3, hist, mod, g_post.reshape(1, -1), w_out_bf, w_pool_bf, pool_scale.reshape(1, -1))


def _ffn_kernel(x_ref, shift_ref, scale_ref, gate_ref, gpre_ref, gpost_ref, w1_ref, w2_ref,
                *refs, n_f, decode):
    i = pl.program_id(0)
    j = pl.program_id(1)
    if decode:
        o_ref, w1bf_ref, w2bf_ref, h_scr, acc_scr = refs
        w1bf_ref[...] = w1_ref[...].astype(BF16)
        w2bf_ref[...] = w2_ref[...].astype(BF16)
        w1_ref, w2_ref = w1bf_ref, w2bf_ref

        @pl.when(j == 0)
        def _():
            _modulate_into(h_scr, x_ref, gpre_ref, shift_ref, scale_ref)
    else:
        xn_ref, shiftn_ref, scalen_ref, o_ref, h_scr, hb_scr, acc_scr = refs

        @pl.when((i == 0) & (j == 0))
        def _():
            _modulate_into(h_scr, x_ref, gpre_ref, shift_ref, scale_ref)

    @pl.when(j == 0)
    def _():
        acc_scr[...] = jnp.zeros(acc_scr.shape, F32)

    def step(h_cur, h_next):
        t = jnp.dot(h_cur[...], w1_ref[...], preferred_element_type=F32)
        t = jnp.square(jnp.maximum(t, 0.0))
        acc_scr[...] += jnp.dot(t.astype(BF16), w2_ref[...], preferred_element_type=F32)
        if h_next is not None:
            per_step = x_ref.shape[0] // n_f
            for r in range(0, per_step, ROW_CHUNK):
                rows = pl.ds(pl.multiple_of(j * per_step + r, ROW_CHUNK), ROW_CHUNK)
                h = (_rms(xn_ref[rows, :], gpre_ref[...]) * (1.0 + scalen_ref[...]) + shiftn_ref[...])
                h_next[rows, :] = h.astype(BF16)

    if decode:
        step(h_scr, None)
    else:
        even = lax.rem(i, 2) == 0
        pl.when(even)(functools.partial(step, h_scr, hb_scr))
        pl.when(jnp.logical_not(even))(functools.partial(step, hb_scr, h_scr))

    @pl.when(j == n_f - 1)
    def _():
        _gated_residual_into(o_ref, x_ref, acc_scr, gpost_ref, gate_ref)


def _ffn(x, mod, g_pre, g_post, w1, w2, *, tm, tf, seq_tiles, decode):
    rows = x.shape[0]
    nt = rows // tm
    n_f = D_FF // tf
    row_mode = dict(pipeline_mode=pl.Buffered(1)) if nt == 1 else {}
    x_spec = pl.BlockSpec((tm, D_MODEL), lambda i, j: (i, 0), **row_mode)
    out_specs = pl.BlockSpec((tm, D_MODEL), lambda i, j: (i, 0), **row_mode)
    out_shape = jax.ShapeDtypeStruct((rows, D_MODEL), F32)
    h_buf = pltpu.VMEM((tm, D_MODEL), BF16)
    scratch = ([h_buf] if decode else [h_buf, h_buf]) + [pltpu.VMEM((tm, D_MODEL), F32)]
    extra_in, extra_args = [], []
    if decode:
        assert nt == 1 and W1_BLOCK % tf == 0
        mod_spec = lambda c: pl.BlockSpec((tm // SUBLANES, D_MODEL), lambda i, j: (i, c))
        w1_spec = pl.BlockSpec((D_MODEL, tf), lambda i, j: (0, j))
        per = W1_BLOCK // tf
        out_specs = [out_specs,
                     pl.BlockSpec((None, D_MODEL, tf), lambda i, j: (j // per, 0, j % per)),
                     pl.BlockSpec((tf, D_MODEL), lambda i, j: (j, 0))]
        out_shape = [out_shape, jax.ShapeDtypeStruct((D_FF // W1_BLOCK, D_MODEL, W1_BLOCK), BF16),
                     jax.ShapeDtypeStruct((D_FF, D_MODEL), BF16)]
    else:
        assert tm % (n_f * ROW_CHUNK) == 0 and tf == W1_BLOCK
        w1_spec = pl.BlockSpec((None, D_MODEL, tf), lambda i, j: (j, 0, 0))
        mod_spec = lambda c: pl.BlockSpec((None, 1, D_MODEL), lambda i, j: (i // seq_tiles, 0, c))
        nxt = lambda i: jnp.minimum(i + 1, nt - 1)
        mod_next = lambda c: pl.BlockSpec((None, 1, D_MODEL), lambda i, j: (nxt(i) // seq_tiles, 0, c))
        extra_in = [pl.BlockSpec((tm, D_MODEL), lambda i, j: (nxt(i), 0)), mod_next(3), mod_next(4)]
        extra_args = [x, mod, mod]
    vec_spec = pl.BlockSpec((1, D_MODEL), lambda i, j: (0, 0))
    return pl.pallas_call(
        functools.partial(_ffn_kernel, n_f=n_f, decode=decode),
        grid=(nt, n_f),
        in_specs=[
            x_spec,
            mod_spec(3), mod_spec(4), mod_spec(5),
            vec_spec, vec_spec,
            w1_spec,
            pl.BlockSpec((tf, D_MODEL), lambda i, j: (j, 0)),
        ] + extra_in,
        out_specs=out_specs,
        out_shape=out_shape,
        scratch_shapes=scratch,
        compiler_params=_cparams(("arbitrary", "arbitrary")),
        name="ffn_decode" if decode else "ffn_prompt",
    )(x, mod, mod, mod, g_pre.reshape(1, -1), g_post.reshape(1, -1), w1, w2, *extra_args)


def kernel(x_prompt, x_sample, cache_k, cache_v, state_pool, page_table, c_prompt, c_sample,
           w_ada, b_ada, g_pre_mix, w_in, lambda_q1, lambda_k1, lambda_q2, lambda_k2, g_subln,
           w_pool, pool_scale, w_out, g_post_mix, g_pre_ffn, w_ff1, w_ff2, g_post_ffn):
    batch, seq, _ = x_prompt.shape
    dec_batch, dec_seq, _ = x_sample.shape
    depth = w_in.shape[0]
    past_len = page_table.shape[1] * PAGE_SIZE
    assert dec_seq == SUBLANES and state_pool.shape[2] == POOL_HIST

    tm_p = ROW_TILE
    seq_tiles = seq // tm_p
    rows_s = dec_batch * dec_seq
    tables_p = _rope_tables(np.arange(seq))
    tables_s = _rope_tables(past_len + (np.arange(rows_s) % dec_seq))

    n_c = batch + dec_batch
    c_all = jnp.concatenate(
        [c_sample, c_prompt, jnp.zeros((-n_c % SUBLANES, D_MODEL), F32)], axis=0)

    xp = x_prompt.reshape(batch * seq, D_MODEL)
    xs = x_sample.reshape(rows_s, D_MODEL)
    outs = [[] for _ in range(6)]
    for l in range(depth):
        lam_init = 0.8 - 0.6 * math.exp(-0.3 * l)
        lams = [v[l].reshape(1, HEAD_DIM) for v in (lambda_q1, lambda_k1, lambda_q2, lambda_k2)]
        w_out_bf = w_out[l].astype(BF16)
        w_pool_bf = w_pool[l].astype(BF16)

        m_all = _ada(c_all, w_ada[l], b_ada[l])
        mod_p = m_all[dec_batch:n_c].reshape(batch, 1, N_MOD * D_MODEL)
        mod_s = m_all

        q4s, k4s, v4s, us, w_in_bf = _mixer_in(
            xs, mod_s, g_pre_mix[l], w_in[l], tables_s, tm=tm_p, seq_tiles=1, decode=True)
        a4s = _decode(q4s, k4s, v4s, cache_k, cache_v, l, page_table, lams, g_subln[l],
                      dec_batch=dec_batch, dec_seq=dec_seq, lam_init=lam_init)
        us3 = us.reshape(dec_batch, dec_seq, POOL_WIDTH)
        xs, pool_s = _mixer_out(xs, a4s, us3, state_pool, mod_s, g_post_mix[l], w_out_bf, w_pool_bf,
                                pool_scale[l], nb=tm_p // (2 * dec_seq), t=dec_seq, seq_tiles=1,
                                decode=True, pos_base=past_len, layer=l)
        xs, w1_bf, w2_bf = _ffn(xs, mod_s, g_pre_ffn[l], g_post_ffn[l], w_ff1[l], w_ff2[l],
                                tm=rows_s, tf=FFN_DECODE_COLS, seq_tiles=1, decode=True)
        outs[3].append(k4s.reshape(dec_batch, dec_seq, N_HEADS, V_DIM))
        outs[4].append(v4s.reshape(dec_batch, dec_seq, N_HEADS, V_DIM))
        outs[5].append(pool_s)

        q_bf, k4, k_bf, v4, v_bf, u = _mixer_in(
            xp, mod_p, g_pre_mix[l], w_in_bf, tables_p, tm=tm_p, seq_tiles=seq_tiles, decode=False)
        a_bf = _flash(q_bf, k_bf, v_bf, lams, g_subln[l], batch=batch, seq=seq, lam_init=lam_init)
        u3 = u.reshape(batch, seq, POOL_WIDTH)
        xp = _mixer_out(xp, a_bf, u3, u3, mod_p, g_post_mix[l], w_out_bf, w_pool_bf, pool_scale[l],
                        nb=1, t=tm_p, seq_tiles=seq_tiles, decode=False)
        xp = _ffn(xp, mod_p, g_pre_ffn[l], g_post_ffn[l], w1_bf, w2_bf,
                  tm=tm_p, tf=W1_BLOCK, seq_tiles=seq_tiles, decode=False)
        outs[0].append(k4.reshape(batch, seq, N_HEADS, V_DIM))
        outs[1].append(v4.reshape(batch, seq, N_HEADS, V_DIM))
        outs[2].append(u3[:, seq - POOL_HIST:])

    kp, vp, sp, ks, vs, ss = (jnp.stack(o) for o in outs)
    return (xp.reshape(batch, seq, D_MODEL), xs.reshape(dec_batch, dec_seq, D_MODEL),
            kp, vp, sp, ks, vs, ss)
```
